```python
import math
import jax, jax.numpy as jnp
from jax import lax
import numpy as np

D_MODEL = 2048
BATCH = 4
SEQ = 4096
DEPTH = 2

N_BRANCH = 4
BRANCH_WIDTH = D_MODEL // N_BRANCH
NORM_EPS = 1e-6
CHUNK = 64
GDN_HEADS = 4
GDN_HEAD_DIM = BRANCH_WIDTH // GDN_HEADS
GDN_CONV = 4
RWKV_HEAD_DIM = 64
RWKV_HEADS = BRANCH_WIDTH // RWKV_HEAD_DIM
RWKV_DECAY_LORA = 64
RWKV_ICLR_LORA = 64
RWKV_GATE_LORA = 128
RWKV_DECAY_SCALE = 0.606531
RWKV_LN_EPS = 64e-5
RWKV_WIDTHS = (BRANCH_WIDTH, BRANCH_WIDTH, BRANCH_WIDTH, RWKV_DECAY_LORA, RWKV_ICLR_LORA, RWKV_GATE_LORA)
RWKV_IN = sum(RWKV_WIDTHS)
RWKV_SPLITS = tuple(sum(RWKV_WIDTHS[:i + 1]) for i in range(len(RWKV_WIDTHS) - 1))
POOL_GROUPS = 4
POOL_GROUP_WIDTH = BRANCH_WIDTH // POOL_GROUPS
POOL_WINDOWS = (2, 4, 8, 16)
POOL_MAX_WINDOW = 16
GLA_HEADS = 4
GLA_KEY_DIM = BRANCH_WIDTH // 2 // GLA_HEADS
GLA_VAL_DIM = BRANCH_WIDTH // GLA_HEADS
GLA_GATE_RANK = 16
GLA_GATE_NORM = 16.0
IN_WIDTHS = (3 * BRANCH_WIDTH,
             BRANCH_WIDTH,
             GDN_HEADS,
             GDN_HEADS,
             RWKV_IN,
             BRANCH_WIDTH,
             GLA_HEADS * GLA_KEY_DIM,
             GLA_HEADS * GLA_KEY_DIM,
             BRANCH_WIDTH,
             BRANCH_WIDTH,
             GLA_GATE_RANK,
             N_BRANCH * D_MODEL)
IN_WIDTH = sum(IN_WIDTHS)
IN_SPLITS = tuple(sum(IN_WIDTHS[:i + 1]) for i in range(len(IN_WIDTHS) - 1))
D_FF = 5632
N_EXPERTS = 8
TOP_K = 2
D_FF_EXPERT = 7168
MOE_BLOCK = 512

kernel_name = 'hybrid_gdn_rwkv7_pool_gla_moe_trunk'

F32 = jnp.float32


def rmsnorm(x, g, eps=NORM_EPS):
    xf = x.astype(F32)
    y = xf * lax.rsqrt(jnp.mean(xf * xf, axis=-1, keepdims=True) + eps)
    return (y * g.astype(F32)).astype(x.dtype)


def l2norm(x, eps=1e-6):
    return x * lax.rsqrt(jnp.sum(x * x, axis=-1, keepdims=True) + eps)


def causal_depthwise_conv(x, w):
    width, ch = w.shape
    return lax.conv_general_dilated(x, w[:, None, :].astype(x.dtype), window_strides=(1,),
                                    padding=((width - 1, 0),),
                                    dimension_numbers=('NWC', 'WIO', 'NWC'),
                                    feature_group_count=ch)


def to_chunks(t):
    b, s = t.shape[:2]
    t = t.reshape((b, s // CHUNK, CHUNK) + t.shape[2:])
    return jnp.moveaxis(t, 2, 3)


def from_chunks(t):
    t = jnp.moveaxis(t, 3, 2)
    b, n, c = t.shape[:3]
    return t.reshape((b, n * c) + t.shape[3:])


def chunk_gated_delta_rule(q, k, v, beta, g):
    dk, dv = q.shape[-1], v.shape[-1]
    qc, kc, vc = to_chunks(q * dk ** -0.5), to_chunks(k), to_chunks(v)
    bc = to_chunks(beta)
    gc = jnp.cumsum(to_chunks(g), axis=-1)
    causal = jnp.tril(jnp.ones((CHUNK, CHUNK), bool))
    strict = jnp.tril(jnp.ones((CHUNK, CHUNK), bool), -1)
    diff = gc[..., :, None] - gc[..., None, :]
    decay = jnp.where(causal, jnp.exp(jnp.where(causal, diff, 0.0)), 0.0)
    kb = kc * bc[..., None]
    lower = jnp.where(strict, jnp.einsum('bnhid,bnhjd->bnhij', kb, kc) * decay, 0.0)
    rhs = jnp.concatenate([vc * bc[..., None], kb * jnp.exp(gc)[..., None]], axis=-1)
    sol = lax.linalg.triangular_solve(jnp.eye(CHUNK, dtype=F32) + lower, rhs,
                                      left_side=True, lower=True)
    u, w = sol[..., :dv], sol[..., dv:]
    attn = jnp.einsum('bnhid,bnhjd->bnhij', qc, kc) * decay
    q_dec = qc * jnp.exp(gc)[..., None]
    k_tail = kc * jnp.exp(gc[..., -1:] - gc)[..., None]
    chunk_decay = jnp.exp(gc[..., -1])

    def step(state, inp):
        u_n, w_n, a_n, qd_n, kt_n, cd_n = inp
        v_new = u_n - jnp.einsum('bhik,bhkv->bhiv', w_n, state)
        o = jnp.einsum('bhik,bhkv->bhiv', qd_n, state) + jnp.einsum('bhij,bhjv->bhiv', a_n, v_new)
        state = state * cd_n[..., None, None] + jnp.einsum('bhik,bhiv->bhkv', kt_n, v_new)
        return state, o

    b, n, h = qc.shape[:3]
    s0 = jnp.zeros((b, h, dk, dv), F32)
    xs = tuple(jnp.moveaxis(t, 1, 0) for t in (u, w, attn, q_dec, k_tail, chunk_decay))
    _, o = lax.scan(step, s0, xs)
    return from_chunks(jnp.moveaxis(o, 0, 1))


def chunk_gla(q, k, v, log_f):
    dk, dv = q.shape[-1], v.shape[-1]
    qc, kc, vc = to_chunks(q * dk ** -0.5), to_chunks(k), to_chunks(v)
    gc = jnp.cumsum(to_chunks(log_f), axis=-2)
    causal = jnp.tril(jnp.ones((CHUNK, CHUNK), bool))
    q_dec = qc * jnp.exp(gc)
    attn = jnp.einsum('bnhid,bnhjd->bnhij', q_dec, kc * jnp.exp(-gc))
    intra = jnp.einsum('bnhij,bnhjv->bnhiv', jnp.where(causal, attn, 0.0), vc)
    g_last = gc[..., -1:, :]
    k_tail = kc * jnp.exp(g_last - gc)
    chunk_decay = jnp.exp(g_last[..., 0, :])

    def step(state, inp):
        q_n, k_n, v_n, d_n = inp
        o = jnp.einsum('bhik,bhkv->bhiv', q_n, state)
        state = state * d_n[..., None] + jnp.einsum('bhik,bhiv->bhkv', k_n, v_n)
        return state, o

    b, n, h = qc.shape[:3]
    s0 = jnp.zeros((b, h, dk, dv), F32)
    xs = tuple(jnp.moveaxis(t, 1, 0) for t in (q_dec, k_tail, vc, chunk_decay))
    _, inter = lax.scan(step, s0, xs)
    return from_chunks(intra + jnp.moveaxis(inter, 0, 1))


def gated_deltanet_branch(qkv, z, b_logit, a_logit, conv_w, a_log, dt_bias, norm_g):
    bsz, s, _ = qkv.shape
    qkv = jax.nn.silu(causal_depthwise_conv(qkv.astype(F32), conv_w.astype(F32)))
    q, k, v = (t.reshape(bsz, s, GDN_HEADS, GDN_HEAD_DIM) for t in jnp.split(qkv, 3, axis=-1))
    q, k = l2norm(q), l2norm(k)
    beta = jax.nn.sigmoid(b_logit.astype(F32))
    g = -jnp.exp(a_log.astype(F32)) * jax.nn.softplus(a_logit.astype(F32) + dt_bias.astype(F32))
    o = chunk_gated_delta_rule(q, k, v, beta, g)
    gate = jax.nn.silu(z.astype(F32).reshape(bsz, s, GDN_HEADS, GDN_HEAD_DIM))
    o = rmsnorm(o, norm_g) * gate
    return o.reshape(bsz, s, BRANCH_WIDTH).astype(qkv.dtype)


def rwkv7_branch(hr, mu, w0, w_up, a0, a_up, g_up, k_k, k_a, r_k, ln_g, ln_b):
    bsz, s, _ = hr.shape
    hr = hr.astype(F32)
    prev = jnp.pad(hr, ((0, 0), (1, 0), (0, 0)))[:, :-1]
    hs = hr + (prev - hr) * mu.astype(F32)
    r, k, v, wd, ad, gd = jnp.split(hs, RWKV_SPLITS, axis=-1)
    log_w = -RWKV_DECAY_SCALE * jax.nn.sigmoid(w0 + jnp.tanh(wd) @ w_up.astype(F32))
    a_lr = jax.nn.sigmoid(a0 + ad @ a_up.astype(F32))
    gate = jax.nn.sigmoid(gd) @ g_up.astype(F32)
    heads = lambda t: t.reshape(bsz, s, RWKV_HEADS, RWKV_HEAD_DIM)
    kk = l2norm(heads(k * k_k))
    k = k * (1.0 + (a_lr - 1.0) * k_a)
    r_h, k_h, v_h, a_h = heads(r), heads(k), heads(v), heads(a_lr)

    def step(state, inp):
        r_t, w_t, k_t, v_t, kk_t, a_t = inp
        sa = jnp.einsum('bhvk,bhk->bhv', state, -kk_t)
        state = (state * w_t[:, :, None, :] + sa[..., None] * (kk_t * a_t)[:, :, None, :]
                 + v_t[..., None] * k_t[:, :, None, :])
        return state, jnp.einsum('bhvk,bhk->bhv', state, r_t)

    s0 = jnp.zeros((bsz, RWKV_HEADS, RWKV_HEAD_DIM, RWKV_HEAD_DIM), F32)
    xs = tuple(jnp.moveaxis(t, 1, 0) for t in (r_h, jnp.exp(heads(log_w)), k_h, v_h, kk, a_h))
    _, y = lax.scan(step, s0, xs, unroll=8)
    y = jnp.moveaxis(y, 0, 1)
    mean = jnp.mean(y, axis=-1, keepdims=True)
    var = jnp.mean(jnp.square(y - mean), axis=-1, keepdims=True)
    y = ((y - mean) * lax.rsqrt(var + RWKV_LN_EPS)).reshape(bsz, s, BRANCH_WIDTH)
    y = y * ln_g.astype(F32) + ln_b.astype(F32)
    bonus = jnp.sum(r_h * k_h * r_k.astype(F32), axis=-1, keepdims=True) * v_h
    y = (y + bonus.reshape(bsz, s, BRANCH_WIDTH)) * gate
    return y.astype(mu.dtype)


def pool_branch(u, pool_w, pool_scale):
    bsz, s, _ = u.shape
    uf = u.astype(F32).reshape(bsz, s, POOL_GROUPS, POOL_GROUP_WIDTH)
    cs = jnp.cumsum(uf, axis=1)
    cs_pad = jnp.pad(cs, ((0, 0), (POOL_MAX_WINDOW, 0), (0, 0), (0, 0)))
    pos = jnp.arange(s)
    outs = []
    for gi, win in enumerate(POOL_WINDOWS):
        window_sum = cs[:, :, gi] - cs_pad[:, POOL_MAX_WINDOW - win:POOL_MAX_WINDOW - win + s, gi]
        count = jnp.minimum(pos + 1, win).astype(F32)[None, :, None]
        outs.append(window_sum / count - uf[:, :, gi])
    pooled = jnp.stack(outs, axis=2)
    y = jnp.einsum('bsgc,gcd->bsgd', pooled, pool_w.astype(F32)).reshape(bsz, s, BRANCH_WIDTH)
    return (y * pool_scale.astype(F32)).astype(u.dtype)


def gla_branch(q, k, v, g, f_down, f_up, f_bias, norm_g):
    bsz, s, _ = q.shape
    q = q.astype(F32).reshape(bsz, s, GLA_HEADS, GLA_KEY_DIM)
    k = k.astype(F32).reshape(bsz, s, GLA_HEADS, GLA_KEY_DIM)
    v = v.astype(F32).reshape(bsz, s, GLA_HEADS, GLA_VAL_DIM)
    log_f = jax.nn.log_sigmoid(f_down.astype(F32) @ f_up.astype(F32) + f_bias.astype(F32)) / GLA_GATE_NORM
    o = chunk_gla(q, k, v, log_f.reshape(bsz, s, GLA_HEADS, GLA_KEY_DIM))
    o = rmsnorm(o, norm_g) * jax.nn.silu(g.astype(F32).reshape(bsz, s, GLA_HEADS, GLA_VAL_DIM))
    return o.reshape(bsz, s, BRANCH_WIDTH).astype(g.dtype)


def hybrid_mixer(hn, w_in, gdn_conv_w, gdn_a_log, gdn_dt_bias, gdn_norm_g,
                 rwkv_mu, rwkv_w0, rwkv_w_up, rwkv_a0, rwkv_a_up, rwkv_g_up, rwkv_k_k, rwkv_k_a,
                 rwkv_r_k, rwkv_ln_g, rwkv_ln_b, pool_w, pool_scale, gla_f_up, gla_f_bias, gla_norm_g,
                 gate_bias, branch_proj, w_out):
    bsz, s, d = hn.shape
    proj = hn @ w_in
    (gdn_qkv, gdn_z, gdn_b, gdn_a, rwkv_in, pool_in, gla_q, gla_k, gla_v, gla_g, gla_f,
     gate_logits) = jnp.split(proj, IN_SPLITS, axis=-1)
    ys = (
        gated_deltanet_branch(gdn_qkv, gdn_z, gdn_b, gdn_a, gdn_conv_w, gdn_a_log, gdn_dt_bias, gdn_norm_g),
        rwkv7_branch(rwkv_in, rwkv_mu, rwkv_w0, rwkv_w_up, rwkv_a0, rwkv_a_up, rwkv_g_up, rwkv_k_k,
                     rwkv_k_a, rwkv_r_k, rwkv_ln_g, rwkv_ln_b),
        pool_branch(pool_in, pool_w, pool_scale),
        gla_branch(gla_q, gla_k, gla_v, gla_g, gla_f, gla_f_up, gla_f_bias, gla_norm_g),
    )
    gates = jax.nn.sigmoid((gate_logits + gate_bias).astype(F32)).reshape(bsz, s, N_BRANCH, d)
    mixed = gates[:, :, 0] * (ys[0] @ branch_proj[0]).astype(F32)
    for i in range(1, N_BRANCH):
        mixed = mixed + gates[:, :, i] * (ys[i] @ branch_proj[i]).astype(F32)
    return mixed.astype(hn.dtype) @ w_out


def swiglu(h, w1, w3, w2):
    return (jax.nn.silu(h @ w1) * (h @ w3)) @ w2


def moe_swiglu(h, router_w, w1, w3, w2):
    bsz, s, d = h.shape
    t = bsz * s
    n_assign = t * TOP_K
    ht = h.reshape(t, d)
    logits = (ht @ router_w).astype(F32)
    top_logit, top_idx = lax.top_k(logits, TOP_K)
    top_w = jax.nn.softmax(top_logit, axis=-1)
    flat_e = top_idx.reshape(-1)
    flat_tok = jnp.arange(n_assign, dtype=jnp.int32) // TOP_K
    flat_w = top_w.reshape(-1)
    order = jnp.argsort(flat_e)
    sorted_e = flat_e[order]
    counts = jnp.bincount(flat_e, length=N_EXPERTS)
    start = jnp.cumsum(counts) - counts
    padded = (counts + MOE_BLOCK - 1) // MOE_BLOCK * MOE_BLOCK
    pad_end = jnp.cumsum(padded)
    pad_start = pad_end - padded
    dest = pad_start[sorted_e] + (jnp.arange(n_assign) - start[sorted_e])
    n_rows = (-(-n_assign // MOE_BLOCK) + N_EXPERTS) * MOE_BLOCK
    n_blocks = n_rows // MOE_BLOCK
    row_tok = jnp.full((n_rows,), t, jnp.int32).at[dest].set(flat_tok[order])
    row_w = jnp.zeros((n_rows,), F32).at[dest].set(flat_w[order])
    block_e = jnp.minimum(jnp.searchsorted(pad_end, jnp.arange(n_blocks) * MOE_BLOCK, side='right'),
                          N_EXPERTS - 1)
    h_pad = jnp.concatenate([ht, jnp.zeros((1, d), ht.dtype)], axis=0)
    xb = h_pad[row_tok].reshape(n_blocks, MOE_BLOCK, d)

    def expert_block(args):
        x_blk, e = args
        return swiglu(x_blk, w1[e], w3[e], w2[e])

    yb = lax.map(expert_block, (xb, block_e)).reshape(n_rows, d)
    out = jnp.zeros((t + 1, d), F32).at[row_tok].add(yb.astype(F32) * row_w[:, None])
    return out[:t].reshape(bsz, s, d).astype(h.dtype)


def setup_inputs(seed: int = 0) -> dict:
    key = jax.random.key(seed)
    ks = iter(jax.random.split(key, 48))
    nrm = lambda shape, scale: jax.random.normal(next(ks), shape, F32) * scale
    L = DEPTH
    n_dense = (DEPTH + 1) // 2
    n_moe = DEPTH // 2
    dt = jnp.exp(jax.random.uniform(next(ks), (L, GDN_HEADS), F32, math.log(1e-3), math.log(1e-1)))
    return {
        'x': nrm((BATCH, SEQ, D_MODEL), 1.0),
        'norm1_g': 1.0 + nrm((L, D_MODEL), 0.05),
        'w_in': nrm((L, D_MODEL, IN_WIDTH), D_MODEL ** -0.5),
        'gdn_conv_w': nrm((L, GDN_CONV, 3 * BRANCH_WIDTH), GDN_CONV ** -0.5),
        'gdn_a_log': jnp.log(jax.random.uniform(next(ks), (L, GDN_HEADS), F32, 1.0, 16.0)),
        'gdn_dt_bias': dt + jnp.log(-jnp.expm1(-dt)),
        'gdn_norm_g': 1.0 + nrm((L, GDN_HEAD_DIM), 0.05),
        'rwkv_mu': jax.random.uniform(next(ks), (L, RWKV_IN), F32, 0.0, 1.0),
        'rwkv_w0': nrm((L, BRANCH_WIDTH), 0.5),
        'rwkv_w_up': nrm((L, RWKV_DECAY_LORA, BRANCH_WIDTH), RWKV_DECAY_LORA ** -0.5),
        'rwkv_a0': nrm((L, BRANCH_WIDTH), 0.1),
        'rwkv_a_up': nrm((L, RWKV_ICLR_LORA, BRANCH_WIDTH), RWKV_ICLR_LORA ** -0.5),
        'rwkv_g_up': nrm((L, RWKV_GATE_LORA, BRANCH_WIDTH), RWKV_GATE_LORA ** -0.5),
        'rwkv_k_k': 0.85 + nrm((L, BRANCH_WIDTH), 0.05),
        'rwkv_k_a': 1.0 + nrm((L, BRANCH_WIDTH), 0.05),
        'rwkv_r_k': nrm((L, RWKV_HEADS, RWKV_HEAD_DIM), 0.1),
        'rwkv_ln_g': 1.0 + nrm((L, BRANCH_WIDTH), 0.05),
        'rwkv_ln_b': nrm((L, BRANCH_WIDTH), 0.02),
        'pool_w': nrm((L, POOL_GROUPS, POOL_GROUP_WIDTH, POOL_GROUP_WIDTH), POOL_GROUP_WIDTH ** -0.5),
        'pool_scale': 1.0 + nrm((L, BRANCH_WIDTH), 0.1),
        'gla_f_up': nrm((L, GLA_GATE_RANK, GLA_HEADS * GLA_KEY_DIM), GLA_GATE_RANK ** -0.5),
        'gla_f_bias': nrm((L, GLA_HEADS * GLA_KEY_DIM), 0.1),
        'gla_norm_g': 1.0 + nrm((L, GLA_VAL_DIM), 0.05),
        'gate_bias': nrm((L, N_BRANCH * D_MODEL), 0.1),
        'branch_proj': nrm((L, N_BRANCH, BRANCH_WIDTH, D_MODEL), BRANCH_WIDTH ** -0.5),
        'w_out': nrm((L, D_MODEL, D_MODEL), D_MODEL ** -0.5),
        'norm2_g': 1.0 + nrm((L, D_MODEL), 0.05),
        'ffn_w1': nrm((n_dense, D_MODEL, D_FF), D_MODEL ** -0.5),
        'ffn_w3': nrm((n_dense, D_MODEL, D_FF), D_MODEL ** -0.5),
        'ffn_w2': nrm((n_dense, D_FF, D_MODEL), D_FF ** -0.5),
        'moe_router': nrm((n_moe, D_MODEL, N_EXPERTS), D_MODEL ** -0.5),
        'moe_w1': nrm((n_moe, N_EXPERTS, D_MODEL, D_FF_EXPERT), D_MODEL ** -0.5),
        'moe_w3': nrm((n_moe, N_EXPERTS, D_MODEL, D_FF_EXPERT), D_MODEL ** -0.5),
        'moe_w2': nrm((n_moe, N_EXPERTS, D_FF_EXPERT, D_MODEL), D_FF_EXPERT ** -0.5),
        'final_norm_g': 1.0 + nrm((D_MODEL,), 0.05),
    }


def reference(x, norm1_g, w_in, gdn_conv_w, gdn_a_log, gdn_dt_bias, gdn_norm_g,
              rwkv_mu, rwkv_w0, rwkv_w_up, rwkv_a0, rwkv_a_up, rwkv_g_up, rwkv_k_k, rwkv_k_a,
              rwkv_r_k, rwkv_ln_g, rwkv_ln_b, pool_w, pool_scale, gla_f_up, gla_f_bias, gla_norm_g,
              gate_bias, branch_proj, w_out, norm2_g, ffn_w1, ffn_w3, ffn_w2,
              moe_router, moe_w1, moe_w3, moe_w2, final_norm_g):
    h = x
    for layer in range(DEPTH):
        hn = rmsnorm(h, norm1_g[layer])
        h = h + hybrid_mixer(hn, w_in[layer], gdn_conv_w[layer], gdn_a_log[layer], gdn_dt_bias[layer],
                             gdn_norm_g[layer], rwkv_mu[layer], rwkv_w0[layer], rwkv_w_up[layer],
                             rwkv_a0[layer], rwkv_a_up[layer], rwkv_g_up[layer], rwkv_k_k[layer],
                             rwkv_k_a[layer], rwkv_r_k[layer], rwkv_ln_g[layer], rwkv_ln_b[layer],
                             pool_w[layer], pool_scale[layer], gla_f_up[layer], gla_f_bias[layer],
                             gla_norm_g[layer], gate_bias[layer], branch_proj[layer], w_out[layer])
        hn = rmsnorm(h, norm2_g[layer])
        i = layer // 2
        if layer % 2 == 0:
            h = h + swiglu(hn, ffn_w1[i], ffn_w3[i], ffn_w2[i])
        else:
            h = h + moe_swiglu(hn, moe_router[i], moe_w1[i], moe_w3[i], moe_w2[i])
    return rmsnorm(h, final_norm_g)
```

```python
import functools

import jax
import jax.numpy as jnp
from jax import lax
from jax.experimental import pallas as pl
from jax.experimental.pallas import tpu as pltpu

F32 = jnp.float32
BF16 = jnp.bfloat16
I32 = jnp.int32

NORM_EPS = 1e-6
CHUNK = 64
N_BRANCH = 4
BRANCH_WIDTH = 512
GDN_HEADS = 4
GDN_HEAD_DIM = 128
GDN_CONV = 4
RWKV_HEAD_DIM = 64
RWKV_DECAY_LORA = 64
RWKV_ICLR_LORA = 64
RWKV_GATE_LORA = 128
RWKV_DECAY_SCALE = 0.606531
RWKV_LN_EPS = 64e-5
POOL_WINDOWS = (2, 4, 8, 16)
POOL_MAX_WINDOW = 16
GLA_HEADS = 4
GLA_KEY_DIM = 64
GLA_VAL_DIM = 128
GLA_GATE_RANK = 16
GLA_GATE_NORM = 16.0
N_EXPERTS = 8
MOE_BLOCK = 512

LANES = 128
SUBLANES = 8
VMEM_LIMIT_BYTES = 56 * 1024 * 1024

INV_BLOCK = 16


def _cparams(sem):
    return pltpu.CompilerParams(dimension_semantics=sem, vmem_limit_bytes=VMEM_LIMIT_BYTES)


def _dot(a, b):
    return jnp.dot(a, b, preferred_element_type=F32)


def _dot_nt(a, b):
    return lax.dot_general(a, b, (((1,), (1,)), ((), ())), preferred_element_type=F32)


def _dot_tn(a, b):
    return lax.dot_general(a, b, (((0,), (0,)), ((), ())), preferred_element_type=F32)


def _bf(x):
    return x.astype(BF16)


def _split3(x):
    x1 = _bf(x)
    r1 = x - x1.astype(F32)
    x2 = _bf(r1)
    x3 = _bf(r1 - x2.astype(F32))
    return x1, x2, x3


def _dot_sel_r(x, sel):
    x1, x2, x3 = _split3(x)
    return _dot(x1, sel) + _dot(x2, sel) + _dot(x3, sel)


def _dot_sel_l(sel, x):
    x1, x2, x3 = _split3(x)
    return _dot(sel, x1) + _dot(sel, x2) + _dot(sel, x3)


def _dot_hi(a, b):
    ah = _bf(a)
    al = _bf(a - ah.astype(F32))
    bh = _bf(b)
    bl = _bf(b - bh.astype(F32))
    return _dot(ah, bh) + _dot(ah, bl) + _dot(al, bh)


def _sigmoid(x):
    return jax.nn.sigmoid(x)


def _silu(x):
    return x * jax.nn.sigmoid(x)


def _softplus(x):
    return jnp.maximum(x, 0.0) + jnp.log1p(jnp.exp(-jnp.abs(x)))


def _iota2(n, m, dim):
    return lax.broadcasted_iota(I32, (n, m), dim)


def _tri_inclusive_bf16(n):
    r = _iota2(n, n, 0)
    c = _iota2(n, n, 1)
    return (c <= r).astype(BF16)


def _unit_lower_inverse(low):
    n = low.shape[0]
    nb = n // INV_BLOCK
    r = _iota2(n, n, 0)
    c = _iota2(n, n, 1)
    shift = INV_BLOCK.bit_length() - 1
    same = (r >> shift) == (c >> shift)
    eye = (r == c).astype(F32)
    a = jnp.where(same, -low, 0.0)
    off = jnp.where(same, 0.0, low)
    t = eye + a
    p = a
    k = 2
    while k < INV_BLOCK:
        p = _dot_hi(p, p)
        t = t + _dot_hi(t, p)
        k *= 2
    nmat = _dot_hi(t, off)
    acc = eye - nmat
    pw = nmat
    for j in range(2, nb):
        pw = _dot_hi(pw, nmat)
        acc = acc + pw if j % 2 == 0 else acc - pw
    return _dot_hi(acc, t)


def _rmsnorm_kernel(x_ref, g_ref, o_ref):
    x = x_ref[...]
    ms = jnp.mean(x * x, axis=-1, keepdims=True)
    o_ref[...] = ((x * lax.rsqrt(ms + NORM_EPS)) * g_ref[...]).astype(o_ref.dtype)


def _rmsnorm(x, g, out_dtype, tm=512):
    t, d = x.shape
    return pl.pallas_call(
        _rmsnorm_kernel,
        grid=(t // tm,),
        in_specs=[pl.BlockSpec((tm, d), lambda i: (i, 0)),
                  pl.BlockSpec((1, d), lambda i: (0, 0))],
        out_specs=pl.BlockSpec((tm, d), lambda i: (i, 0)),
        out_shape=jax.ShapeDtypeStruct((t, d), out_dtype),
        compiler_params=_cparams(("parallel",)),
        name="rmsnorm",
    )(x, g.reshape(1, d))


def _mm_kernel(a_ref, b_ref, o_ref):
    o_ref[...] = _dot(a_ref[...], b_ref[...]).astype(o_ref.dtype)


def _mm_res_kernel(a_ref, b_ref, r_ref, o_ref):
    o_ref[...] = (r_ref[...] + _dot(a_ref[...], b_ref[...])).astype(o_ref.dtype)


def _matmul(a, b, *, tm, tn, out_dtype=F32, residual=None, name="matmul"):
    m, k = a.shape
    n = b.shape[1]
    grid = (n // tn, m // tm)
    in_specs = [pl.BlockSpec((tm, k), lambda j, i: (i, 0)),
                pl.BlockSpec((k, tn), lambda j, i: (0, j))]
    args = [a, b]
    kern = _mm_kernel
    if residual is not None:
        in_specs.append(pl.BlockSpec((tm, tn), lambda j, i: (i, j)))
        args.append(residual)
        kern = _mm_res_kernel
    return pl.pallas_call(
        kern,
        grid=grid,
        in_specs=in_specs,
        out_specs=pl.BlockSpec((tm, tn), lambda j, i: (i, j)),
        out_shape=jax.ShapeDtypeStruct((m, n), out_dtype),
        compiler_params=_cparams(("parallel", "parallel")),
        name=name,
    )(*args)


GDN_QKV = 3 * BRANCH_WIDTH
GDN_Z_OFF = GDN_QKV
GDN_BA_OFF = GDN_QKV + BRANCH_WIDTH
GDN_WIDTH = GDN_BA_OFF + LANES


def _gdn_kernel(p_ref, cw_ref, alog_ref, dtb_ref, ng_ref, o_ref,
                xbuf, carry, ybuf, bbuf, gbuf, state, *, tile):
    s = pl.program_id(1)

    @pl.when(s == 0)
    def _():
        state[...] = jnp.zeros_like(state)
        carry[...] = jnp.zeros_like(carry)

    xbuf[0:SUBLANES, :] = carry[...]
    xbuf[SUBLANES:SUBLANES + tile, :] = p_ref[:, 0:GDN_QKV]
    cw = cw_ref[...]
    acc = xbuf[SUBLANES:SUBLANES + tile, :] * cw[GDN_CONV - 1:GDN_CONV, :]
    for j in range(GDN_CONV - 1):
        off = SUBLANES - (GDN_CONV - 1) + j
        acc = acc + xbuf[off:off + tile, :] * cw[j:j + 1, :]
    carry[...] = xbuf[tile:tile + SUBLANES, :]
    ybuf[...] = _silu(acc)

    ba = p_ref[:, GDN_BA_OFF:GDN_BA_OFF + LANES]
    bbuf[...] = _sigmoid(ba)
    gbuf[...] = -jnp.exp(alog_ref[...]) * _softplus(ba + dtb_ref[...])

    tri = _tri_inclusive_bf16(CHUNK)
    r = _iota2(CHUNK, CHUNK, 0)
    c = _iota2(CHUNK, CHUNK, 1)
    causal = c <= r
    strict = c < r
    ng = ng_ref[...]
    hd = GDN_HEAD_DIM

    def chunk_body(ci, _):
        r0 = pl.multiple_of(ci * CHUNK, CHUNK)
        rows = pl.ds(r0, CHUNK)
        yc = ybuf[rows, :]
        bet = bbuf[rows, :]
        gc = _dot_sel_l(tri, gbuf[rows, :])
        gct = gc.T
        for h in range(GDN_HEADS):
            q = yc[:, h * hd:(h + 1) * hd]
            k = yc[:, BRANCH_WIDTH + h * hd:BRANCH_WIDTH + (h + 1) * hd]
            v = yc[:, 2 * BRANCH_WIDTH + h * hd:2 * BRANCH_WIDTH + (h + 1) * hd]
            q = q * lax.rsqrt(jnp.sum(q * q, axis=-1, keepdims=True) + 1e-6) * (hd ** -0.5)
            k = k * lax.rsqrt(jnp.sum(k * k, axis=-1, keepdims=True) + 1e-6)
            b_h = bet[:, h:h + 1]
            gcol = gc[:, GDN_HEADS + h:GDN_HEADS + h + 1]
            grow = gct[GDN_HEADS + h:GDN_HEADS + h + 1, :]
            glast = gc[CHUNK - 1:CHUNK, GDN_HEADS + h:GDN_HEADS + h + 1]
            decay = jnp.where(causal, jnp.exp(jnp.where(causal, gcol - grow, 0.0)), 0.0)
            kb = k * b_h
            kbf = _bf(k)
            lower = jnp.where(strict, _dot_nt(_bf(kb), kbf) * decay, 0.0)
            eg = jnp.exp(gcol)
            rhs = jnp.concatenate([v * b_h, kb * eg], axis=-1)
            sol = _dot_hi(_unit_lower_inverse(lower), rhs)
            u = sol[:, :hd]
            w = sol[:, hd:]
            attn = _dot_nt(_bf(q), kbf) * decay
            st = state[h]
            stb = _bf(st)
            v_new = u - _dot(_bf(w), stb)
            vnb = _bf(v_new)
            o = _dot(_bf(q * eg), stb) + _dot(_bf(attn), vnb)
            k_tail = k * jnp.exp(glast - gcol)
            state[h] = st * jnp.exp(glast) + _dot_tn(_bf(k_tail), vnb)
            z = p_ref[rows, GDN_Z_OFF + h * hd:GDN_Z_OFF + (h + 1) * hd]
            on = (o * lax.rsqrt(jnp.mean(o * o, axis=-1, keepdims=True) + NORM_EPS)) * ng
            o_ref[rows, h * hd:(h + 1) * hd] = (on * _silu(z)).astype(o_ref.dtype)
        return 0

    lax.fori_loop(0, tile // CHUNK, chunk_body, 0)


def _gdn_branch(proj, conv_w, a_log, dt_bias, norm_g, *, batch, seq, tile):
    ns = seq // tile
    zeros = jnp.zeros((LANES,), F32)
    alog_p = zeros.at[GDN_HEADS:2 * GDN_HEADS].set(a_log).reshape(1, LANES)
    dtb_p = zeros.at[GDN_HEADS:2 * GDN_HEADS].set(dt_bias).reshape(1, LANES)
    const = lambda b, s: (0, 0)
    return pl.pallas_call(
        functools.partial(_gdn_kernel, tile=tile),
        grid=(batch, ns),
        in_specs=[pl.BlockSpec((tile, GDN_WIDTH), lambda b, s: (b * ns + s, 0)),
                  pl.BlockSpec((GDN_CONV, GDN_QKV), const),
                  pl.BlockSpec((1, LANES), const),
                  pl.BlockSpec((1, LANES), const),
                  pl.BlockSpec((1, GDN_HEAD_DIM), const)],
        out_specs=pl.BlockSpec((tile, BRANCH_WIDTH), lambda b, s: (b * ns + s, 0)),
        out_shape=jax.ShapeDtypeStruct((batch * seq, BRANCH_WIDTH), BF16),
        scratch_shapes=[pltpu.VMEM((tile + SUBLANES, GDN_QKV), F32),
                        pltpu.VMEM((SUBLANES, GDN_QKV), F32),
                        pltpu.VMEM((tile, GDN_QKV), F32),
                        pltpu.VMEM((tile, LANES), F32),
                        pltpu.VMEM((tile, LANES), F32),
                        pltpu.VMEM((GDN_HEADS, GDN_HEAD_DIM, GDN_HEAD_DIM), F32)],
        compiler_params=_cparams(("parallel", "arbitrary")),
        name="gdn_branch",
    )(proj, conv_w, alog_p, dtb_p, norm_g.reshape(1, GDN_HEAD_DIM))


RWKV_R_OFF = 0
RWKV_K_OFF = BRANCH_WIDTH
RWKV_V_OFF = 2 * BRANCH_WIDTH
RWKV_WD_OFF = 3 * BRANCH_WIDTH
RWKV_AD_OFF = RWKV_WD_OFF + LANES
RWKV_GD_OFF = RWKV_AD_OFF + LANES
RWKV_WIDTH = RWKV_GD_OFF + LANES
RWKV_PAIRS = BRANCH_WIDTH // LANES


def _rwkv_kernel(p_ref, mu_ref, w0_ref, wup_ref, a0_ref, aup_ref, gup_ref, kk_ref, ka_ref,
                 rk_ref, lng_ref, lnb_ref, o_ref, xbuf, carry, hsbuf, state, *, tile):
    s = pl.program_id(1)

    @pl.when(s == 0)
    def _():
        state[...] = jnp.zeros_like(state)
        carry[...] = jnp.zeros_like(carry)

    xbuf[0:SUBLANES, :] = carry[...]
    xbuf[SUBLANES:SUBLANES + tile, :] = p_ref[...]
    hr = xbuf[SUBLANES:SUBLANES + tile, :]
    prev = xbuf[SUBLANES - 1:SUBLANES - 1 + tile, :]
    carry[...] = xbuf[tile:tile + SUBLANES, :]
    hsbuf[...] = hr + (prev - hr) * mu_ref[...]

    tri = _tri_inclusive_bf16(CHUNK)
    r = _iota2(CHUNK, CHUNK, 0)
    c = _iota2(CHUNK, CHUNK, 1)
    causal = c <= r
    strict = c < r
    lane = _iota2(CHUNK, LANES, 1)
    m0 = lane < RWKV_HEAD_DIM
    rr = _iota2(LANES, LANES, 0)
    cc = _iota2(LANES, LANES, 1)
    same_head = (rr < RWKV_HEAD_DIM) == (cc < RWKV_HEAD_DIM)
    seg = same_head.astype(BF16)
    hdim = RWKV_HEAD_DIM
    inv_hd = 1.0 / RWKV_HEAD_DIM

    def chunk_body(ci, _):
        r0 = pl.multiple_of(ci * CHUNK, CHUNK)
        rows = pl.ds(r0, CHUNK)
        hs = hsbuf[rows, :]
        rv = hs[:, RWKV_R_OFF:RWKV_R_OFF + BRANCH_WIDTH]
        kv = hs[:, RWKV_K_OFF:RWKV_K_OFF + BRANCH_WIDTH]
        vv = hs[:, RWKV_V_OFF:RWKV_V_OFF + BRANCH_WIDTH]
        wd = hs[:, RWKV_WD_OFF:RWKV_WD_OFF + LANES]
        ad = hs[:, RWKV_AD_OFF:RWKV_AD_OFF + LANES]
        gd = hs[:, RWKV_GD_OFF:RWKV_GD_OFF + LANES]
        log_w = -RWKV_DECAY_SCALE * _sigmoid(w0_ref[...] + _dot(_bf(jnp.tanh(wd)), wup_ref[...]))
        a_lr = _sigmoid(a0_ref[...] + _dot(_bf(ad), aup_ref[...]))
        gate = _dot(_bf(_sigmoid(gd)), gup_ref[...])
        kkr = kv * kk_ref[...]
        kmod = kv * (1.0 + (a_lr - 1.0) * ka_ref[...])
        g = _dot_sel_l(tri, log_w)
        gp = g - log_w
        eg = jnp.exp(g)
        egn = jnp.exp(-g)
        glast = g[CHUNK - 1:CHUNK, :]
        etail = jnp.exp(glast - g)
        dec = jnp.exp(glast)
        rk = rv * kmod * rk_ref[...]
        for p in range(RWKV_PAIRS):
            sl = slice(p * LANES, (p + 1) * LANES)
            kkp = kkr[:, sl]
            kk = kkp * lax.rsqrt(_dot_sel_r(kkp * kkp, seg) + 1e-6)
            alr = a_lr[:, sl]
            b = kk * alr
            a_t = -kk * jnp.exp(gp[:, sl])
            r_t = rv[:, sl] * eg[:, sl]
            b_t = _bf(b * egn[:, sl])
            k_t = _bf(kmod[:, sl] * egn[:, sl])
            b_hat = _bf(b * etail[:, sl])
            k_hat = _bf(kmod[:, sl] * etail[:, sl])
            vb = _bf(vv[:, sl])
            lhs = _bf(jnp.concatenate([jnp.where(m0, a_t, 0.0), jnp.where(m0, 0.0, a_t),
                                       jnp.where(m0, r_t, 0.0), jnp.where(m0, 0.0, r_t)], axis=0))
            pb = _dot_nt(lhs, b_t)
            pk = _dot_nt(lhs, k_t)
            st = state[p]
            stb = _bf(st)
            r_init = _dot_nt(_bf(a_t), stb)
            y_init = _dot_nt(_bf(r_t), stb)
            us = []
            ys = []
            for j in range(2):
                a_ab = jnp.where(strict, pb[j * CHUNK:(j + 1) * CHUNK, :], 0.0)
                a_ak = jnp.where(strict, pk[j * CHUNK:(j + 1) * CHUNK, :], 0.0)
                a_rb = jnp.where(causal, pb[(2 + j) * CHUNK:(3 + j) * CHUNK, :], 0.0)
                a_rk = jnp.where(causal, pk[(2 + j) * CHUNK:(3 + j) * CHUNK, :], 0.0)
                rj = r_init + _dot(_bf(a_ak), vb)
                uj = _dot_hi(_unit_lower_inverse(-a_ab), rj)
                yj = y_init + _dot(_bf(a_rb), _bf(uj)) + _dot(_bf(a_rk), vb)
                us.append(uj)
                ys.append(yj)
            u = jnp.where(m0, us[0], us[1])
            y = jnp.where(m0, ys[0], ys[1])
            upd = _dot_tn(_bf(u), b_hat) + _dot_tn(vb, k_hat)
            state[p] = st * dec[:, sl] + jnp.where(same_head, upd, 0.0)
            mean = _dot_sel_r(y, seg) * inv_hd
            yc = y - mean
            var = _dot_sel_r(yc * yc, seg) * inv_hd
            yn = yc * lax.rsqrt(var + RWKV_LN_EPS) * lng_ref[:, sl] + lnb_ref[:, sl]
            bonus = _dot_sel_r(rk[:, sl], seg) * vv[:, sl]
            o_ref[rows, sl] = ((yn + bonus) * gate[:, sl]).astype(o_ref.dtype)
        return 0

    lax.fori_loop(0, tile // CHUNK, chunk_body, 0)


def _pad_rows(w, rows):
    return jnp.zeros((rows,) + w.shape[1:], w.dtype).at[:w.shape[0]].set(w)


def _rwkv_mu_layout(mu):
    z = jnp.zeros((LANES - RWKV_DECAY_LORA,), mu.dtype)
    o = 3 * BRANCH_WIDTH
    return jnp.concatenate([mu[:o], mu[o:o + RWKV_DECAY_LORA], z,
                            mu[o + RWKV_DECAY_LORA:o + RWKV_DECAY_LORA + RWKV_ICLR_LORA], z,
                            mu[o + RWKV_DECAY_LORA + RWKV_ICLR_LORA:]])


def _rwkv_branch(proj, mu, w0, w_up, a0, a_up, g_up, k_k, k_a, r_k, ln_g, ln_b, *, batch, seq, tile):
    ns = seq // tile
    bw = BRANCH_WIDTH
    row = lambda x: x.reshape(1, -1).astype(F32)
    const = lambda b, s: (0, 0)
    vec = pl.BlockSpec((1, bw), const)
    return pl.pallas_call(
        functools.partial(_rwkv_kernel, tile=tile),
        grid=(batch, ns),
        in_specs=[pl.BlockSpec((tile, RWKV_WIDTH), lambda b, s: (b * ns + s, 0)),
                  pl.BlockSpec((1, RWKV_WIDTH), const),
                  vec, pl.BlockSpec((LANES, bw), const),
                  vec, pl.BlockSpec((LANES, bw), const),
                  pl.BlockSpec((LANES, bw), const),
                  vec, vec, vec, vec, vec],
        out_specs=pl.BlockSpec((tile, bw), lambda b, s: (b * ns + s, 0)),
        out_shape=jax.ShapeDtypeStruct((batch * seq, bw), BF16),
        scratch_shapes=[pltpu.VMEM((tile + SUBLANES, RWKV_WIDTH), F32),
                        pltpu.VMEM((SUBLANES, RWKV_WIDTH), F32),
                        pltpu.VMEM((tile, RWKV_WIDTH), F32),
                        pltpu.VMEM((RWKV_PAIRS, LANES, LANES), F32)],
        compiler_params=_cparams(("parallel", "arbitrary")),
        name="rwkv_branch",
    )(proj, row(_rwkv_mu_layout(mu)), row(w0), _bf(_pad_rows(w_up, LANES)), row(a0),
      _bf(_pad_rows(a_up, LANES)), _bf(g_up), row(k_k), row(k_a), row(r_k), row(ln_g), row(ln_b))


def _pool_kernel(u_ref, pw_ref, ps_ref, o_ref, xbuf, sbuf, carry, *, tile):
    s = pl.program_id(1)

    @pl.when(s == 0)
    def _():
        carry[...] = jnp.zeros_like(carry)

    hist = POOL_MAX_WINDOW
    n = tile + hist
    xbuf[0:hist, :] = carry[...]
    xbuf[hist:n, :] = u_ref[...]
    carry[...] = xbuf[tile:n, :]
    pos = s * tile + _iota2(tile, LANES, 0)
    gw = LANES
    for gi, win in enumerate(POOL_WINDOWS):
        sl = slice(gi * gw, (gi + 1) * gw)
        sbuf[...] = xbuf[:, sl]
        span = 1
        while span < win:
            sbuf[span:n, :] = sbuf[span:n, :] + sbuf[0:n - span, :]
            span *= 2
        x = xbuf[hist:n, sl]
        count = jnp.minimum(pos + 1, win).astype(F32)
        pooled = sbuf[hist:n, :] / count - x
        y = _dot(_bf(pooled), pw_ref[gi])
        o_ref[:, sl] = (y * ps_ref[:, sl]).astype(o_ref.dtype)


def _pool_branch(u, pool_w, pool_scale, *, batch, seq, tile):
    ns = seq // tile
    bw = BRANCH_WIDTH
    return pl.pallas_call(
        functools.partial(_pool_kernel, tile=tile),
        grid=(batch, ns),
        in_specs=[pl.BlockSpec((tile, bw), lambda b, s: (b * ns + s, 0)),
                  pl.BlockSpec((len(POOL_WINDOWS), LANES, LANES), lambda b, s: (0, 0, 0)),
                  pl.BlockSpec((1, bw), lambda b, s: (0, 0))],
        out_specs=pl.BlockSpec((tile, bw), lambda b, s: (b * ns + s, 0)),
        out_shape=jax.ShapeDtypeStruct((batch * seq, bw), BF16),
        scratch_shapes=[pltpu.VMEM((tile + POOL_MAX_WINDOW, bw), F32),
                        pltpu.VMEM((tile + POOL_MAX_WINDOW, LANES), F32),
                        pltpu.VMEM((POOL_MAX_WINDOW, bw), F32)],
        compiler_params=_cparams(("parallel", "arbitrary")),
        name="pool_branch",
    )(u, _bf(pool_w), pool_scale.reshape(1, bw))


GLA_QW = GLA_HEADS * GLA_KEY_DIM
GLA_Q_OFF = 0
GLA_K_OFF = GLA_QW
GLA_V_OFF = 2 * GLA_QW
GLA_G_OFF = GLA_V_OFF + BRANCH_WIDTH
GLA_F_OFF = GLA_G_OFF + BRANCH_WIDTH
GLA_WIDTH = GLA_F_OFF + LANES
GLA_PAIRS = GLA_QW // LANES


def _gla_kernel(p_ref, fup_ref, fb_ref, ng_ref, o_ref, state, *, tile):
    s = pl.program_id(1)

    @pl.when(s == 0)
    def _():
        state[...] = jnp.zeros_like(state)

    tri = _tri_inclusive_bf16(CHUNK)
    r = _iota2(CHUNK, CHUNK, 0)
    c = _iota2(CHUNK, CHUNK, 1)
    causal = c <= r
    lane = _iota2(CHUNK, LANES, 1)
    m0 = lane < GLA_KEY_DIM
    col0 = _iota2(LANES, LANES, 1) < GLA_KEY_DIM
    ng = ng_ref[...]
    dv = GLA_VAL_DIM

    def chunk_body(ci, _):
        r0 = pl.multiple_of(ci * CHUNK, CHUNK)
        rows = pl.ds(r0, CHUNK)
        fl = p_ref[rows, GLA_F_OFF:GLA_F_OFF + LANES]
        logits = _dot(_bf(fl), fup_ref[...]) + fb_ref[...]
        log_f = -_softplus(-logits) / GLA_GATE_NORM
        gc = _dot_sel_l(tri, log_f)
        q = p_ref[rows, GLA_Q_OFF:GLA_Q_OFF + GLA_QW] * (GLA_KEY_DIM ** -0.5)
        k = p_ref[rows, GLA_K_OFF:GLA_K_OFF + GLA_QW]
        q_dec = q * jnp.exp(gc)
        k_dec = k * jnp.exp(-gc)
        glast = gc[CHUNK - 1:CHUNK, :]
        k_tail = k * jnp.exp(glast - gc)
        dec = jnp.exp(glast)
        for p in range(GLA_PAIRS):
            sl = slice(p * LANES, (p + 1) * LANES)
            qd = q_dec[:, sl]
            kd = _bf(k_dec[:, sl])
            kt = _bf(k_tail[:, sl])
            st = state[p]
            stb = _bf(st)
            upd = []
            for j in range(2):
                h = 2 * p + j
                qm = _bf(jnp.where(m0, qd, 0.0) if j == 0 else jnp.where(m0, 0.0, qd))
                attn = _dot_nt(qm, kd)
                vh = _bf(p_ref[rows, GLA_V_OFF + h * dv:GLA_V_OFF + (h + 1) * dv])
                o = _dot(_bf(jnp.where(causal, attn, 0.0)), vh) + _dot_nt(qm, stb)
                upd.append(_dot_tn(vh, kt))
                gate = p_ref[rows, GLA_G_OFF + h * dv:GLA_G_OFF + (h + 1) * dv]
                on = (o * lax.rsqrt(jnp.mean(o * o, axis=-1, keepdims=True) + NORM_EPS)) * ng
                o_ref[rows, h * dv:(h + 1) * dv] = (on * _silu(gate)).astype(o_ref.dtype)
            state[p] = st * dec[:, sl] + jnp.where(col0, upd[0], upd[1])
        return 0

    lax.fori_loop(0, tile // CHUNK, chunk_body, 0)


def _gla_branch(proj, f_up, f_bias, norm_g, *, batch, seq, tile):
    ns = seq // tile
    const = lambda b, s: (0, 0)
    return pl.pallas_call(
        functools.partial(_gla_kernel, tile=tile),
        grid=(batch, ns),
        in_specs=[pl.BlockSpec((tile, GLA_WIDTH), lambda b, s: (b * ns + s, 0)),
                  pl.BlockSpec((LANES, GLA_QW), const),
                  pl.BlockSpec((1, GLA_QW), const),
                  pl.BlockSpec((1, GLA_VAL_DIM), const)],
        out_specs=pl.BlockSpec((tile, BRANCH_WIDTH), lambda b, s: (b * ns + s, 0)),
        out_shape=jax.ShapeDtypeStruct((batch * seq, BRANCH_WIDTH), BF16),
        scratch_shapes=[pltpu.VMEM((GLA_PAIRS, LANES, LANES), F32)],
        compiler_params=_cparams(("parallel", "arbitrary")),
        name="gla_branch",
    )(proj, _bf(_pad_rows(f_up, LANES)), f_bias.reshape(1, GLA_QW), norm_g.reshape(1, GLA_VAL_DIM))


def _merge_kernel(y0, y1, y2, y3, bp_ref, g0, g1, g2, g3, gb_ref, o_ref):
    ys = (y0, y1, y2, y3)
    gs = (g0, g1, g2, g3)
    acc = None
    for i in range(N_BRANCH):
        term = _sigmoid(gs[i][...] + gb_ref[i]) * _dot(ys[i][...], bp_ref[i])
        acc = term if acc is None else acc + term
    o_ref[...] = acc.astype(o_ref.dtype)


def _merge(ys, branch_proj, gate_logits, gate_bias, *, tm, tn):
    t = ys[0].shape[0]
    d = branch_proj.shape[-1]
    nj = d // tn
    y_spec = pl.BlockSpec((tm, BRANCH_WIDTH), lambda j, i: (i, 0))
    g_specs = [pl.BlockSpec((tm, tn), functools.partial(lambda j, i, b: (i, b * nj + j), b=b))
               for b in range(N_BRANCH)]
    return pl.pallas_call(
        _merge_kernel,
        grid=(nj, t // tm),
        in_specs=[y_spec] * N_BRANCH
        + [pl.BlockSpec((N_BRANCH, BRANCH_WIDTH, tn), lambda j, i: (0, 0, j))]
        + g_specs
        + [pl.BlockSpec((N_BRANCH, 1, tn), lambda j, i: (0, 0, j))],
        out_specs=pl.BlockSpec((tm, tn), lambda j, i: (i, j)),
        out_shape=jax.ShapeDtypeStruct((t, d), BF16),
        compiler_params=_cparams(("parallel", "parallel")),
        name="merge",
    )(*ys, _bf(branch_proj), gate_logits, gate_logits, gate_logits, gate_logits,
      gate_bias.reshape(N_BRANCH, 1, d))


def _ffn_kernel(x_ref, w1_ref, w3_ref, w2_ref, r_ref, o_ref, acc_ref, *, nf):
    f = pl.program_id(1)

    @pl.when(f == 0)
    def _():
        acc_ref[...] = jnp.zeros_like(acc_ref)

    x = x_ref[...]
    mid = _bf(_silu(_dot(x, w1_ref[...])) * _dot(x, w3_ref[...]))
    acc_ref[...] += _dot(mid, w2_ref[...])

    @pl.when(f == nf - 1)
    def _():
        o_ref[...] = r_ref[...] + acc_ref[...]


def _ffn(hn, w1, w3, w2, resid, *, tm, tf):
    t, d = hn.shape
    ff = w1.shape[1]
    nf = ff // tf
    return pl.pallas_call(
        functools.partial(_ffn_kernel, nf=nf),
        grid=(t // tm, nf),
        in_specs=[pl.BlockSpec((tm, d), lambda i, f: (i, 0)),
                  pl.BlockSpec((d, tf), lambda i, f: (0, f)),
                  pl.BlockSpec((d, tf), lambda i, f: (0, f)),
                  pl.BlockSpec((tf, d), lambda i, f: (f, 0)),
                  pl.BlockSpec((tm, d), lambda i, f: (i, 0))],
        out_specs=pl.BlockSpec((tm, d), lambda i, f: (i, 0)),
        out_shape=jax.ShapeDtypeStruct((t, d), F32),
        scratch_shapes=[pltpu.VMEM((tm, d), F32)],
        compiler_params=_cparams(("parallel", "arbitrary")),
        name="ffn",
    )(hn, w1, w3, w2, resid)


ROUTE_E0, ROUTE_E1, ROUTE_RANK0, ROUTE_RANK1 = 0, 1, 2, 3
ROUTE_W0, ROUTE_W1 = 0, 1


def _router_kernel(hn_ref, rw_ref, ri_ref, rf_ref, cnt_ref, run):
    i = pl.program_id(0)

    @pl.when(i == 0)
    def _():
        run[...] = jnp.zeros_like(run)

    tm = hn_ref.shape[0]
    logits = _dot(hn_ref[...], rw_ref[...])
    lane = _iota2(tm, LANES, 1)
    neg = jnp.float32(-jnp.inf)
    lg = jnp.where(lane < N_EXPERTS, logits, neg)
    m1 = jnp.max(lg, axis=-1, keepdims=True)
    e0 = jnp.min(jnp.where(lg == m1, lane, LANES), axis=-1, keepdims=True)
    lg2 = jnp.where(lane == e0, neg, lg)
    m2 = jnp.max(lg2, axis=-1, keepdims=True)
    e1 = jnp.min(jnp.where(lg2 == m2, lane, LANES), axis=-1, keepdims=True)
    ex = jnp.exp(m2 - m1)
    den = 1.0 + ex
    w0 = 1.0 / den
    w1 = ex / den
    hit0 = lane == e0
    hit1 = lane == e1
    onehot = (hit0 | hit1).astype(F32)
    rr = _iota2(tm, tm, 0)
    cc = _iota2(tm, tm, 1)
    before = _dot((cc < rr).astype(BF16), _bf(onehot)) + run[...]
    rank0 = jnp.sum(jnp.where(hit0, before, 0.0), axis=-1, keepdims=True).astype(I32)
    rank1 = jnp.sum(jnp.where(hit1, before, 0.0), axis=-1, keepdims=True).astype(I32)
    run[...] += jnp.sum(onehot, axis=0, keepdims=True)
    cnt_ref[...] = run[...]
    ri_ref[...] = jnp.where(lane == ROUTE_E0, e0,
                            jnp.where(lane == ROUTE_E1, e1,
                                      jnp.where(lane == ROUTE_RANK0, rank0,
                                                jnp.where(lane == ROUTE_RANK1, rank1, 0))))
    rf_ref[...] = jnp.where(lane == ROUTE_W0, w0, jnp.where(lane == ROUTE_W1, w1, 0.0))


def _router(hn, router_w, *, tm=512):
    t, d = hn.shape
    rw = _bf(jnp.zeros((d, LANES), F32).at[:, :N_EXPERTS].set(router_w))
    return pl.pallas_call(
        _router_kernel,
        grid=(t // tm,),
        in_specs=[pl.BlockSpec((tm, d), lambda i: (i, 0)),
                  pl.BlockSpec((d, LANES), lambda i: (0, 0))],
        out_specs=[pl.BlockSpec((tm, LANES), lambda i: (i, 0)),
                   pl.BlockSpec((tm, LANES), lambda i: (i, 0)),
                   pl.BlockSpec((1, LANES), lambda i: (0, 0))],
        out_shape=[jax.ShapeDtypeStruct((t, LANES), I32),
                   jax.ShapeDtypeStruct((t, LANES), F32),
                   jax.ShapeDtypeStruct((1, LANES), F32)],
        scratch_shapes=[pltpu.VMEM((1, LANES), F32)],
        compiler_params=_cparams(("arbitrary",)),
        name="moe_router",
    )(hn, rw)


def _row_copy(src_ref, src_row, dst_ref, dst_row, sem):
    return pltpu.make_async_copy(src_ref.at[pl.ds(src_row, 1)], dst_ref.at[pl.ds(dst_row, 1)], sem)


def _dispatch_kernel(d0_ref, d1_ref, x_ref, xb_in_ref, xb_ref, sem):
    del xb_in_ref
    tm = x_ref.shape[0]
    base = pl.program_id(0) * tm

    def copies(rw):
        return (_row_copy(x_ref, rw, xb_ref, d0_ref[base + rw], sem),
                _row_copy(x_ref, rw, xb_ref, d1_ref[base + rw], sem))

    def start(rw, _):
        for cp in copies(rw):
            cp.start()
        return 0

    def wait(rw, _):
        for cp in copies(rw):
            cp.wait()
        return 0

    lax.fori_loop(0, tm, start, 0)
    lax.fori_loop(0, tm, wait, 0)


def _dispatch(x_words, dest0, dest1, n_rows, *, tm=512):
    t, w = x_words.shape
    return pl.pallas_call(
        _dispatch_kernel,
        grid_spec=pltpu.PrefetchScalarGridSpec(
            num_scalar_prefetch=2,
            grid=(t // tm,),
            in_specs=[pl.BlockSpec((tm, w), lambda i, d0, d1: (i, 0)),
                      pl.BlockSpec(memory_space=pl.ANY)],
            out_specs=pl.BlockSpec(memory_space=pl.ANY),
            scratch_shapes=[pltpu.SemaphoreType.DMA(())]),
        out_shape=jax.ShapeDtypeStruct((n_rows, w), jnp.uint32),
        input_output_aliases={3: 0},
        compiler_params=_cparams(("arbitrary",)),
        name="moe_dispatch",
    )(dest0, dest1, x_words, jnp.zeros((n_rows, w), jnp.uint32))


def _moe_ffn_kernel(be_ref, bv_ref, x_ref, w1_ref, w3_ref, w2_ref, o_ref, acc_ref, *, nf):
    del be_ref
    i = pl.program_id(0)
    f = pl.program_id(1)

    @pl.when(f == 0)
    def _():
        acc_ref[...] = jnp.zeros_like(acc_ref)

    @pl.when(bv_ref[i] > 0)
    def _():
        x = x_ref[...]
        mid = _bf(_silu(_dot(x, w1_ref[0])) * _dot(x, w3_ref[0]))
        acc_ref[...] += _dot(mid, w2_ref[0])

    @pl.when(f == nf - 1)
    def _():
        o_ref[...] = acc_ref[...]


def _moe_ffn(xb, w1, w3, w2, block_e, block_valid, *, tf):
    n_rows, d = xb.shape
    ff = w1.shape[-1]
    nf = ff // tf
    nb = n_rows // MOE_BLOCK
    return pl.pallas_call(
        functools.partial(_moe_ffn_kernel, nf=nf),
        grid_spec=pltpu.PrefetchScalarGridSpec(
            num_scalar_prefetch=2,
            grid=(nb, nf),
            in_specs=[pl.BlockSpec((MOE_BLOCK, d), lambda i, f, be, bv: (i, 0)),
                      pl.BlockSpec((1, d, tf), lambda i, f, be, bv: (be[i], 0, f)),
                      pl.BlockSpec((1, d, tf), lambda i, f, be, bv: (be[i], 0, f)),
                      pl.BlockSpec((1, tf, d), lambda i, f, be, bv: (be[i], f, 0))],
            out_specs=pl.BlockSpec((MOE_BLOCK, d), lambda i, f, be, bv: (i, 0)),
            scratch_shapes=[pltpu.VMEM((MOE_BLOCK, d), F32)]),
        out_shape=jax.ShapeDtypeStruct((n_rows, d), F32),
        compiler_params=_cparams(("parallel", "arbitrary")),
        name="moe_experts",
    )(block_e, block_valid, xb, w1, w3, w2)


def _combine_kernel(d0_ref, d1_ref, yb_ref, h_ref, rf_ref, o_ref, buf0, buf1, sem):
    tm = h_ref.shape[0]
    base = pl.program_id(0) * tm

    def copies(rw):
        return (_row_copy(yb_ref, d0_ref[base + rw], buf0, rw, sem),
                _row_copy(yb_ref, d1_ref[base + rw], buf1, rw, sem))

    def start(rw, _):
        for cp in copies(rw):
            cp.start()
        return 0

    def wait(rw, _):
        for cp in copies(rw):
            cp.wait()
        return 0

    lax.fori_loop(0, tm, start, 0)
    lax.fori_loop(0, tm, wait, 0)
    w0 = rf_ref[:, ROUTE_W0:ROUTE_W0 + 1]
    w1 = rf_ref[:, ROUTE_W1:ROUTE_W1 + 1]
    o_ref[...] = h_ref[...] + (buf0[...] * w0 + buf1[...] * w1)


def _combine(yb, h, route_f, dest0, dest1, *, tm=256):
    t, d = h.shape
    return pl.pallas_call(
        _combine_kernel,
        grid_spec=pltpu.PrefetchScalarGridSpec(
            num_scalar_prefetch=2,
            grid=(t // tm,),
            in_specs=[pl.BlockSpec(memory_space=pl.ANY),
                      pl.BlockSpec((tm, d), lambda i, d0, d1: (i, 0)),
                      pl.BlockSpec((tm, LANES), lambda i, d0, d1: (i, 0))],
            out_specs=pl.BlockSpec((tm, d), lambda i, d0, d1: (i, 0)),
            scratch_shapes=[pltpu.VMEM((tm, d), F32), pltpu.VMEM((tm, d), F32),
                            pltpu.SemaphoreType.DMA(())]),
        out_shape=jax.ShapeDtypeStruct((t, d), F32),
        compiler_params=_cparams(("arbitrary",)),
        name="moe_combine",
    )(dest0, dest1, yb, h, route_f)


def _moe(hn, h, router_w, w1, w3, w2):
    t, d = hn.shape
    route_i, route_f, counts_f = _router(hn, router_w)
    counts = counts_f[0, :N_EXPERTS].astype(I32)
    padded = (counts + MOE_BLOCK - 1) // MOE_BLOCK * MOE_BLOCK
    pad_end = jnp.cumsum(padded)
    pad_start = pad_end - padded
    n_assign = t * 2
    n_rows = (-(-n_assign // MOE_BLOCK) + N_EXPERTS) * MOE_BLOCK
    n_blocks = n_rows // MOE_BLOCK
    block_first = jnp.arange(n_blocks, dtype=I32) * MOE_BLOCK
    block_e = jnp.minimum(jnp.searchsorted(pad_end, block_first, side='right'), N_EXPERTS - 1).astype(I32)
    block_valid = (block_first < pad_start[block_e] + counts[block_e]).astype(I32)
    dest0 = pad_start[route_i[:, ROUTE_E0]] + route_i[:, ROUTE_RANK0]
    dest1 = pad_start[route_i[:, ROUTE_E1]] + route_i[:, ROUTE_RANK1]
    x_words = lax.bitcast_convert_type(hn.reshape(t, d // 2, 2), jnp.uint32)
    xb_words = _dispatch(x_words, dest0, dest1, n_rows)
    xb = lax.bitcast_convert_type(xb_words, BF16).reshape(n_rows, d)
    yb = _moe_ffn(xb, _bf(w1), _bf(w3), _bf(w2), block_e, block_valid, tf=512)
    return _combine(yb, h, route_f, dest0, dest1)


def _pad_cols(w, width):
    return jnp.concatenate([w, jnp.zeros((w.shape[0], width - w.shape[1]), w.dtype)], axis=1)


def _split_w_in(w_in):
    bw = BRANCH_WIDTH
    d = w_in.shape[0]
    widths = (3 * bw, bw, GDN_HEADS, GDN_HEADS, 3 * bw + RWKV_DECAY_LORA + RWKV_ICLR_LORA + RWKV_GATE_LORA,
              bw, GLA_QW, GLA_QW, bw, bw, GLA_GATE_RANK, N_BRANCH * d)
    offs = [0]
    for w in widths:
        offs.append(offs[-1] + w)
    col = lambda i: w_in[:, offs[i]:offs[i + 1]]
    gdn = jnp.concatenate([col(0), col(1), _pad_cols(jnp.concatenate([col(2), col(3)], axis=1), LANES)], axis=1)
    rw = col(4)
    o = 3 * bw
    rwkv = jnp.concatenate([rw[:, :o],
                            _pad_cols(rw[:, o:o + RWKV_DECAY_LORA], LANES),
                            _pad_cols(rw[:, o + RWKV_DECAY_LORA:o + RWKV_DECAY_LORA + RWKV_ICLR_LORA], LANES),
                            rw[:, o + RWKV_DECAY_LORA + RWKV_ICLR_LORA:]], axis=1)
    pool = col(5)
    gla = jnp.concatenate([col(6), col(7), col(8), col(9), _pad_cols(col(10), LANES)], axis=1)
    gate = col(11)
    return tuple(_bf(w) for w in (gdn, rwkv, pool, gla, gate))


def _mixer(hn, h, layer, batch, seq, w_in, gdn_conv_w, gdn_a_log, gdn_dt_bias, gdn_norm_g,
           rwkv_mu, rwkv_w0, rwkv_w_up, rwkv_a0, rwkv_a_up, rwkv_g_up, rwkv_k_k, rwkv_k_a,
           rwkv_r_k, rwkv_ln_g, rwkv_ln_b, pool_w, pool_scale, gla_f_up, gla_f_bias, gla_norm_g,
           gate_bias, branch_proj, w_out):
    l = layer
    w_gdn, w_rwkv, w_pool, w_gla, w_gate = _split_w_in(w_in[l])
    tm = 512
    p_gdn = _matmul(hn, w_gdn, tm=tm, tn=GDN_WIDTH, name="proj_gdn")
    p_rwkv = _matmul(hn, w_rwkv, tm=tm, tn=RWKV_WIDTH, name="proj_rwkv")
    p_pool = _matmul(hn, w_pool, tm=tm, tn=BRANCH_WIDTH, name="proj_pool")
    p_gla = _matmul(hn, w_gla, tm=tm, tn=GLA_WIDTH, name="proj_gla")
    p_gate = _matmul(hn, w_gate, tm=tm, tn=1024, name="proj_gate")
    seq_tile = 256
    kw = dict(batch=batch, seq=seq, tile=seq_tile)
    y_gdn = _gdn_branch(p_gdn, gdn_conv_w[l], gdn_a_log[l], gdn_dt_bias[l], gdn_norm_g[l], **kw)
    y_rwkv = _rwkv_branch(p_rwkv, rwkv_mu[l], rwkv_w0[l], rwkv_w_up[l], rwkv_a0[l], rwkv_a_up[l],
                          rwkv_g_up[l], rwkv_k_k[l], rwkv_k_a[l], rwkv_r_k[l].reshape(-1),
                          rwkv_ln_g[l], rwkv_ln_b[l], **kw)
    y_pool = _pool_branch(p_pool, pool_w[l], pool_scale[l], **kw)
    y_gla = _gla_branch(p_gla, gla_f_up[l], gla_f_bias[l], gla_norm_g[l], **kw)
    mixed = _merge((y_gdn, y_rwkv, y_pool, y_gla), branch_proj[l], p_gate, gate_bias[l], tm=512, tn=512)
    return _matmul(mixed, _bf(w_out[l]), tm=512, tn=1024, residual=h, name="out_proj")


def kernel(x, norm1_g, w_in, gdn_conv_w, gdn_a_log, gdn_dt_bias, gdn_norm_g, rwkv_mu, rwkv_w0, rwkv_w_up, rwkv_a0, rwkv_a_up, rwkv_g_up, rwkv_k_k, rwkv_k_a, rwkv_r_k, rwkv_ln_g, rwkv_ln_b, pool_w, pool_scale, gla_f_up, gla_f_bias, gla_norm_g, gate_bias, branch_proj, w_out, norm2_g, ffn_w1, ffn_w3, ffn_w2, moe_router, moe_w1, moe_w3, moe_w2, final_norm_g):
    batch, seq, d = x.shape
    depth = norm1_g.shape[0]
    h = x.reshape(batch * seq, d)
    for layer in range(depth):
        hn = _rmsnorm(h, norm1_g[layer], BF16)
        h = _mixer(hn, h, layer, batch, seq, w_in, gdn_conv_w, gdn_a_log, gdn_dt_bias, gdn_norm_g,
                   rwkv_mu, rwkv_w0, rwkv_w_up, rwkv_a0, rwkv_a_up, rwkv_g_up, rwkv_k_k, rwkv_k_a,
                   rwkv_r_k, rwkv_ln_g, rwkv_ln_b, pool_w, pool_scale, gla_f_up, gla_f_bias,
                   gla_norm_g, gate_bias, branch_proj, w_out)
        hn = _rmsnorm(h, norm2_g[layer], BF16)
        i = layer // 2
        if layer % 2 == 0:
            h = _ffn(hn, _bf(ffn_w1[i]), _bf(ffn_w3[i]), _bf(ffn_w2[i]), h, tm=512, tf=512)
        else:
            h = _moe(hn, h, moe_router[i], moe_w1[i], moe_w3[i], moe_w2[i])
    return _rmsnorm(h, final_norm_g, F32).reshape(batch, seq, d)
```

```python
import functools

import jax
import jax.numpy as jnp
from jax import lax
from jax.experimental import pallas as pl
from jax.experimental.pallas import tpu as pltpu

F32 = jnp.float32
BF16 = jnp.bfloat16
I32 = jnp.int32

NORM_EPS = 1e-6
CHUNK = 64
N_BRANCH = 4
BRANCH_WIDTH = 512
GDN_HEADS = 4
GDN_HEAD_DIM = 128
GDN_CONV = 4
RWKV_HEAD_DIM = 64
RWKV_DECAY_LORA = 64
RWKV_ICLR_LORA = 64
RWKV_GATE_LORA = 128
RWKV_DECAY_SCALE = 0.606531
RWKV_LN_EPS = 64e-5
POOL_WINDOWS = (2, 4, 8, 16)
POOL_MAX_WINDOW = 16
GLA_HEADS = 4
GLA_KEY_DIM = 64
GLA_VAL_DIM = 128
GLA_GATE_RANK = 16
GLA_GATE_NORM = 16.0
N_EXPERTS = 8
MOE_BLOCK = 512

LANES = 128
SUBLANES = 8
VMEM_LIMIT_BYTES = 56 * 1024 * 1024

INV_BLOCK = 16
INV_HI = False


def _cparams(sem):
    return pltpu.CompilerParams(dimension_semantics=sem, vmem_limit_bytes=VMEM_LIMIT_BYTES)


def _dot(a, b):
    return jnp.dot(a, b, preferred_element_type=F32)


def _dot_nt(a, b):
    return lax.dot_general(a, b, (((1,), (1,)), ((), ())), preferred_element_type=F32)


def _dot_tn(a, b):
    return lax.dot_general(a, b, (((0,), (0,)), ((), ())), preferred_element_type=F32)


def _bf(x):
    return x.astype(BF16)


def _split_terms(x, terms):
    out = []
    rem = x
    for i in range(terms):
        xi = _bf(rem)
        out.append(xi)
        if i + 1 < terms:
            rem = rem - xi.astype(F32)
    return out


def _dot_sel_r(x, sel, terms=3):
    parts = [_dot(t, sel) for t in _split_terms(x, terms)]
    return functools.reduce(lambda a, b: a + b, parts)


def _dot_sel_l(sel, x, terms=3):
    parts = [_dot(sel, t) for t in _split_terms(x, terms)]
    return functools.reduce(lambda a, b: a + b, parts)


def _dot_hi(a, b):
    ah = _bf(a)
    al = _bf(a - ah.astype(F32))
    bh = _bf(b)
    bl = _bf(b - bh.astype(F32))
    return _dot(ah, bh) + _dot(ah, bl) + _dot(al, bh)


def _sigmoid(x):
    return jax.nn.sigmoid(x)


def _silu(x):
    return x * jax.nn.sigmoid(x)


def _softplus(x):
    return jnp.maximum(x, 0.0) + jnp.log1p(jnp.exp(-jnp.abs(x)))


def _iota2(n, m, dim):
    return lax.broadcasted_iota(I32, (n, m), dim)


def _tri_inclusive_bf16(n):
    r = _iota2(n, n, 0)
    c = _iota2(n, n, 1)
    return (c <= r).astype(BF16)


class _RowBlock:
    def __init__(self, nblk):
        self.nblk = nblk
        n = nblk * CHUNK
        self._shift = CHUNK.bit_length() - 1
        row = _iota2(CHUNK, n, 0)
        colw = _iota2(CHUNK, n, 1) & (CHUNK - 1)
        self.eye = (row == colw).astype(F32)
        self.strict = colw < row
        self.causal = colw <= row
        ish = INV_BLOCK.bit_length() - 1
        self.same_diag_block = (row >> ish) == (colw >> ish)
        self._bd = {}

    def bd_mask(self, w):
        if w not in self._bd:
            n = self.nblk * CHUNK
            rb = _iota2(n, self.nblk * w, 0) >> self._shift
            cb = _iota2(n, self.nblk * w, 1) >> (w.bit_length() - 1)
            self._bd[w] = (rb == cb).astype(BF16)
        return self._bd[w]

    def rhs(self, y, hi):
        mask = self.bd_mask(y.shape[1] // self.nblk)
        yh = _bf(y)
        bdh = jnp.concatenate([yh] * self.nblk, axis=0) * mask
        if not hi:
            return (bdh, None)
        yl = _bf(y - yh.astype(F32))
        return (bdh, jnp.concatenate([yl] * self.nblk, axis=0) * mask)


def _rb_mul(x, rhs):
    bdh, bdl = rhs
    xh = _bf(x)
    out = _dot(xh, bdh)
    if bdl is not None:
        xl = _bf(x - xh.astype(F32))
        out = out + _dot(xh, bdl) + _dot(xl, bdh)
    return out


def _rb_unit_lower_inverse(rb, lows, hi):
    a = [jnp.where(rb.same_diag_block, -l, 0.0) for l in lows]
    off = [jnp.where(rb.same_diag_block, 0.0, l) for l in lows]
    t = [rb.eye + x for x in a]
    p = a
    rp = [rb.rhs(x, hi) for x in p]
    k = 2
    while k < INV_BLOCK:
        p = [_rb_mul(x, r) for x, r in zip(p, rp)]
        rp = [rb.rhs(x, hi) for x in p]
        t = [x + _rb_mul(x, r) for x, r in zip(t, rp)]
        k *= 2
    roff = [rb.rhs(x, hi) for x in off]
    nm = [_rb_mul(x, r) for x, r in zip(t, roff)]
    acc = [rb.eye - x for x in nm]
    rn = [rb.rhs(x, hi) for x in nm]
    pw = nm
    for j in range(2, CHUNK // INV_BLOCK):
        pw = [_rb_mul(x, r) for x, r in zip(pw, rn)]
        acc = [x + y if j % 2 == 0 else x - y for x, y in zip(acc, pw)]
    rt = [rb.rhs(x, hi) for x in t]
    return [_rb_mul(x, r) for x, r in zip(acc, rt)]


def _rmsnorm_kernel(x_ref, g_ref, o_ref):
    x = x_ref[...]
    ms = jnp.mean(x * x, axis=-1, keepdims=True)
    o_ref[...] = ((x * lax.rsqrt(ms + NORM_EPS)) * g_ref[...]).astype(o_ref.dtype)


def _rmsnorm(x, g, out_dtype, tm=512):
    t, d = x.shape
    return pl.pallas_call(
        _rmsnorm_kernel,
        grid=(t // tm,),
        in_specs=[pl.BlockSpec((tm, d), lambda i: (i, 0)),
                  pl.BlockSpec((1, d), lambda i: (0, 0))],
        out_specs=pl.BlockSpec((tm, d), lambda i: (i, 0)),
        out_shape=jax.ShapeDtypeStruct((t, d), out_dtype),
        compiler_params=_cparams(("parallel",)),
        name="rmsnorm",
    )(x, g.reshape(1, d))


def _mm_kernel(a_ref, b_ref, o_ref):
    o_ref[...] = _dot(a_ref[...], b_ref[...]).astype(o_ref.dtype)


def _mm_res_kernel(a_ref, b_ref, r_ref, o_ref):
    o_ref[...] = (r_ref[...] + _dot(a_ref[...], b_ref[...])).astype(o_ref.dtype)


def _matmul(a, b, *, tm, tn, out_dtype=F32, residual=None, name="matmul"):
    m, k = a.shape
    n = b.shape[1]
    grid = (n // tn, m // tm)
    in_specs = [pl.BlockSpec((tm, k), lambda j, i: (i, 0)),
                pl.BlockSpec((k, tn), lambda j, i: (0, j))]
    args = [a, b]
    kern = _mm_kernel
    if residual is not None:
        in_specs.append(pl.BlockSpec((tm, tn), lambda j, i: (i, j)))
        args.append(residual)
        kern = _mm_res_kernel
    return pl.pallas_call(
        kern,
        grid=grid,
        in_specs=in_specs,
        out_specs=pl.BlockSpec((tm, tn), lambda j, i: (i, j)),
        out_shape=jax.ShapeDtypeStruct((m, n), out_dtype),
        compiler_params=_cparams(("parallel", "parallel")),
        name=name,
    )(*args)


GDN_QKV = 3 * BRANCH_WIDTH
GDN_Z_OFF = GDN_QKV
GDN_BA_OFF = GDN_QKV + BRANCH_WIDTH
GDN_WIDTH = GDN_BA_OFF + LANES


def _gdn_kernel(p_ref, cw_ref, alog_ref, dtb_ref, ng_ref, o_ref,
                xbuf, carry, ybuf, bbuf, gbuf, state, *, tile, nb):
    s = pl.program_id(0)

    @pl.when(s == 0)
    def _():
        state[...] = jnp.zeros_like(state)
        carry[...] = jnp.zeros_like(carry)

    cw = cw_ref[...]
    for bi in range(nb):
        xbuf[bi, 0:SUBLANES, :] = carry[bi]
        xbuf[bi, SUBLANES:SUBLANES + tile, :] = p_ref[bi, :, 0:GDN_QKV]
        acc = xbuf[bi, SUBLANES:SUBLANES + tile, :] * cw[GDN_CONV - 1:GDN_CONV, :]
        for j in range(GDN_CONV - 1):
            off = SUBLANES - (GDN_CONV - 1) + j
            acc = acc + xbuf[bi, off:off + tile, :] * cw[j:j + 1, :]
        carry[bi] = xbuf[bi, tile:tile + SUBLANES, :]
        ybuf[bi] = _silu(acc)
        ba = p_ref[bi, :, GDN_BA_OFF:GDN_BA_OFF + LANES]
        bbuf[bi] = _sigmoid(ba)
        gbuf[bi] = -jnp.exp(alog_ref[...]) * _softplus(ba + dtb_ref[...])

    tri = _tri_inclusive_bf16(CHUNK)
    rb = _RowBlock(GDN_HEADS)
    lane_blk = _iota2(CHUNK, GDN_HEADS * CHUNK, 1) >> (CHUNK.bit_length() - 1)
    ng = ng_ref[...]
    hd = GDN_HEAD_DIM
    bw = BRANCH_WIDTH
    heads = range(GDN_HEADS)
    hsl = [slice(h * hd, (h + 1) * hd) for h in heads]

    def per_head_lanes(cols):
        return jnp.concatenate([jnp.broadcast_to(c, (c.shape[0], hd)) for c in cols], axis=1)

    def per_head_blocks(cols):
        out = cols[GDN_HEADS - 1]
        for h in range(GDN_HEADS - 2, -1, -1):
            out = jnp.where(lane_blk == h, cols[h], out)
        return out

    def l2n(x):
        return jnp.concatenate(
            [x[:, s_] * lax.rsqrt(jnp.sum(x[:, s_] * x[:, s_], axis=-1, keepdims=True) + 1e-6) for s_ in hsl],
            axis=1)

    def chunk_body(ci, _):
        r0 = pl.multiple_of(ci * CHUNK, CHUNK)
        rows = pl.ds(r0, CHUNK)
        bs = range(nb)
        yc = [ybuf[bi, rows, :] for bi in bs]
        bet = [bbuf[bi, rows, :] for bi in bs]
        gc = [_dot_sel_l(tri, gbuf[bi, rows, :]) for bi in bs]
        q = [l2n(x[:, 0:bw]) * (hd ** -0.5) for x in yc]
        k = [l2n(x[:, bw:2 * bw]) for x in yc]
        v = [x[:, 2 * bw:3 * bw] for x in yc]
        gcols = [[x[:, GDN_HEADS + h:GDN_HEADS + h + 1] for h in heads] for x in gc]
        glast = [[x[CHUNK - 1:CHUNK, GDN_HEADS + h:GDN_HEADS + h + 1] for h in heads] for x in gc]
        beta_l = [per_head_lanes([x[:, h:h + 1] for h in heads]) for x in bet]
        gcol_l = [per_head_lanes(c) for c in gcols]
        gcol_b = [per_head_blocks(c) for c in gcols]
        grow_b = [jnp.sum(jnp.where(rb.eye > 0, x, 0.0), axis=0, keepdims=True) for x in gcol_b]
        decay = [jnp.where(rb.causal, jnp.exp(jnp.where(rb.causal, x - y, 0.0)), 0.0)
                 for x, y in zip(gcol_b, grow_b)]
        eg = [jnp.exp(x) for x in gcol_l]
        kb = [x * y for x, y in zip(k, beta_l)]
        lhs = [_bf(jnp.concatenate([x, y], axis=0)) for x, y in zip(kb, q)]
        kr = [rb.rhs(x, False)[0] for x in k]
        pm = [_dot_nt(x, y) for x, y in zip(lhs, kr)]
        lower = [jnp.where(rb.strict, x[:CHUNK] * d, 0.0) for x, d in zip(pm, decay)]
        attn = [x[CHUNK:] * d for x, d in zip(pm, decay)]
        minv = _rb_unit_lower_inverse(rb, lower, INV_HI)
        u = [_rb_mul(m, rb.rhs(x * y, INV_HI)) for m, x, y in zip(minv, v, beta_l)]
        w = [_rb_mul(m, rb.rhs(x * y, INV_HI)) for m, x, y in zip(minv, kb, eg)]
        qd = [x * y for x, y in zip(q, eg)]
        it = [(bi, h) for bi in bs for h in heads]
        sts = [state[bi * GDN_HEADS + h] for bi, h in it]
        stb = [_bf(x) for x in sts]
        wst = [_dot(_bf(w[bi][:, hsl[h]]), sb) for (bi, h), sb in zip(it, stb)]
        qst = [_dot(_bf(qd[bi][:, hsl[h]]), sb) for (bi, h), sb in zip(it, stb)]
        v_new = [jnp.concatenate([u[bi][:, hsl[h]] - wst[bi * GDN_HEADS + h] for h in heads], axis=1) for bi in bs]
        av = [_rb_mul(a, rb.rhs(x, False)) for a, x in zip(attn, v_new)]
        k_tail = [_bf(k[bi][:, hsl[h]] * jnp.exp(glast[bi][h] - gcols[bi][h])) for bi, h in it]
        upd = [_dot_tn(kt, _bf(v_new[bi][:, hsl[h]])) for (bi, h), kt in zip(it, k_tail)]
        for i, (bi, h) in enumerate(it):
            state[i] = sts[i] * jnp.exp(glast[bi][h]) + upd[i]
            o = qst[i] + av[bi][:, hsl[h]]
            z = p_ref[bi, rows, GDN_Z_OFF + h * hd:GDN_Z_OFF + (h + 1) * hd]
            on = (o * lax.rsqrt(jnp.mean(o * o, axis=-1, keepdims=True) + NORM_EPS)) * ng
            o_ref[bi, rows, hsl[h]] = (on * _silu(z)).astype(o_ref.dtype)
        return 0

    lax.fori_loop(0, tile // CHUNK, chunk_body, 0)


def _gdn_branch(proj, conv_w, a_log, dt_bias, norm_g, *, batch, seq, tile):
    zeros = jnp.zeros((LANES,), F32)
    alog_p = zeros.at[GDN_HEADS:2 * GDN_HEADS].set(a_log).reshape(1, LANES)
    dtb_p = zeros.at[GDN_HEADS:2 * GDN_HEADS].set(dt_bias).reshape(1, LANES)
    const = lambda s: (0, 0)
    out = pl.pallas_call(
        functools.partial(_gdn_kernel, tile=tile, nb=batch),
        grid=(seq // tile,),
        in_specs=[pl.BlockSpec((batch, tile, GDN_WIDTH), lambda s: (0, s, 0)),
                  pl.BlockSpec((GDN_CONV, GDN_QKV), const),
                  pl.BlockSpec((1, LANES), const),
                  pl.BlockSpec((1, LANES), const),
                  pl.BlockSpec((1, GDN_HEAD_DIM), const)],
        out_specs=pl.BlockSpec((batch, tile, BRANCH_WIDTH), lambda s: (0, s, 0)),
        out_shape=jax.ShapeDtypeStruct((batch, seq, BRANCH_WIDTH), BF16),
        scratch_shapes=[pltpu.VMEM((batch, tile + SUBLANES, GDN_QKV), F32),
                        pltpu.VMEM((batch, SUBLANES, GDN_QKV), F32),
                        pltpu.VMEM((batch, tile, GDN_QKV), F32),
                        pltpu.VMEM((batch, tile, LANES), F32),
                        pltpu.VMEM((batch, tile, LANES), F32),
                        pltpu.VMEM((batch * GDN_HEADS, GDN_HEAD_DIM, GDN_HEAD_DIM), F32)],
        compiler_params=_cparams(("arbitrary",)),
        name="gdn_branch",
    )(proj.reshape(batch, seq, GDN_WIDTH), conv_w, alog_p, dtb_p, norm_g.reshape(1, GDN_HEAD_DIM))
    return out.reshape(batch * seq, BRANCH_WIDTH)


RWKV_R_OFF = 0
RWKV_K_OFF = BRANCH_WIDTH
RWKV_V_OFF = 2 * BRANCH_WIDTH
RWKV_WD_OFF = 3 * BRANCH_WIDTH
RWKV_AD_OFF = RWKV_WD_OFF + LANES
RWKV_GD_OFF = RWKV_AD_OFF + LANES
RWKV_WIDTH = RWKV_GD_OFF + LANES
RWKV_GROUP_HEADS = 4
RWKV_GROUP_W = RWKV_GROUP_HEADS * RWKV_HEAD_DIM


def _rwkv_kernel(p_ref, mu_ref, w0_ref, wup_ref, a0_ref, aup_ref, gup_ref, kk_ref, ka_ref,
                 rk_ref, lng_ref, lnb_ref, o_ref, xbuf, carry, hsbuf, state, *, tile, nb):
    s = pl.program_id(0)

    @pl.when(s == 0)
    def _():
        state[...] = jnp.zeros_like(state)
        carry[...] = jnp.zeros_like(carry)

    for bi in range(nb):
        xbuf[bi, 0:SUBLANES, :] = carry[bi]
        xbuf[bi, SUBLANES:SUBLANES + tile, :] = p_ref[bi]
        hr = xbuf[bi, SUBLANES:SUBLANES + tile, :]
        prev = xbuf[bi, SUBLANES - 1:SUBLANES - 1 + tile, :]
        carry[bi] = xbuf[bi, tile:tile + SUBLANES, :]
        hsbuf[bi] = hr + (prev - hr) * mu_ref[...]

    tri = _tri_inclusive_bf16(CHUNK)
    rb = _RowBlock(RWKV_GROUP_HEADS)
    gw = RWKV_GROUP_W
    seg = rb.bd_mask(RWKV_HEAD_DIM)
    segf = seg.astype(F32)
    inv_hd = 1.0 / RWKV_HEAD_DIM
    ng = BRANCH_WIDTH // gw
    items = [(bi, gi) for bi in range(nb) for gi in range(ng)]
    sl = [slice(gi * gw, (gi + 1) * gw) for _, gi in items]
    bidx = [bi for bi, _ in items]

    def chunk_body(ci, _):
        r0 = pl.multiple_of(ci * CHUNK, CHUNK)
        rows = pl.ds(r0, CHUNK)
        kkr, kmod, a_lr, log_w, gate, vv, rv, rk = [], [], [], [], [], [], [], []
        for bi in range(nb):
            hs = hsbuf[bi, rows, :]
            rv.append(hs[:, RWKV_R_OFF:RWKV_R_OFF + BRANCH_WIDTH])
            kv = hs[:, RWKV_K_OFF:RWKV_K_OFF + BRANCH_WIDTH]
            vv.append(hs[:, RWKV_V_OFF:RWKV_V_OFF + BRANCH_WIDTH])
            wd = hs[:, RWKV_WD_OFF:RWKV_WD_OFF + LANES]
            ad = hs[:, RWKV_AD_OFF:RWKV_AD_OFF + LANES]
            gd = hs[:, RWKV_GD_OFF:RWKV_GD_OFF + LANES]
            log_w.append(-RWKV_DECAY_SCALE * _sigmoid(w0_ref[...] + _dot(_bf(jnp.tanh(wd)), wup_ref[...])))
            a_lr.append(_sigmoid(a0_ref[...] + _dot(_bf(ad), aup_ref[...])))
            gate.append(_dot(_bf(_sigmoid(gd)), gup_ref[...]))
            kkr.append(kv * kk_ref[...])
            kmod.append(kv * (1.0 + (a_lr[bi] - 1.0) * ka_ref[...]))
            rk.append(rv[bi] * kmod[bi] * rk_ref[...])
        g = [_dot_sel_l(tri, x) for x in log_w]
        egn = [jnp.exp(-x) for x in g]
        glast = [x[CHUNK - 1:CHUNK, :] for x in g]
        etail = [jnp.exp(gl - x) for gl, x in zip(glast, g)]
        dec = [jnp.exp(x) for x in glast]
        r_t = [x * jnp.exp(y) for x, y in zip(rv, g)]
        g_prev = [x - y for x, y in zip(g, log_w)]
        kkss = [_dot_sel_r(kkr[bi][:, s_] * kkr[bi][:, s_], seg, terms=2) for bi, s_ in zip(bidx, sl)]
        bon = [_dot_sel_r(rk[bi][:, s_], seg, terms=2) for bi, s_ in zip(bidx, sl)]
        kk = [kkr[bi][:, s_] * lax.rsqrt(x + 1e-6) for bi, s_, x in zip(bidx, sl, kkss)]
        b = [x * a_lr[bi][:, s_] for bi, s_, x in zip(bidx, sl, kk)]
        a_t = [-x * jnp.exp(g_prev[bi][:, s_]) for bi, s_, x in zip(bidx, sl, kk)]
        vs = [vv[bi][:, s_] for bi, s_ in zip(bidx, sl)]
        km = [kmod[bi][:, s_] for bi, s_ in zip(bidx, sl)]
        lhs = [_bf(jnp.concatenate([x, r_t[bi][:, s_]], axis=0)) for bi, s_, x in zip(bidx, sl, a_t)]
        rbt = [rb.rhs(x * egn[bi][:, s_], False)[0] for bi, s_, x in zip(bidx, sl, b)]
        rkt = [rb.rhs(x * egn[bi][:, s_], False)[0] for bi, s_, x in zip(bidx, sl, km)]
        pb = [_dot_nt(x, y) for x, y in zip(lhs, rbt)]
        pk = [_dot_nt(x, y) for x, y in zip(lhs, rkt)]
        minv = _rb_unit_lower_inverse(rb, [jnp.where(rb.strict, -x[:CHUNK], 0.0) for x in pb], INV_HI)
        rv_rhs = [rb.rhs(x, False) for x in vs]
        a_ak = [jnp.where(rb.strict, x[:CHUNK], 0.0) for x in pk]
        akv = [_rb_mul(x, r) for x, r in zip(a_ak, rv_rhs)]
        a_rb = [jnp.where(rb.causal, x[CHUNK:], 0.0) for x in pb]
        a_rk = [jnp.where(rb.causal, x[CHUNK:], 0.0) for x in pk]
        arkv = [_rb_mul(x, r) for x, r in zip(a_rk, rv_rhs)]
        sts = [state[i] for i in range(len(items))]
        init = [_dot_nt(x, _bf(st)) for x, st in zip(lhs, sts)]
        u = [_rb_mul(m, rb.rhs(x[:CHUNK] + y, INV_HI)) for m, x, y in zip(minv, init, akv)]
        y = [x[CHUNK:] + _rb_mul(p, rb.rhs(q, False)) + w for x, p, q, w in zip(init, a_rb, u, arkv)]
        tails = [_bf(jnp.concatenate([x * etail[bi][:, s_], k_ * etail[bi][:, s_]], axis=0))
                 for bi, s_, x, k_ in zip(bidx, sl, b, km)]
        upd = [_dot_tn(_bf(jnp.concatenate([x, v_], axis=0)), t) for x, v_, t in zip(u, vs, tails)]
        for i, (bi, s_) in enumerate(zip(bidx, sl)):
            state[i] = sts[i] * dec[bi][:, s_] + upd[i] * segf
        mean = [_dot_sel_r(x, seg, terms=2) * inv_hd for x in y]
        yc = [x - m for x, m in zip(y, mean)]
        var = [_dot_sel_r(x * x, seg, terms=2) * inv_hd for x in yc]
        for i, (bi, s_) in enumerate(zip(bidx, sl)):
            yn = yc[i] * lax.rsqrt(var[i] + RWKV_LN_EPS) * lng_ref[:, s_] + lnb_ref[:, s_]
            o_ref[bi, rows, s_] = ((yn + bon[i] * vs[i]) * gate[bi][:, s_]).astype(o_ref.dtype)
        return 0

    lax.fori_loop(0, tile // CHUNK, chunk_body, 0)


def _pad_rows(w, rows):
    return jnp.zeros((rows,) + w.shape[1:], w.dtype).at[:w.shape[0]].set(w)


def _rwkv_mu_layout(mu):
    z = jnp.zeros((LANES - RWKV_DECAY_LORA,), mu.dtype)
    o = 3 * BRANCH_WIDTH
    return jnp.concatenate([mu[:o], mu[o:o + RWKV_DECAY_LORA], z,
                            mu[o + RWKV_DECAY_LORA:o + RWKV_DECAY_LORA + RWKV_ICLR_LORA], z,
                            mu[o + RWKV_DECAY_LORA + RWKV_ICLR_LORA:]])


def _rwkv_branch(proj, mu, w0, w_up, a0, a_up, g_up, k_k, k_a, r_k, ln_g, ln_b, *, batch, seq, tile):
    bw = BRANCH_WIDTH
    row = lambda x: x.reshape(1, -1).astype(F32)
    const = lambda s: (0, 0)
    vec = pl.BlockSpec((1, bw), const)
    n_state = batch * (bw // RWKV_GROUP_W)
    out = pl.pallas_call(
        functools.partial(_rwkv_kernel, tile=tile, nb=batch),
        grid=(seq // tile,),
        in_specs=[pl.BlockSpec((batch, tile, RWKV_WIDTH), lambda s: (0, s, 0)),
                  pl.BlockSpec((1, RWKV_WIDTH), const),
                  vec, pl.BlockSpec((LANES, bw), const),
                  vec, pl.BlockSpec((LANES, bw), const),
                  pl.BlockSpec((LANES, bw), const),
                  vec, vec, vec, vec, vec],
        out_specs=pl.BlockSpec((batch, tile, bw), lambda s: (0, s, 0)),
        out_shape=jax.ShapeDtypeStruct((batch, seq, bw), BF16),
        scratch_shapes=[pltpu.VMEM((batch, tile + SUBLANES, RWKV_WIDTH), F32),
                        pltpu.VMEM((batch, SUBLANES, RWKV_WIDTH), F32),
                        pltpu.VMEM((batch, tile, RWKV_WIDTH), F32),
                        pltpu.VMEM((n_state, RWKV_GROUP_W, RWKV_GROUP_W), F32)],
        compiler_params=_cparams(("arbitrary",)),
        name="rwkv_branch",
    )(proj.reshape(batch, seq, RWKV_WIDTH), row(_rwkv_mu_layout(mu)), row(w0), _bf(_pad_rows(w_up, LANES)),
      row(a0), _bf(_pad_rows(a_up, LANES)), _bf(g_up), row(k_k), row(k_a), row(r_k), row(ln_g), row(ln_b))
    return out.reshape(batch * seq, bw)


def _pool_kernel(u_ref, pw_ref, ps_ref, o_ref, xbuf, sbuf, carry, *, tile):
    s = pl.program_id(1)

    @pl.when(s == 0)
    def _():
        carry[...] = jnp.zeros_like(carry)

    hist = POOL_MAX_WINDOW
    n = tile + hist
    xbuf[0:hist, :] = carry[...]
    xbuf[hist:n, :] = u_ref[...]
    carry[...] = xbuf[tile:n, :]
    pos = s * tile + _iota2(tile, LANES, 0)
    gw = LANES
    for gi, win in enumerate(POOL_WINDOWS):
        sl = slice(gi * gw, (gi + 1) * gw)
        sbuf[...] = xbuf[:, sl]
        span = 1
        while span < win:
            sbuf[span:n, :] = sbuf[span:n, :] + sbuf[0:n - span, :]
            span *= 2
        x = xbuf[hist:n, sl]
        count = jnp.minimum(pos + 1, win).astype(F32)
        pooled = sbuf[hist:n, :] / count - x
        y = _dot(_bf(pooled), pw_ref[gi])
        o_ref[:, sl] = (y * ps_ref[:, sl]).astype(o_ref.dtype)


def _pool_branch(u, pool_w, pool_scale, *, batch, seq, tile):
    ns = seq // tile
    bw = BRANCH_WIDTH
    return pl.pallas_call(
        functools.partial(_pool_kernel, tile=tile),
        grid=(batch, ns),
        in_specs=[pl.BlockSpec((tile, bw), lambda b, s: (b * ns + s, 0)),
                  pl.BlockSpec((len(POOL_WINDOWS), LANES, LANES), lambda b, s: (0, 0, 0)),
                  pl.BlockSpec((1, bw), lambda b, s: (0, 0))],
        out_specs=pl.BlockSpec((tile, bw), lambda b, s: (b * ns + s, 0)),
        out_shape=jax.ShapeDtypeStruct((batch * seq, bw), BF16),
        scratch_shapes=[pltpu.VMEM((tile + POOL_MAX_WINDOW, bw), F32),
                        pltpu.VMEM((tile + POOL_MAX_WINDOW, LANES), F32),
                        pltpu.VMEM((POOL_MAX_WINDOW, bw), F32)],
        compiler_params=_cparams(("parallel", "arbitrary")),
        name="pool_branch",
    )(u, _bf(pool_w), pool_scale.reshape(1, bw))


GLA_QW = GLA_HEADS * GLA_KEY_DIM
GLA_Q_OFF = 0
GLA_K_OFF = GLA_QW
GLA_V_OFF = 2 * GLA_QW
GLA_G_OFF = GLA_V_OFF + BRANCH_WIDTH
GLA_F_OFF = GLA_G_OFF + BRANCH_WIDTH
GLA_WIDTH = GLA_F_OFF + LANES
GLA_PAIRS = GLA_QW // LANES


def _gla_kernel(p_ref, fup_ref, fb_ref, ng_ref, o_ref, state, *, tile):
    s = pl.program_id(1)

    @pl.when(s == 0)
    def _():
        state[...] = jnp.zeros_like(state)

    tri = _tri_inclusive_bf16(CHUNK)
    r = _iota2(CHUNK, CHUNK, 0)
    c = _iota2(CHUNK, CHUNK, 1)
    causal = c <= r
    lane = _iota2(CHUNK, LANES, 1)
    m0 = lane < GLA_KEY_DIM
    col0 = _iota2(LANES, LANES, 1) < GLA_KEY_DIM
    ng = ng_ref[...]
    dv = GLA_VAL_DIM

    def chunk_body(ci, _):
        r0 = pl.multiple_of(ci * CHUNK, CHUNK)
        rows = pl.ds(r0, CHUNK)
        fl = p_ref[rows, GLA_F_OFF:GLA_F_OFF + LANES]
        logits = _dot(_bf(fl), fup_ref[...]) + fb_ref[...]
        log_f = -_softplus(-logits) / GLA_GATE_NORM
        gc = _dot_sel_l(tri, log_f)
        q = p_ref[rows, GLA_Q_OFF:GLA_Q_OFF + GLA_QW] * (GLA_KEY_DIM ** -0.5)
        k = p_ref[rows, GLA_K_OFF:GLA_K_OFF + GLA_QW]
        q_dec = q * jnp.exp(gc)
        k_dec = k * jnp.exp(-gc)
        glast = gc[CHUNK - 1:CHUNK, :]
        k_tail = k * jnp.exp(glast - gc)
        dec = jnp.exp(glast)
        for p in range(GLA_PAIRS):
            sl = slice(p * LANES, (p + 1) * LANES)
            qd = q_dec[:, sl]
            kd = _bf(k_dec[:, sl])
            kt = _bf(k_tail[:, sl])
            st = state[p]
            stb = _bf(st)
            upd = []
            for j in range(2):
                h = 2 * p + j
                qm = _bf(jnp.where(m0, qd, 0.0) if j == 0 else jnp.where(m0, 0.0, qd))
                attn = _dot_nt(qm, kd)
                vh = _bf(p_ref[rows, GLA_V_OFF + h * dv:GLA_V_OFF + (h + 1) * dv])
                o = _dot(_bf(jnp.where(causal, attn, 0.0)), vh) + _dot_nt(qm, stb)
                upd.append(_dot_tn(vh, kt))
                gate = p_ref[rows, GLA_G_OFF + h * dv:GLA_G_OFF + (h + 1) * dv]
                on = (o * lax.rsqrt(jnp.mean(o * o, axis=-1, keepdims=True) + NORM_EPS)) * ng
                o_ref[rows, h * dv:(h + 1) * dv] = (on * _silu(gate)).astype(o_ref.dtype)
            state[p] = st * dec[:, sl] + jnp.where(col0, upd[0], upd[1])
        return 0

    lax.fori_loop(0, tile // CHUNK, chunk_body, 0)


def _gla_branch(proj, f_up, f_bias, norm_g, *, batch, seq, tile):
    ns = seq // tile
    const = lambda b, s: (0, 0)
    return pl.pallas_call(
        functools.partial(_gla_kernel, tile=tile),
        grid=(batch, ns),
        in_specs=[pl.BlockSpec((tile, GLA_WIDTH), lambda b, s: (b * ns + s, 0)),
                  pl.BlockSpec((LANES, GLA_QW), const),
                  pl.BlockSpec((1, GLA_QW), const),
                  pl.BlockSpec((1, GLA_VAL_DIM), const)],
        out_specs=pl.BlockSpec((tile, BRANCH_WIDTH), lambda b, s: (b * ns + s, 0)),
        out_shape=jax.ShapeDtypeStruct((batch * seq, BRANCH_WIDTH), BF16),
        scratch_shapes=[pltpu.VMEM((GLA_PAIRS, LANES, LANES), F32)],
        compiler_params=_cparams(("parallel", "arbitrary")),
        name="gla_branch",
    )(proj, _bf(_pad_rows(f_up, LANES)), f_bias.reshape(1, GLA_QW), norm_g.reshape(1, GLA_VAL_DIM))


def _merge_kernel(y0, y1, y2, y3, bp_ref, g0, g1, g2, g3, gb_ref, o_ref):
    ys = (y0, y1, y2, y3)
    gs = (g0, g1, g2, g3)
    acc = None
    for i in range(N_BRANCH):
        term = _sigmoid(gs[i][...] + gb_ref[i]) * _dot(ys[i][...], bp_ref[i])
        acc = term if acc is None else acc + term
    o_ref[...] = acc.astype(o_ref.dtype)


def _merge(ys, branch_proj, gate_logits, gate_bias, *, tm, tn):
    t = ys[0].shape[0]
    d = branch_proj.shape[-1]
    nj = d // tn
    y_spec = pl.BlockSpec((tm, BRANCH_WIDTH), lambda j, i: (i, 0))
    g_specs = [pl.BlockSpec((tm, tn), functools.partial(lambda j, i, b: (i, b * nj + j), b=b))
               for b in range(N_BRANCH)]
    return pl.pallas_call(
        _merge_kernel,
        grid=(nj, t // tm),
        in_specs=[y_spec] * N_BRANCH
        + [pl.BlockSpec((N_BRANCH, BRANCH_WIDTH, tn), lambda j, i: (0, 0, j))]
        + g_specs
        + [pl.BlockSpec((N_BRANCH, 1, tn), lambda j, i: (0, 0, j))],
        out_specs=pl.BlockSpec((tm, tn), lambda j, i: (i, j)),
        out_shape=jax.ShapeDtypeStruct((t, d), BF16),
        compiler_params=_cparams(("parallel", "parallel")),
        name="merge",
    )(*ys, _bf(branch_proj), gate_logits, gate_logits, gate_logits, gate_logits,
      gate_bias.reshape(N_BRANCH, 1, d))


def _rms_rows(x, g):
    return (x * lax.rsqrt(jnp.mean(x * x, axis=-1, keepdims=True) + NORM_EPS)) * g


def _ffn_kernel(h_ref, g_ref, w1_ref, w3_ref, w2_ref, o_ref, xn_ref, acc_ref, *, nf):
    f = pl.program_id(1)

    @pl.when(f == 0)
    def _():
        acc_ref[...] = jnp.zeros_like(acc_ref)
        xn_ref[...] = _bf(_rms_rows(h_ref[...], g_ref[...]))

    x = xn_ref[...]
    mid = _bf(_silu(_dot(x, w1_ref[...])) * _dot(x, w3_ref[...]))
    acc_ref[...] += _dot(mid, w2_ref[...])

    @pl.when(f == nf - 1)
    def _():
        o_ref[...] = h_ref[...] + acc_ref[...]


def _ffn(h, g, w1, w3, w2, *, tm, tf):
    t, d = h.shape
    ff = w1.shape[1]
    nf = ff // tf
    return pl.pallas_call(
        functools.partial(_ffn_kernel, nf=nf),
        grid=(t // tm, nf),
        in_specs=[pl.BlockSpec((tm, d), lambda i, f: (i, 0)),
                  pl.BlockSpec((1, d), lambda i, f: (0, 0)),
                  pl.BlockSpec((d, tf), lambda i, f: (0, f)),
                  pl.BlockSpec((d, tf), lambda i, f: (0, f)),
                  pl.BlockSpec((tf, d), lambda i, f: (f, 0))],
        out_specs=pl.BlockSpec((tm, d), lambda i, f: (i, 0)),
        out_shape=jax.ShapeDtypeStruct((t, d), F32),
        scratch_shapes=[pltpu.VMEM((tm, d), BF16), pltpu.VMEM((tm, d), F32)],
        compiler_params=_cparams(("parallel", "arbitrary")),
        name="ffn",
    )(h, g.reshape(1, d), w1, w3, w2)


ROUTE_E0, ROUTE_E1, ROUTE_RANK0, ROUTE_RANK1 = 0, 1, 2, 3
ROUTE_W0, ROUTE_W1 = 0, 1


def _router_kernel(h_ref, g_ref, rw_ref, ri_ref, rf_ref, cnt_ref, run):
    i = pl.program_id(0)

    @pl.when(i == 0)
    def _():
        run[...] = jnp.zeros_like(run)

    tm = h_ref.shape[0]
    hn = _bf(_rms_rows(h_ref[...], g_ref[...]))
    logits = _dot(hn, rw_ref[...])
    lane = _iota2(tm, LANES, 1)
    neg = jnp.float32(-jnp.inf)
    lg = jnp.where(lane < N_EXPERTS, logits, neg)
    m1 = jnp.max(lg, axis=-1, keepdims=True)
    e0 = jnp.min(jnp.where(lg == m1, lane, LANES), axis=-1, keepdims=True)
    lg2 = jnp.where(lane == e0, neg, lg)
    m2 = jnp.max(lg2, axis=-1, keepdims=True)
    e1 = jnp.min(jnp.where(lg2 == m2, lane, LANES), axis=-1, keepdims=True)
    ex = jnp.exp(m2 - m1)
    den = 1.0 + ex
    w0 = 1.0 / den
    w1 = ex / den
    hit0 = lane == e0
    hit1 = lane == e1
    onehot = (hit0 | hit1).astype(F32)
    rr = _iota2(tm, tm, 0)
    cc = _iota2(tm, tm, 1)
    before = _dot((cc < rr).astype(BF16), _bf(onehot)) + run[...]
    rank0 = jnp.sum(jnp.where(hit0, before, 0.0), axis=-1, keepdims=True).astype(I32)
    rank1 = jnp.sum(jnp.where(hit1, before, 0.0), axis=-1, keepdims=True).astype(I32)
    run[...] += jnp.sum(onehot, axis=0, keepdims=True)
    cnt_ref[...] = run[...]
    ri_ref[...] = jnp.where(lane == ROUTE_E0, e0,
                            jnp.where(lane == ROUTE_E1, e1,
                                      jnp.where(lane == ROUTE_RANK0, rank0,
                                                jnp.where(lane == ROUTE_RANK1, rank1, 0))))
    rf_ref[...] = jnp.where(lane == ROUTE_W0, w0, jnp.where(lane == ROUTE_W1, w1, 0.0))


def _router(h, g, router_w, *, tm=512):
    t, d = h.shape
    rw = _bf(jnp.zeros((d, LANES), F32).at[:, :N_EXPERTS].set(router_w))
    return pl.pallas_call(
        _router_kernel,
        grid=(t // tm,),
        in_specs=[pl.BlockSpec((tm, d), lambda i: (i, 0)),
                  pl.BlockSpec((1, d), lambda i: (0, 0)),
                  pl.BlockSpec((d, LANES), lambda i: (0, 0))],
        out_specs=[pl.BlockSpec((tm, LANES), lambda i: (i, 0)),
                   pl.BlockSpec((tm, LANES), lambda i: (i, 0)),
                   pl.BlockSpec((1, LANES), lambda i: (0, 0))],
        out_shape=[jax.ShapeDtypeStruct((t, LANES), I32),
                   jax.ShapeDtypeStruct((t, LANES), F32),
                   jax.ShapeDtypeStruct((1, LANES), F32)],
        scratch_shapes=[pltpu.VMEM((1, LANES), F32)],
        compiler_params=_cparams(("arbitrary",)),
        name="moe_router",
    )(h, g.reshape(1, d), rw)


def _row_copy(src_ref, src_row, dst_ref, dst_row, sem):
    return pltpu.make_async_copy(src_ref.at[pl.ds(src_row, 1)], dst_ref.at[pl.ds(dst_row, 1)], sem)


DMA_ISSUE_UNROLL = 8


def _dispatch_kernel(d0_ref, d1_ref, zs_ref, zf_ref, x_ref, xb_ref, zbuf, sem, zsem):
    tm = x_ref.shape[0]
    step = pl.program_id(0)
    base = step * tm

    @pl.when(step == 0)
    def _():
        zbuf[...] = jnp.zeros_like(zbuf)

        def zero_copy(j):
            first = pl.multiple_of(zs_ref[j], MOE_BLOCK)
            return pltpu.make_async_copy(zbuf, xb_ref.at[pl.ds(first, MOE_BLOCK)], zsem)

        for j in range(2 * N_EXPERTS):
            @pl.when(zf_ref[j] > 0)
            def _():
                zero_copy(j).start()
        for j in range(2 * N_EXPERTS):
            @pl.when(zf_ref[j] > 0)
            def _():
                zero_copy(j).wait()

    def copies(rw):
        return (_row_copy(x_ref, rw, xb_ref, d0_ref[base + rw], sem),
                _row_copy(x_ref, rw, xb_ref, d1_ref[base + rw], sem))

    def start(rw, _):
        for cp in copies(rw):
            cp.start()
        return 0

    def wait(rw, _):
        for cp in copies(rw):
            cp.wait()
        return 0

    lax.fori_loop(0, tm, start, 0, unroll=DMA_ISSUE_UNROLL)
    lax.fori_loop(0, tm, wait, 0, unroll=DMA_ISSUE_UNROLL)


def _dispatch(x, dest0, dest1, zero_start, zero_flag, n_rows, *, tm=512):
    t, d = x.shape
    return pl.pallas_call(
        _dispatch_kernel,
        grid_spec=pltpu.PrefetchScalarGridSpec(
            num_scalar_prefetch=4,
            grid=(t // tm,),
            in_specs=[pl.BlockSpec((tm, d), lambda i, *_: (i, 0))],
            out_specs=pl.BlockSpec(memory_space=pl.ANY),
            scratch_shapes=[pltpu.VMEM((MOE_BLOCK, d), F32),
                            pltpu.SemaphoreType.DMA(()), pltpu.SemaphoreType.DMA(())]),
        out_shape=jax.ShapeDtypeStruct((n_rows, d), F32),
        compiler_params=_cparams(("arbitrary",)),
        name="moe_dispatch",
    )(dest0, dest1, zero_start, zero_flag, x)


def _moe_ffn_kernel(be_ref, bv_ref, x_ref, g_ref, w1_ref, w3_ref, w2_ref, o_ref, xn_ref, acc_ref, *, nf):
    del be_ref
    i = pl.program_id(0)
    f = pl.program_id(1)

    @pl.when(f == 0)
    def _():
        acc_ref[...] = jnp.zeros_like(acc_ref)
        xn_ref[...] = _bf(_rms_rows(x_ref[...], g_ref[...]))

    @pl.when(bv_ref[i] > 0)
    def _():
        x = xn_ref[...]
        mid = _bf(_silu(_dot(x, w1_ref[0])) * _dot(x, w3_ref[0]))
        acc_ref[...] += _dot(mid, w2_ref[0])

    @pl.when(f == nf - 1)
    def _():
        o_ref[...] = acc_ref[...]


def _moe_ffn(xb, g, w1, w3, w2, block_e, block_valid, *, tf):
    n_rows, d = xb.shape
    ff = w1.shape[-1]
    nf = ff // tf
    nb = n_rows // MOE_BLOCK
    return pl.pallas_call(
        functools.partial(_moe_ffn_kernel, nf=nf),
        grid_spec=pltpu.PrefetchScalarGridSpec(
            num_scalar_prefetch=2,
            grid=(nb, nf),
            in_specs=[pl.BlockSpec((MOE_BLOCK, d), lambda i, f, be, bv: (i, 0)),
                      pl.BlockSpec((1, d), lambda i, f, be, bv: (0, 0)),
                      pl.BlockSpec((1, d, tf), lambda i, f, be, bv: (be[i], 0, f * bv[i])),
                      pl.BlockSpec((1, d, tf), lambda i, f, be, bv: (be[i], 0, f * bv[i])),
                      pl.BlockSpec((1, tf, d), lambda i, f, be, bv: (be[i], f * bv[i], 0))],
            out_specs=pl.BlockSpec((MOE_BLOCK, d), lambda i, f, be, bv: (i, 0)),
            scratch_shapes=[pltpu.VMEM((MOE_BLOCK, d), BF16), pltpu.VMEM((MOE_BLOCK, d), F32)]),
        out_shape=jax.ShapeDtypeStruct((n_rows, d), F32),
        compiler_params=_cparams(("parallel", "arbitrary")),
        name="moe_experts",
    )(block_e, block_valid, xb, g.reshape(1, d), w1, w3, w2)


def _combine_kernel(d0_ref, d1_ref, yb_ref, h_ref, rf_ref, g_ref, o_ref, buf0, buf1, sem, *, final_norm):
    tm = h_ref.shape[0]
    base = pl.program_id(0) * tm

    def copies(rw):
        return (_row_copy(yb_ref, d0_ref[base + rw], buf0, rw, sem),
                _row_copy(yb_ref, d1_ref[base + rw], buf1, rw, sem))

    def start(rw, _):
        for cp in copies(rw):
            cp.start()
        return 0

    def wait(rw, _):
        for cp in copies(rw):
            cp.wait()
        return 0

    lax.fori_loop(0, tm, start, 0, unroll=DMA_ISSUE_UNROLL)
    lax.fori_loop(0, tm, wait, 0, unroll=DMA_ISSUE_UNROLL)
    w0 = rf_ref[:, ROUTE_W0:ROUTE_W0 + 1]
    w1 = rf_ref[:, ROUTE_W1:ROUTE_W1 + 1]
    out = h_ref[...] + (buf0[...] * w0 + buf1[...] * w1)
    o_ref[...] = _rms_rows(out, g_ref[...]) if final_norm else out


def _combine(yb, h, route_f, dest0, dest1, final_g, *, tm=256):
    t, d = h.shape
    g = jnp.ones((1, d), F32) if final_g is None else final_g.reshape(1, d)
    return pl.pallas_call(
        functools.partial(_combine_kernel, final_norm=final_g is not None),
        grid_spec=pltpu.PrefetchScalarGridSpec(
            num_scalar_prefetch=2,
            grid=(t // tm,),
            in_specs=[pl.BlockSpec(memory_space=pl.ANY),
                      pl.BlockSpec((tm, d), lambda i, d0, d1: (i, 0)),
                      pl.BlockSpec((tm, LANES), lambda i, d0, d1: (i, 0)),
                      pl.BlockSpec((1, d), lambda i, d0, d1: (0, 0))],
            out_specs=pl.BlockSpec((tm, d), lambda i, d0, d1: (i, 0)),
            scratch_shapes=[pltpu.VMEM((tm, d), F32), pltpu.VMEM((tm, d), F32),
                            pltpu.SemaphoreType.DMA(())]),
        out_shape=jax.ShapeDtypeStruct((t, d), F32),
        compiler_params=_cparams(("arbitrary",)),
        name="moe_combine",
    )(dest0, dest1, yb, h, route_f, g)


def _moe(h, g, router_w, w1, w3, w2, final_g=None):
    t, d = h.shape
    route_i, route_f, counts_f = _router(h, g, router_w)
    counts = counts_f[0, :N_EXPERTS].astype(I32)
    padded = (counts + MOE_BLOCK - 1) // MOE_BLOCK * MOE_BLOCK
    pad_end = jnp.cumsum(padded)
    pad_start = pad_end - padded
    n_rows = (-(-(t * 2) // MOE_BLOCK) + N_EXPERTS) * MOE_BLOCK
    n_blocks = n_rows // MOE_BLOCK
    block_first = jnp.arange(n_blocks, dtype=I32) * MOE_BLOCK
    block_e = jnp.minimum(jnp.sum((block_first[:, None] >= pad_end[None, :]).astype(I32), axis=1),
                          N_EXPERTS - 1)
    block_valid = (block_first < pad_start[block_e] + counts[block_e]).astype(I32)
    dest0 = pad_start[route_i[:, ROUTE_E0]] + route_i[:, ROUTE_RANK0]
    dest1 = pad_start[route_i[:, ROUTE_E1]] + route_i[:, ROUTE_RANK1]
    spare = pad_end[N_EXPERTS - 1] // MOE_BLOCK + jnp.arange(N_EXPERTS, dtype=I32)
    zero_start = jnp.concatenate([pad_end - MOE_BLOCK, jnp.minimum(spare, n_blocks - 1) * MOE_BLOCK])
    zero_flag = jnp.concatenate([padded > 0, spare < n_blocks]).astype(I32)
    xb = _dispatch(h, dest0, dest1, zero_start.astype(I32), zero_flag, n_rows)
    yb = _moe_ffn(xb, g, _bf(w1), _bf(w3), _bf(w2), block_e, block_valid, tf=512)
    return _combine(yb, h, route_f, dest0, dest1, final_g)


def _pad_cols(w, width):
    return jnp.concatenate([w, jnp.zeros((w.shape[0], width - w.shape[1]), w.dtype)], axis=1)


def _split_w_in(w_in):
    bw = BRANCH_WIDTH
    d = w_in.shape[0]
    widths = (3 * bw, bw, GDN_HEADS, GDN_HEADS, 3 * bw + RWKV_DECAY_LORA + RWKV_ICLR_LORA + RWKV_GATE_LORA,
              bw, GLA_QW, GLA_QW, bw, bw, GLA_GATE_RANK, N_BRANCH * d)
    offs = [0]
    for w in widths:
        offs.append(offs[-1] + w)
    col = lambda i: w_in[:, offs[i]:offs[i + 1]]
    gdn = jnp.concatenate([col(0), col(1), _pad_cols(jnp.concatenate([col(2), col(3)], axis=1), LANES)], axis=1)
    rw = col(4)
    o = 3 * bw
    rwkv = jnp.concatenate([rw[:, :o],
                            _pad_cols(rw[:, o:o + RWKV_DECAY_LORA], LANES),
                            _pad_cols(rw[:, o + RWKV_DECAY_LORA:o + RWKV_DECAY_LORA + RWKV_ICLR_LORA], LANES),
                            rw[:, o + RWKV_DECAY_LORA + RWKV_ICLR_LORA:]], axis=1)
    pool = col(5)
    gla = jnp.concatenate([col(6), col(7), col(8), col(9), _pad_cols(col(10), LANES)], axis=1)
    gate = col(11)
    return tuple(_bf(w) for w in (gdn, rwkv, pool, gla, gate))


def _mixer(hn, h, layer, batch, seq, w_in, gdn_conv_w, gdn_a_log, gdn_dt_bias, gdn_norm_g,
           rwkv_mu, rwkv_w0, rwkv_w_up, rwkv_a0, rwkv_a_up, rwkv_g_up, rwkv_k_k, rwkv_k_a,
           rwkv_r_k, rwkv_ln_g, rwkv_ln_b, pool_w, pool_scale, gla_f_up, gla_f_bias, gla_norm_g,
           gate_bias, branch_proj, w_out):
    l = layer
    w_gdn, w_rwkv, w_pool, w_gla, w_gate = _split_w_in(w_in[l])
    tm = 512
    p_gdn = _matmul(hn, w_gdn, tm=tm, tn=GDN_WIDTH, name="proj_gdn")
    p_rwkv = _matmul(hn, w_rwkv, tm=tm, tn=RWKV_WIDTH, name="proj_rwkv")
    p_pool = _matmul(hn, w_pool, tm=tm, tn=BRANCH_WIDTH, name="proj_pool")
    p_gla = _matmul(hn, w_gla, tm=tm, tn=GLA_WIDTH, name="proj_gla")
    p_gate = _matmul(hn, w_gate, tm=tm, tn=1024, name="proj_gate")
    seq_tile = 256
    kw = dict(batch=batch, seq=seq, tile=seq_tile)
    y_gdn = _gdn_branch(p_gdn, gdn_conv_w[l], gdn_a_log[l], gdn_dt_bias[l], gdn_norm_g[l], **kw)
    y_rwkv = _rwkv_branch(p_rwkv, rwkv_mu[l], rwkv_w0[l], rwkv_w_up[l], rwkv_a0[l], rwkv_a_up[l],
                          rwkv_g_up[l], rwkv_k_k[l], rwkv_k_a[l], rwkv_r_k[l].reshape(-1),
                          rwkv_ln_g[l], rwkv_ln_b[l], **kw)
    y_pool = _pool_branch(p_pool, pool_w[l], pool_scale[l], **kw)
    y_gla = _gla_branch(p_gla, gla_f_up[l], gla_f_bias[l], gla_norm_g[l], **kw)
    mixed = _merge((y_gdn, y_rwkv, y_pool, y_gla), branch_proj[l], p_gate, gate_bias[l], tm=512, tn=512)
    return _matmul(mixed, _bf(w_out[l]), tm=512, tn=1024, residual=h, name="out_proj")


def kernel(x, norm1_g, w_in, gdn_conv_w, gdn_a_log, gdn_dt_bias, gdn_norm_g, rwkv_mu, rwkv_w0, rwkv_w_up, rwkv_a0, rwkv_a_up, rwkv_g_up, rwkv_k_k, rwkv_k_a, rwkv_r_k, rwkv_ln_g, rwkv_ln_b, pool_w, pool_scale, gla_f_up, gla_f_bias, gla_norm_g, gate_bias, branch_proj, w_out, norm2_g, ffn_w1, ffn_w3, ffn_w2, moe_router, moe_w1, moe_w3, moe_w2, final_norm_g):
    batch, seq, d = x.shape
    depth = norm1_g.shape[0]
    h = x.reshape(batch * seq, d)
    for layer in range(depth):
        hn = _rmsnorm(h, norm1_g[layer], BF16)
        h = _mixer(hn, h, layer, batch, seq, w_in, gdn_conv_w, gdn_a_log, gdn_dt_bias, gdn_norm_g,
                   rwkv_mu, rwkv_w0, rwkv_w_up, rwkv_a0, rwkv_a_up, rwkv_g_up, rwkv_k_k, rwkv_k_a,
                   rwkv_r_k, rwkv_ln_g, rwkv_ln_b, pool_w, pool_scale, gla_f_up, gla_f_bias,
                   gla_norm_g, gate_bias, branch_proj, w_out)
        i = layer // 2
        last = layer == depth - 1
        if layer % 2 == 0:
            h = _ffn(h, norm2_g[layer], _bf(ffn_w1[i]), _bf(ffn_w3[i]), _bf(ffn_w2[i]), tm=512, tf=512)
            if last:
                h = _rmsnorm(h, final_norm_g, F32)
        else:
            h = _moe(h, norm2_g[layer], moe_router[i], moe_w1[i], moe_w3[i], moe_w2[i],
                     final_g=final_norm_g if last else None)
    return h.reshape(batch, seq, d)
```

```python
import functools

import jax
import jax.numpy as jnp
from jax import lax
from jax.experimental import pallas as pl
from jax.experimental.pallas import tpu as pltpu

F32 = jnp.float32
BF16 = jnp.bfloat16
I32 = jnp.int32

NORM_EPS = 1e-6
CHUNK = 64
N_BRANCH = 4
BRANCH_WIDTH = 512
GDN_HEADS = 4
GDN_HEAD_DIM = 128
GDN_CONV = 4
RWKV_HEAD_DIM = 64
RWKV_DECAY_LORA = 64
RWKV_ICLR_LORA = 64
RWKV_GATE_LORA = 128
RWKV_DECAY_SCALE = 0.606531
RWKV_LN_EPS = 64e-5
POOL_WINDOWS = (2, 4, 8, 16)
POOL_MAX_WINDOW = 16
GLA_HEADS = 4
GLA_KEY_DIM = 64
GLA_VAL_DIM = 128
GLA_GATE_RANK = 16
GLA_GATE_NORM = 16.0
N_EXPERTS = 8
MOE_BLOCK = 512

LANES = 128
SUBLANES = 8
VMEM_LIMIT_BYTES = 56 * 1024 * 1024

INV_BLOCK = 16
INV_HI = False


def _cparams(sem):
    return pltpu.CompilerParams(dimension_semantics=sem, vmem_limit_bytes=VMEM_LIMIT_BYTES)


def _dot(a, b):
    return jnp.dot(a, b, preferred_element_type=F32)


def _dot_nt(a, b):
    return lax.dot_general(a, b, (((1,), (1,)), ((), ())), preferred_element_type=F32)


def _dot_tn(a, b):
    return lax.dot_general(a, b, (((0,), (0,)), ((), ())), preferred_element_type=F32)


def _bf(x):
    return x.astype(BF16)


def _split_terms(x, terms):
    out = []
    rem = x
    for i in range(terms):
        xi = _bf(rem)
        out.append(xi)
        if i + 1 < terms:
            rem = rem - xi.astype(F32)
    return out


def _dot_sel_r(x, sel, terms=3):
    parts = [_dot(t, sel) for t in _split_terms(x, terms)]
    return functools.reduce(lambda a, b: a + b, parts)


def _dot_sel_l(sel, x, terms=3):
    parts = [_dot(sel, t) for t in _split_terms(x, terms)]
    return functools.reduce(lambda a, b: a + b, parts)


def _stacked_sel_r(xs, sel, terms=2):
    n = len(xs)
    rows = xs[0].shape[0]
    parts = [_split_terms(x, terms) for x in xs]
    out = _dot(jnp.concatenate([p[t] for t in range(terms) for p in parts], axis=0), sel)
    res = []
    for i in range(n):
        acc = out[i * rows:(i + 1) * rows]
        for t in range(1, terms):
            acc = acc + out[(t * n + i) * rows:(t * n + i + 1) * rows]
        res.append(acc)
    return res


def _dot_hi(a, b):
    ah = _bf(a)
    al = _bf(a - ah.astype(F32))
    bh = _bf(b)
    bl = _bf(b - bh.astype(F32))
    return _dot(ah, bh) + _dot(ah, bl) + _dot(al, bh)


def _sigmoid(x):
    return jax.nn.sigmoid(x)


def _silu(x):
    return x * jax.nn.sigmoid(x)


def _softplus(x):
    return jnp.maximum(x, 0.0) + jnp.log1p(jnp.exp(-jnp.abs(x)))


def _iota2(n, m, dim):
    return lax.broadcasted_iota(I32, (n, m), dim)


def _tri_inclusive_bf16(n):
    r = _iota2(n, n, 0)
    c = _iota2(n, n, 1)
    return (c <= r).astype(BF16)


class _RowBlock:
    def __init__(self, nblk):
        self.nblk = nblk
        n = nblk * CHUNK
        self._shift = CHUNK.bit_length() - 1
        row = _iota2(CHUNK, n, 0)
        colw = _iota2(CHUNK, n, 1) & (CHUNK - 1)
        self.eye = (row == colw).astype(F32)
        self.strict = colw < row
        self.causal = colw <= row
        ish = INV_BLOCK.bit_length() - 1
        self.same_diag_block = (row >> ish) == (colw >> ish)
        self._bd = {}

    def bd_mask(self, w):
        if w not in self._bd:
            n = self.nblk * CHUNK
            rb = _iota2(n, self.nblk * w, 0) >> self._shift
            cb = _iota2(n, self.nblk * w, 1) >> (w.bit_length() - 1)
            self._bd[w] = (rb == cb).astype(BF16)
        return self._bd[w]

    def rhs(self, y, hi):
        mask = self.bd_mask(y.shape[1] // self.nblk)
        yh = _bf(y)
        bdh = jnp.concatenate([yh] * self.nblk, axis=0) * mask
        if not hi:
            return (bdh, None)
        yl = _bf(y - yh.astype(F32))
        return (bdh, jnp.concatenate([yl] * self.nblk, axis=0) * mask)


def _rb_mul(x, rhs):
    bdh, bdl = rhs
    xh = _bf(x)
    out = _dot(xh, bdh)
    if bdl is not None:
        xl = _bf(x - xh.astype(F32))
        out = out + _dot(xh, bdl) + _dot(xl, bdh)
    return out


def _rb_unit_lower_inverse(rb, lows, hi):
    a = [jnp.where(rb.same_diag_block, -l, 0.0) for l in lows]
    off = [jnp.where(rb.same_diag_block, 0.0, l) for l in lows]
    t = [rb.eye + x for x in a]
    p = a
    rp = [rb.rhs(x, hi) for x in p]
    k = 2
    while k < INV_BLOCK:
        p = [_rb_mul(x, r) for x, r in zip(p, rp)]
        rp = [rb.rhs(x, hi) for x in p]
        t = [x + _rb_mul(x, r) for x, r in zip(t, rp)]
        k *= 2
    roff = [rb.rhs(x, hi) for x in off]
    nm = [_rb_mul(x, r) for x, r in zip(t, roff)]
    acc = [rb.eye - x for x in nm]
    rn = [rb.rhs(x, hi) for x in nm]
    pw = nm
    for j in range(2, CHUNK // INV_BLOCK):
        pw = [_rb_mul(x, r) for x, r in zip(pw, rn)]
        acc = [x + y if j % 2 == 0 else x - y for x, y in zip(acc, pw)]
    rt = [rb.rhs(x, hi) for x in t]
    return [_rb_mul(x, r) for x, r in zip(acc, rt)]


def _rmsnorm_kernel(x_ref, g_ref, o_ref):
    x = x_ref[...]
    ms = jnp.mean(x * x, axis=-1, keepdims=True)
    o_ref[...] = ((x * lax.rsqrt(ms + NORM_EPS)) * g_ref[...]).astype(o_ref.dtype)


def _rmsnorm(x, g, out_dtype, tm=512):
    t, d = x.shape
    return pl.pallas_call(
        _rmsnorm_kernel,
        grid=(t // tm,),
        in_specs=[pl.BlockSpec((tm, d), lambda i: (i, 0)),
                  pl.BlockSpec((1, d), lambda i: (0, 0))],
        out_specs=pl.BlockSpec((tm, d), lambda i: (i, 0)),
        out_shape=jax.ShapeDtypeStruct((t, d), out_dtype),
        compiler_params=_cparams(("parallel",)),
        name="rmsnorm",
    )(x, g.reshape(1, d))


def _mm_kernel(a_ref, b_ref, o_ref):
    o_ref[...] = _dot(a_ref[...], b_ref[...]).astype(o_ref.dtype)


def _mm_res_kernel(a_ref, b_ref, r_ref, o_ref):
    o_ref[...] = (r_ref[...] + _dot(a_ref[...], b_ref[...])).astype(o_ref.dtype)


def _matmul(a, b, *, tm, tn, out_dtype=F32, residual=None, name="matmul"):
    m, k = a.shape
    n = b.shape[1]
    grid = (n // tn, m // tm)
    in_specs = [pl.BlockSpec((tm, k), lambda j, i: (i, 0)),
                pl.BlockSpec((k, tn), lambda j, i: (0, j))]
    args = [a, b]
    kern = _mm_kernel
    if residual is not None:
        in_specs.append(pl.BlockSpec((tm, tn), lambda j, i: (i, j)))
        args.append(residual)
        kern = _mm_res_kernel
    return pl.pallas_call(
        kern,
        grid=grid,
        in_specs=in_specs,
        out_specs=pl.BlockSpec((tm, tn), lambda j, i: (i, j)),
        out_shape=jax.ShapeDtypeStruct((m, n), out_dtype),
        compiler_params=_cparams(("parallel", "parallel")),
        name=name,
    )(*args)


GDN_QKV = 3 * BRANCH_WIDTH
GDN_Z_OFF = GDN_QKV
GDN_BA_OFF = GDN_QKV + BRANCH_WIDTH
GDN_WIDTH = GDN_BA_OFF + LANES


def _gdn_kernel(p_ref, cw_ref, alog_ref, dtb_ref, ng_ref, o_ref,
                xbuf, carry, ybuf, bbuf, gbuf, state, *, tile, nb):
    s = pl.program_id(0)

    @pl.when(s == 0)
    def _():
        state[...] = jnp.zeros_like(state)
        carry[...] = jnp.zeros_like(carry)

    cw = cw_ref[...]
    for bi in range(nb):
        xbuf[bi, 0:SUBLANES, :] = carry[bi]
        xbuf[bi, SUBLANES:SUBLANES + tile, :] = p_ref[bi, :, 0:GDN_QKV]
        acc = xbuf[bi, SUBLANES:SUBLANES + tile, :] * cw[GDN_CONV - 1:GDN_CONV, :]
        for j in range(GDN_CONV - 1):
            off = SUBLANES - (GDN_CONV - 1) + j
            acc = acc + xbuf[bi, off:off + tile, :] * cw[j:j + 1, :]
        carry[bi] = xbuf[bi, tile:tile + SUBLANES, :]
        ybuf[bi] = _silu(acc)
        ba = p_ref[bi, :, GDN_BA_OFF:GDN_BA_OFF + LANES]
        bbuf[bi] = _sigmoid(ba)
        gbuf[bi] = -jnp.exp(alog_ref[...]) * _softplus(ba + dtb_ref[...])

    tri = _tri_inclusive_bf16(CHUNK)
    rb = _RowBlock(GDN_HEADS)
    lane_blk = _iota2(CHUNK, GDN_HEADS * CHUNK, 1) >> (CHUNK.bit_length() - 1)
    ng = ng_ref[...]
    hd = GDN_HEAD_DIM
    bw = BRANCH_WIDTH
    heads = range(GDN_HEADS)
    hsl = [slice(h * hd, (h + 1) * hd) for h in heads]

    def per_head_lanes(cols):
        return jnp.concatenate([jnp.broadcast_to(c, (c.shape[0], hd)) for c in cols], axis=1)

    def per_head_blocks(cols):
        out = cols[GDN_HEADS - 1]
        for h in range(GDN_HEADS - 2, -1, -1):
            out = jnp.where(lane_blk == h, cols[h], out)
        return out

    def l2n(x):
        return jnp.concatenate(
            [x[:, s_] * lax.rsqrt(jnp.sum(x[:, s_] * x[:, s_], axis=-1, keepdims=True) + 1e-6) for s_ in hsl],
            axis=1)

    def chunk_body(ci, _):
        r0 = pl.multiple_of(ci * CHUNK, CHUNK)
        rows = pl.ds(r0, CHUNK)
        bs = range(nb)
        yc = [ybuf[bi, rows, :] for bi in bs]
        bet = [bbuf[bi, rows, :] for bi in bs]
        gc = [_dot_sel_l(tri, gbuf[bi, rows, :]) for bi in bs]
        q = [l2n(x[:, 0:bw]) * (hd ** -0.5) for x in yc]
        k = [l2n(x[:, bw:2 * bw]) for x in yc]
        v = [x[:, 2 * bw:3 * bw] for x in yc]
        gcols = [[x[:, GDN_HEADS + h:GDN_HEADS + h + 1] for h in heads] for x in gc]
        glast = [[x[CHUNK - 1:CHUNK, GDN_HEADS + h:GDN_HEADS + h + 1] for h in heads] for x in gc]
        beta_l = [per_head_lanes([x[:, h:h + 1] for h in heads]) for x in bet]
        gcol_l = [per_head_lanes(c) for c in gcols]
        gcol_b = [per_head_blocks(c) for c in gcols]
        grow_b = [jnp.sum(jnp.where(rb.eye > 0, x, 0.0), axis=0, keepdims=True) for x in gcol_b]
        decay = [jnp.where(rb.causal, jnp.exp(jnp.where(rb.causal, x - y, 0.0)), 0.0)
                 for x, y in zip(gcol_b, grow_b)]
        eg = [jnp.exp(x) for x in gcol_l]
        kb = [x * y for x, y in zip(k, beta_l)]
        lhs = [_bf(jnp.concatenate([x, y], axis=0)) for x, y in zip(kb, q)]
        kr = [rb.rhs(x, False)[0] for x in k]
        pm = [_dot_nt(x, y) for x, y in zip(lhs, kr)]
        lower = [jnp.where(rb.strict, x[:CHUNK] * d, 0.0) for x, d in zip(pm, decay)]
        attn = [x[CHUNK:] * d for x, d in zip(pm, decay)]
        minv = _rb_unit_lower_inverse(rb, lower, INV_HI)
        u = [_rb_mul(m, rb.rhs(x * y, INV_HI)) for m, x, y in zip(minv, v, beta_l)]
        w = [_rb_mul(m, rb.rhs(x * y, INV_HI)) for m, x, y in zip(minv, kb, eg)]
        qd = [x * y for x, y in zip(q, eg)]
        it = [(bi, h) for bi in bs for h in heads]
        sts = [state[bi * GDN_HEADS + h] for bi, h in it]
        stb = [_bf(x) for x in sts]
        wq = [_dot(_bf(jnp.concatenate([w[bi][:, hsl[h]], qd[bi][:, hsl[h]]], axis=0)), sb)
              for (bi, h), sb in zip(it, stb)]
        wst = [x[:CHUNK] for x in wq]
        qst = [x[CHUNK:] for x in wq]
        v_new = [jnp.concatenate([u[bi][:, hsl[h]] - wst[bi * GDN_HEADS + h] for h in heads], axis=1) for bi in bs]
        av = [_rb_mul(a, rb.rhs(x, False)) for a, x in zip(attn, v_new)]
        k_tail = [_bf(k[bi][:, hsl[h]] * jnp.exp(glast[bi][h] - gcols[bi][h])) for bi, h in it]
        upd = [_dot_tn(kt, _bf(v_new[bi][:, hsl[h]])) for (bi, h), kt in zip(it, k_tail)]
        for i, (bi, h) in enumerate(it):
            state[i] = sts[i] * jnp.exp(glast[bi][h]) + upd[i]
            o = qst[i] + av[bi][:, hsl[h]]
            z = p_ref[bi, rows, GDN_Z_OFF + h * hd:GDN_Z_OFF + (h + 1) * hd]
            on = (o * lax.rsqrt(jnp.mean(o * o, axis=-1, keepdims=True) + NORM_EPS)) * ng
            o_ref[bi, rows, hsl[h]] = (on * _silu(z)).astype(o_ref.dtype)
        return 0

    lax.fori_loop(0, tile // CHUNK, chunk_body, 0)


def _gdn_branch(proj, conv_w, a_log, dt_bias, norm_g, *, batch, seq, tile):
    zeros = jnp.zeros((LANES,), F32)
    alog_p = zeros.at[GDN_HEADS:2 * GDN_HEADS].set(a_log).reshape(1, LANES)
    dtb_p = zeros.at[GDN_HEADS:2 * GDN_HEADS].set(dt_bias).reshape(1, LANES)
    const = lambda s: (0, 0)
    out = pl.pallas_call(
        functools.partial(_gdn_kernel, tile=tile, nb=batch),
        grid=(seq // tile,),
        in_specs=[pl.BlockSpec((batch, tile, GDN_WIDTH), lambda s: (0, s, 0)),
                  pl.BlockSpec((GDN_CONV, GDN_QKV), const),
                  pl.BlockSpec((1, LANES), const),
                  pl.BlockSpec((1, LANES), const),
                  pl.BlockSpec((1, GDN_HEAD_DIM), const)],
        out_specs=pl.BlockSpec((batch, tile, BRANCH_WIDTH), lambda s: (0, s, 0)),
        out_shape=jax.ShapeDtypeStruct((batch, seq, BRANCH_WIDTH), BF16),
        scratch_shapes=[pltpu.VMEM((batch, tile + SUBLANES, GDN_QKV), F32),
                        pltpu.VMEM((batch, SUBLANES, GDN_QKV), F32),
                        pltpu.VMEM((batch, tile, GDN_QKV), F32),
                        pltpu.VMEM((batch, tile, LANES), F32),
                        pltpu.VMEM((batch, tile, LANES), F32),
                        pltpu.VMEM((batch * GDN_HEADS, GDN_HEAD_DIM, GDN_HEAD_DIM), F32)],
        compiler_params=_cparams(("arbitrary",)),
        name="gdn_branch",
    )(proj.reshape(batch, seq, GDN_WIDTH), conv_w, alog_p, dtb_p, norm_g.reshape(1, GDN_HEAD_DIM))
    return out.reshape(batch * seq, BRANCH_WIDTH)


RWKV_R_OFF = 0
RWKV_K_OFF = BRANCH_WIDTH
RWKV_V_OFF = 2 * BRANCH_WIDTH
RWKV_WD_OFF = 3 * BRANCH_WIDTH
RWKV_AD_OFF = RWKV_WD_OFF + LANES
RWKV_GD_OFF = RWKV_AD_OFF + LANES
RWKV_WIDTH = RWKV_GD_OFF + LANES
RWKV_GROUP_HEADS = 4
RWKV_GROUP_W = RWKV_GROUP_HEADS * RWKV_HEAD_DIM


def _rwkv_kernel(p_ref, mu_ref, w0_ref, wup_ref, a0_ref, aup_ref, gup_ref, kk_ref, ka_ref,
                 rk_ref, lng_ref, lnb_ref, o_ref, xbuf, carry, hsbuf, state, *, tile, nb):
    s = pl.program_id(0)

    @pl.when(s == 0)
    def _():
        state[...] = jnp.zeros_like(state)
        carry[...] = jnp.zeros_like(carry)

    for bi in range(nb):
        xbuf[bi, 0:SUBLANES, :] = carry[bi]
        xbuf[bi, SUBLANES:SUBLANES + tile, :] = p_ref[bi]
        hr = xbuf[bi, SUBLANES:SUBLANES + tile, :]
        prev = xbuf[bi, SUBLANES - 1:SUBLANES - 1 + tile, :]
        carry[bi] = xbuf[bi, tile:tile + SUBLANES, :]
        hsbuf[bi] = hr + (prev - hr) * mu_ref[...]

    tri = _tri_inclusive_bf16(CHUNK)
    rb = _RowBlock(RWKV_GROUP_HEADS)
    gw = RWKV_GROUP_W
    seg = rb.bd_mask(RWKV_HEAD_DIM)
    segf = seg.astype(F32)
    inv_hd = 1.0 / RWKV_HEAD_DIM
    ng = BRANCH_WIDTH // gw
    items = [(bi, gi) for bi in range(nb) for gi in range(ng)]
    sl = [slice(gi * gw, (gi + 1) * gw) for _, gi in items]
    bidx = [bi for bi, _ in items]

    def chunk_body(ci, _):
        r0 = pl.multiple_of(ci * CHUNK, CHUNK)
        rows = pl.ds(r0, CHUNK)
        hs = [hsbuf[bi, rows, :] for bi in range(nb)]
        rv = [x[:, RWKV_R_OFF:RWKV_R_OFF + BRANCH_WIDTH] for x in hs]
        kv = [x[:, RWKV_K_OFF:RWKV_K_OFF + BRANCH_WIDTH] for x in hs]
        vv = [x[:, RWKV_V_OFF:RWKV_V_OFF + BRANCH_WIDTH] for x in hs]
        lora_in = jnp.concatenate([x[:, RWKV_WD_OFF:RWKV_WIDTH] for x in hs], axis=0)
        dw = _dot(_bf(jnp.tanh(lora_in[:, 0:LANES])), wup_ref[...])
        da = _dot(_bf(lora_in[:, LANES:2 * LANES]), aup_ref[...])
        gate_all = _dot(_bf(_sigmoid(lora_in[:, 2 * LANES:3 * LANES])), gup_ref[...])
        rowsl = [slice(bi * CHUNK, (bi + 1) * CHUNK) for bi in range(nb)]
        log_w = [-RWKV_DECAY_SCALE * _sigmoid(w0_ref[...] + dw[r_]) for r_ in rowsl]
        a_lr = [_sigmoid(a0_ref[...] + da[r_]) for r_ in rowsl]
        gate = [gate_all[r_] for r_ in rowsl]
        kkr = [x * kk_ref[...] for x in kv]
        kmod = [x * (1.0 + (a - 1.0) * ka_ref[...]) for x, a in zip(kv, a_lr)]
        rk = [x * y * rk_ref[...] for x, y in zip(rv, kmod)]
        g = [_dot_sel_l(tri, x) for x in log_w]
        egn = [jnp.exp(-x) for x in g]
        glast = [x[CHUNK - 1:CHUNK, :] for x in g]
        etail = [jnp.exp(gl - x) for gl, x in zip(glast, g)]
        dec = [jnp.exp(x) for x in glast]
        r_t = [x * jnp.exp(y) for x, y in zip(rv, g)]
        g_prev = [x - y for x, y in zip(g, log_w)]
        kkss = _stacked_sel_r([kkr[bi][:, s_] * kkr[bi][:, s_] for bi, s_ in zip(bidx, sl)], seg)
        bon = _stacked_sel_r([rk[bi][:, s_] for bi, s_ in zip(bidx, sl)], seg)
        kk = [kkr[bi][:, s_] * lax.rsqrt(x + 1e-6) for bi, s_, x in zip(bidx, sl, kkss)]
        b = [x * a_lr[bi][:, s_] for bi, s_, x in zip(bidx, sl, kk)]
        a_t = [-x * jnp.exp(g_prev[bi][:, s_]) for bi, s_, x in zip(bidx, sl, kk)]
        vs = [vv[bi][:, s_] for bi, s_ in zip(bidx, sl)]
        km = [kmod[bi][:, s_] for bi, s_ in zip(bidx, sl)]
        lhs = [_bf(jnp.concatenate([x, r_t[bi][:, s_]], axis=0)) for bi, s_, x in zip(bidx, sl, a_t)]
        rbt = [rb.rhs(x * egn[bi][:, s_], False)[0] for bi, s_, x in zip(bidx, sl, b)]
        rkt = [rb.rhs(x * egn[bi][:, s_], False)[0] for bi, s_, x in zip(bidx, sl, km)]
        pb = [_dot_nt(x, y) for x, y in zip(lhs, rbt)]
        pk = [_dot_nt(x, y) for x, y in zip(lhs, rkt)]
        minv = _rb_unit_lower_inverse(rb, [jnp.where(rb.strict, -x[:CHUNK], 0.0) for x in pb], INV_HI)
        rv_rhs = [rb.rhs(x, False) for x in vs]
        a_k = [jnp.concatenate([jnp.where(rb.strict, x[:CHUNK], 0.0), jnp.where(rb.causal, x[CHUNK:], 0.0)],
                               axis=0) for x in pk]
        a_kv = [_rb_mul(x, r) for x, r in zip(a_k, rv_rhs)]
        akv = [x[:CHUNK] for x in a_kv]
        arkv = [x[CHUNK:] for x in a_kv]
        a_rb = [jnp.where(rb.causal, x[CHUNK:], 0.0) for x in pb]
        sts = [state[i] for i in range(len(items))]
        init = [_dot_nt(x, _bf(st)) for x, st in zip(lhs, sts)]
        u = [_rb_mul(m, rb.rhs(x[:CHUNK] + y, INV_HI)) for m, x, y in zip(minv, init, akv)]
        y = [x[CHUNK:] + _rb_mul(p, rb.rhs(q, False)) + w for x, p, q, w in zip(init, a_rb, u, arkv)]
        tails = [_bf(jnp.concatenate([x * etail[bi][:, s_], k_ * etail[bi][:, s_]], axis=0))
                 for bi, s_, x, k_ in zip(bidx, sl, b, km)]
        upd = [_dot_tn(_bf(jnp.concatenate([x, v_], axis=0)), t) for x, v_, t in zip(u, vs, tails)]
        for i, (bi, s_) in enumerate(zip(bidx, sl)):
            state[i] = sts[i] * dec[bi][:, s_] + upd[i] * segf
        mean = [x * inv_hd for x in _stacked_sel_r(y, seg)]
        yc = [x - m for x, m in zip(y, mean)]
        var = [x * inv_hd for x in _stacked_sel_r([x * x for x in yc], seg)]
        for i, (bi, s_) in enumerate(zip(bidx, sl)):
            yn = yc[i] * lax.rsqrt(var[i] + RWKV_LN_EPS) * lng_ref[:, s_] + lnb_ref[:, s_]
            o_ref[bi, rows, s_] = ((yn + bon[i] * vs[i]) * gate[bi][:, s_]).astype(o_ref.dtype)
        return 0

    lax.fori_loop(0, tile // CHUNK, chunk_body, 0)


def _pad_rows(w, rows):
    return jnp.zeros((rows,) + w.shape[1:], w.dtype).at[:w.shape[0]].set(w)


def _rwkv_mu_layout(mu):
    z = jnp.zeros((LANES - RWKV_DECAY_LORA,), mu.dtype)
    o = 3 * BRANCH_WIDTH
    return jnp.concatenate([mu[:o], mu[o:o + RWKV_DECAY_LORA], z,
                            mu[o + RWKV_DECAY_LORA:o + RWKV_DECAY_LORA + RWKV_ICLR_LORA], z,
                            mu[o + RWKV_DECAY_LORA + RWKV_ICLR_LORA:]])


def _rwkv_branch(proj, mu, w0, w_up, a0, a_up, g_up, k_k, k_a, r_k, ln_g, ln_b, *, batch, seq, tile):
    bw = BRANCH_WIDTH
    row = lambda x: x.reshape(1, -1).astype(F32)
    const = lambda s: (0, 0)
    vec = pl.BlockSpec((1, bw), const)
    n_state = batch * (bw // RWKV_GROUP_W)
    out = pl.pallas_call(
        functools.partial(_rwkv_kernel, tile=tile, nb=batch),
        grid=(seq // tile,),
        in_specs=[pl.BlockSpec((batch, tile, RWKV_WIDTH), lambda s: (0, s, 0)),
                  pl.BlockSpec((1, RWKV_WIDTH), const),
                  vec, pl.BlockSpec((LANES, bw), const),
                  vec, pl.BlockSpec((LANES, bw), const),
                  pl.BlockSpec((LANES, bw), const),
                  vec, vec, vec, vec, vec],
        out_specs=pl.BlockSpec((batch, tile, bw), lambda s: (0, s, 0)),
        out_shape=jax.ShapeDtypeStruct((batch, seq, bw), BF16),
        scratch_shapes=[pltpu.VMEM((batch, tile + SUBLANES, RWKV_WIDTH), F32),
                        pltpu.VMEM((batch, SUBLANES, RWKV_WIDTH), F32),
                        pltpu.VMEM((batch, tile, RWKV_WIDTH), F32),
                        pltpu.VMEM((n_state, RWKV_GROUP_W, RWKV_GROUP_W), F32)],
        compiler_params=_cparams(("arbitrary",)),
        name="rwkv_branch",
    )(proj.reshape(batch, seq, RWKV_WIDTH), row(_rwkv_mu_layout(mu)), row(w0), _bf(_pad_rows(w_up, LANES)),
      row(a0), _bf(_pad_rows(a_up, LANES)), _bf(g_up), row(k_k), row(k_a), row(r_k), row(ln_g), row(ln_b))
    return out.reshape(batch * seq, bw)


def _pool_kernel(u_ref, pw_ref, ps_ref, o_ref, xbuf, sbuf, carry, *, tile):
    s = pl.program_id(1)

    @pl.when(s == 0)
    def _():
        carry[...] = jnp.zeros_like(carry)

    hist = POOL_MAX_WINDOW
    n = tile + hist
    xbuf[0:hist, :] = carry[...]
    xbuf[hist:n, :] = u_ref[...]
    carry[...] = xbuf[tile:n, :]
    pos = s * tile + _iota2(tile, LANES, 0)
    gw = LANES
    for gi, win in enumerate(POOL_WINDOWS):
        sl = slice(gi * gw, (gi + 1) * gw)
        sbuf[...] = xbuf[:, sl]
        span = 1
        while span < win:
            sbuf[span:n, :] = sbuf[span:n, :] + sbuf[0:n - span, :]
            span *= 2
        x = xbuf[hist:n, sl]
        count = jnp.minimum(pos + 1, win).astype(F32)
        pooled = sbuf[hist:n, :] / count - x
        y = _dot(_bf(pooled), pw_ref[gi])
        o_ref[:, sl] = (y * ps_ref[:, sl]).astype(o_ref.dtype)


def _pool_branch(u, pool_w, pool_scale, *, batch, seq, tile):
    ns = seq // tile
    bw = BRANCH_WIDTH
    return pl.pallas_call(
        functools.partial(_pool_kernel, tile=tile),
        grid=(batch, ns),
        in_specs=[pl.BlockSpec((tile, bw), lambda b, s: (b * ns + s, 0)),
                  pl.BlockSpec((len(POOL_WINDOWS), LANES, LANES), lambda b, s: (0, 0, 0)),
                  pl.BlockSpec((1, bw), lambda b, s: (0, 0))],
        out_specs=pl.BlockSpec((tile, bw), lambda b, s: (b * ns + s, 0)),
        out_shape=jax.ShapeDtypeStruct((batch * seq, bw), BF16),
        scratch_shapes=[pltpu.VMEM((tile + POOL_MAX_WINDOW, bw), F32),
                        pltpu.VMEM((tile + POOL_MAX_WINDOW, LANES), F32),
                        pltpu.VMEM((POOL_MAX_WINDOW, bw), F32)],
        compiler_params=_cparams(("parallel", "arbitrary")),
        name="pool_branch",
    )(u, _bf(pool_w), pool_scale.reshape(1, bw))


GLA_QW = GLA_HEADS * GLA_KEY_DIM
GLA_Q_OFF = 0
GLA_K_OFF = GLA_QW
GLA_V_OFF = 2 * GLA_QW
GLA_G_OFF = GLA_V_OFF + BRANCH_WIDTH
GLA_F_OFF = GLA_G_OFF + BRANCH_WIDTH
GLA_WIDTH = GLA_F_OFF + LANES


def _gla_kernel(p_ref, fup_ref, fb_ref, ng_ref, o_ref, state, *, tile, nb):
    s = pl.program_id(0)

    @pl.when(s == 0)
    def _():
        state[...] = jnp.zeros_like(state)

    tri = _tri_inclusive_bf16(CHUNK)
    rb = _RowBlock(GLA_HEADS)
    vh = _iota2(BRANCH_WIDTH, GLA_QW, 0) >> (GLA_VAL_DIM.bit_length() - 1)
    kh = _iota2(BRANCH_WIDTH, GLA_QW, 1) >> (GLA_KEY_DIM.bit_length() - 1)
    same_head = (vh == kh).astype(F32)
    ng = ng_ref[...]
    dv = GLA_VAL_DIM
    bs = range(nb)

    def chunk_body(ci, _):
        r0 = pl.multiple_of(ci * CHUNK, CHUNK)
        rows = pl.ds(r0, CHUNK)
        fl = jnp.concatenate([p_ref[bi, rows, GLA_F_OFF:GLA_F_OFF + LANES] for bi in bs], axis=0)
        logits = _dot(_bf(fl), fup_ref[...]) + fb_ref[...]
        log_f = -_softplus(-logits) / GLA_GATE_NORM
        gc = [_dot_sel_l(tri, log_f[bi * CHUNK:(bi + 1) * CHUNK]) for bi in bs]
        q = [p_ref[bi, rows, GLA_Q_OFF:GLA_Q_OFF + GLA_QW] * (GLA_KEY_DIM ** -0.5) for bi in bs]
        k = [p_ref[bi, rows, GLA_K_OFF:GLA_K_OFF + GLA_QW] for bi in bs]
        v = [p_ref[bi, rows, GLA_V_OFF:GLA_V_OFF + BRANCH_WIDTH] for bi in bs]
        q_dec = [_bf(x * jnp.exp(g)) for x, g in zip(q, gc)]
        k_dec = [x * jnp.exp(-g) for x, g in zip(k, gc)]
        glast = [g[CHUNK - 1:CHUNK, :] for g in gc]
        k_tail = [_bf(x * jnp.exp(gl - g)) for x, gl, g in zip(k, glast, gc)]
        attn = [_dot_nt(x, rb.rhs(y, False)[0]) for x, y in zip(q_dec, k_dec)]
        intra = [_rb_mul(jnp.where(rb.causal, a, 0.0), rb.rhs(x, False)) for a, x in zip(attn, v)]
        sts = [state[bi] for bi in bs]
        inter = [_dot_nt(x, _bf(st)) for x, st in zip(q_dec, sts)]
        upd = [_dot_tn(_bf(x), y) for x, y in zip(v, k_tail)]
        for bi in bs:
            state[bi] = sts[bi] * jnp.exp(glast[bi]) + upd[bi] * same_head
            o = intra[bi] + inter[bi]
            for h in range(GLA_HEADS):
                oh = o[:, h * dv:(h + 1) * dv]
                gate = p_ref[bi, rows, GLA_G_OFF + h * dv:GLA_G_OFF + (h + 1) * dv]
                on = (oh * lax.rsqrt(jnp.mean(oh * oh, axis=-1, keepdims=True) + NORM_EPS)) * ng
                o_ref[bi, rows, h * dv:(h + 1) * dv] = (on * _silu(gate)).astype(o_ref.dtype)
        return 0

    lax.fori_loop(0, tile // CHUNK, chunk_body, 0)


def _gla_branch(proj, f_up, f_bias, norm_g, *, batch, seq, tile):
    const = lambda s: (0, 0)
    out = pl.pallas_call(
        functools.partial(_gla_kernel, tile=tile, nb=batch),
        grid=(seq // tile,),
        in_specs=[pl.BlockSpec((batch, tile, GLA_WIDTH), lambda s: (0, s, 0)),
                  pl.BlockSpec((LANES, GLA_QW), const),
                  pl.BlockSpec((1, GLA_QW), const),
                  pl.BlockSpec((1, GLA_VAL_DIM), const)],
        out_specs=pl.BlockSpec((batch, tile, BRANCH_WIDTH), lambda s: (0, s, 0)),
        out_shape=jax.ShapeDtypeStruct((batch, seq, BRANCH_WIDTH), BF16),
        scratch_shapes=[pltpu.VMEM((batch, BRANCH_WIDTH, GLA_QW), F32)],
        compiler_params=_cparams(("arbitrary",)),
        name="gla_branch",
    )(proj.reshape(batch, seq, GLA_WIDTH), _bf(_pad_rows(f_up, LANES)), f_bias.reshape(1, GLA_QW),
      norm_g.reshape(1, GLA_VAL_DIM))
    return out.reshape(batch * seq, BRANCH_WIDTH)


def _merge_kernel(y0, y1, y2, y3, bp_ref, g0, g1, g2, g3, gb_ref, o_ref):
    ys = (y0, y1, y2, y3)
    gs = (g0, g1, g2, g3)
    acc = None
    for i in range(N_BRANCH):
        gate = 0.5 * (jnp.tanh(0.5 * (gs[i][...] + gb_ref[i])) + 1.0)
        term = gate * _dot(ys[i][...], bp_ref[i])
        acc = term if acc is None else acc + term
    o_ref[...] = acc.astype(o_ref.dtype)


def _merge(ys, branch_proj, gate_logits, gate_bias, *, tm, tn):
    t = ys[0].shape[0]
    d = branch_proj.shape[-1]
    nj = d // tn
    y_spec = pl.BlockSpec((tm, BRANCH_WIDTH), lambda j, i: (i, 0))
    g_specs = [pl.BlockSpec((tm, tn), functools.partial(lambda j, i, b: (i, b * nj + j), b=b))
               for b in range(N_BRANCH)]
    return pl.pallas_call(
        _merge_kernel,
        grid=(nj, t // tm),
        in_specs=[y_spec] * N_BRANCH
        + [pl.BlockSpec((N_BRANCH, BRANCH_WIDTH, tn), lambda j, i: (0, 0, j))]
        + g_specs
        + [pl.BlockSpec((N_BRANCH, 1, tn), lambda j, i: (0, 0, j))],
        out_specs=pl.BlockSpec((tm, tn), lambda j, i: (i, j)),
        out_shape=jax.ShapeDtypeStruct((t, d), BF16),
        compiler_params=_cparams(("parallel", "parallel")),
        name="merge",
    )(*ys, _bf(branch_proj), gate_logits, gate_logits, gate_logits, gate_logits,
      gate_bias.reshape(N_BRANCH, 1, d))


def _rms_rows(x, g):
    return (x * lax.rsqrt(jnp.mean(x * x, axis=-1, keepdims=True) + NORM_EPS)) * g


def _ffn_kernel(h_ref, g_ref, w1_ref, w3_ref, w2_ref, o_ref, xn_ref, acc_ref, *, nf):
    f = pl.program_id(1)

    @pl.when(f == 0)
    def _():
        acc_ref[...] = jnp.zeros_like(acc_ref)
        xn_ref[...] = _bf(_rms_rows(h_ref[...], g_ref[...]))

    x = xn_ref[...]
    mid = _bf(_silu(_dot(x, w1_ref[...])) * _dot(x, w3_ref[...]))
    acc_ref[...] += _dot(mid, w2_ref[...])

    @pl.when(f == nf - 1)
    def _():
        o_ref[...] = h_ref[...] + acc_ref[...]


def _ffn(h, g, w1, w3, w2, *, tm, tf):
    t, d = h.shape
    ff = w1.shape[1]
    nf = ff // tf
    return pl.pallas_call(
        functools.partial(_ffn_kernel, nf=nf),
        grid=(t // tm, nf),
        in_specs=[pl.BlockSpec((tm, d), lambda i, f: (i, 0)),
                  pl.BlockSpec((1, d), lambda i, f: (0, 0)),
                  pl.BlockSpec((d, tf), lambda i, f: (0, f)),
                  pl.BlockSpec((d, tf), lambda i, f: (0, f)),
                  pl.BlockSpec((tf, d), lambda i, f: (f, 0))],
        out_specs=pl.BlockSpec((tm, d), lambda i, f: (i, 0)),
        out_shape=jax.ShapeDtypeStruct((t, d), F32),
        scratch_shapes=[pltpu.VMEM((tm, d), BF16), pltpu.VMEM((tm, d), F32)],
        compiler_params=_cparams(("parallel", "arbitrary")),
        name="ffn",
    )(h, g.reshape(1, d), w1, w3, w2)


ROUTE_E0, ROUTE_E1, ROUTE_RANK0, ROUTE_RANK1 = 0, 1, 2, 3
ROUTE_W0, ROUTE_W1 = 0, 1


def _router_kernel(h_ref, g_ref, rw_ref, ri_ref, rf_ref, cnt_ref, run):
    i = pl.program_id(0)

    @pl.when(i == 0)
    def _():
        run[...] = jnp.zeros_like(run)

    tm = h_ref.shape[0]
    hn = _bf(_rms_rows(h_ref[...], g_ref[...]))
    logits = _dot(hn, rw_ref[...])
    lane = _iota2(tm, LANES, 1)
    neg = jnp.float32(-jnp.inf)
    lg = jnp.where(lane < N_EXPERTS, logits, neg)
    m1 = jnp.max(lg, axis=-1, keepdims=True)
    e0 = jnp.min(jnp.where(lg == m1, lane, LANES), axis=-1, keepdims=True)
    lg2 = jnp.where(lane == e0, neg, lg)
    m2 = jnp.max(lg2, axis=-1, keepdims=True)
    e1 = jnp.min(jnp.where(lg2 == m2, lane, LANES), axis=-1, keepdims=True)
    ex = jnp.exp(m2 - m1)
    den = 1.0 + ex
    w0 = 1.0 / den
    w1 = ex / den
    hit0 = lane == e0
    hit1 = lane == e1
    onehot = (hit0 | hit1).astype(F32)
    rr = _iota2(tm, tm, 0)
    cc = _iota2(tm, tm, 1)
    before = _dot((cc < rr).astype(BF16), _bf(onehot)) + run[...]
    rank0 = jnp.sum(jnp.where(hit0, before, 0.0), axis=-1, keepdims=True).astype(I32)
    rank1 = jnp.sum(jnp.where(hit1, before, 0.0), axis=-1, keepdims=True).astype(I32)
    run[...] += jnp.sum(onehot, axis=0, keepdims=True)
    cnt_ref[...] = run[...]
    ri_ref[...] = jnp.where(lane == ROUTE_E0, e0,
                            jnp.where(lane == ROUTE_E1, e1,
                                      jnp.where(lane == ROUTE_RANK0, rank0,
                                                jnp.where(lane == ROUTE_RANK1, rank1, 0))))
    rf_ref[...] = jnp.where(lane == ROUTE_W0, w0, jnp.where(lane == ROUTE_W1, w1, 0.0))


def _router(h, g, router_w, *, tm=512):
    t, d = h.shape
    rw = _bf(jnp.zeros((d, LANES), F32).at[:, :N_EXPERTS].set(router_w))
    return pl.pallas_call(
        _router_kernel,
        grid=(t // tm,),
        in_specs=[pl.BlockSpec((tm, d), lambda i: (i, 0)),
                  pl.BlockSpec((1, d), lambda i: (0, 0)),
                  pl.BlockSpec((d, LANES), lambda i: (0, 0))],
        out_specs=[pl.BlockSpec((tm, LANES), lambda i: (i, 0)),
                   pl.BlockSpec((tm, LANES), lambda i: (i, 0)),
                   pl.BlockSpec((1, LANES), lambda i: (0, 0))],
        out_shape=[jax.ShapeDtypeStruct((t, LANES), I32),
                   jax.ShapeDtypeStruct((t, LANES), F32),
                   jax.ShapeDtypeStruct((1, LANES), F32)],
        scratch_shapes=[pltpu.VMEM((1, LANES), F32)],
        compiler_params=_cparams(("arbitrary",)),
        name="moe_router",
    )(h, g.reshape(1, d), rw)


def _row_copy(src_ref, src_row, dst_ref, dst_row, sem):
    return pltpu.make_async_copy(src_ref.at[pl.ds(src_row, 1)], dst_ref.at[pl.ds(dst_row, 1)], sem)


DMA_ISSUE_UNROLL = 8


def _dispatch_kernel(d0_ref, d1_ref, zs_ref, zf_ref, x_ref, xb_ref, zbuf, sem, zsem):
    tm = x_ref.shape[0]
    step = pl.program_id(0)
    base = step * tm

    @pl.when(step == 0)
    def _():
        zbuf[...] = jnp.zeros_like(zbuf)

        def zero_copy(j):
            first = pl.multiple_of(zs_ref[j], MOE_BLOCK)
            return pltpu.make_async_copy(zbuf, xb_ref.at[pl.ds(first, MOE_BLOCK)], zsem)

        for j in range(2 * N_EXPERTS):
            @pl.when(zf_ref[j] > 0)
            def _():
                zero_copy(j).start()
        for j in range(2 * N_EXPERTS):
            @pl.when(zf_ref[j] > 0)
            def _():
                zero_copy(j).wait()

    def copies(rw):
        return (_row_copy(x_ref, rw, xb_ref, d0_ref[base + rw], sem),
                _row_copy(x_ref, rw, xb_ref, d1_ref[base + rw], sem))

    def start(rw, _):
        for cp in copies(rw):
            cp.start()
        return 0

    def wait(rw, _):
        for cp in copies(rw):
            cp.wait()
        return 0

    lax.fori_loop(0, tm, start, 0, unroll=DMA_ISSUE_UNROLL)
    lax.fori_loop(0, tm, wait, 0, unroll=DMA_ISSUE_UNROLL)


def _dispatch(x, dest0, dest1, zero_start, zero_flag, n_rows, *, tm=512):
    t, d = x.shape
    return pl.pallas_call(
        _dispatch_kernel,
        grid_spec=pltpu.PrefetchScalarGridSpec(
            num_scalar_prefetch=4,
            grid=(t // tm,),
            in_specs=[pl.BlockSpec((tm, d), lambda i, *_: (i, 0))],
            out_specs=pl.BlockSpec(memory_space=pl.ANY),
            scratch_shapes=[pltpu.VMEM((MOE_BLOCK, d), F32),
                            pltpu.SemaphoreType.DMA(()), pltpu.SemaphoreType.DMA(())]),
        out_shape=jax.ShapeDtypeStruct((n_rows, d), F32),
        compiler_params=_cparams(("arbitrary",)),
        name="moe_dispatch",
    )(dest0, dest1, zero_start, zero_flag, x)


MOE_UNIT_BLOCKS = 2


def _moe_ffn_kernel(ue_ref, ub0_ref, ub1_ref, uv0_ref, uv1_ref, zs_ref, zf_ref, x0_ref, x1_ref, g_ref,
                    w1_ref, w3_ref, w2_ref, yb_ref, xn0, xn1, acc0, acc1, sem, *, nf):
    del ue_ref
    u = pl.program_id(0)
    f = pl.program_id(1)
    blocks = ((uv0_ref, ub0_ref, x0_ref, xn0, acc0), (uv1_ref, ub1_ref, x1_ref, xn1, acc1))

    @pl.when((u == 0) & (f == 0))
    def _():
        acc1[...] = jnp.zeros_like(acc1)

        def zero_copy(j):
            first = pl.multiple_of(zs_ref[j], MOE_BLOCK)
            return pltpu.make_async_copy(acc1, yb_ref.at[pl.ds(first, MOE_BLOCK)], sem)

        for j in range(N_EXPERTS):
            @pl.when(zf_ref[j] > 0)
            def _():
                zero_copy(j).start()
        for j in range(N_EXPERTS):
            @pl.when(zf_ref[j] > 0)
            def _():
                zero_copy(j).wait()

    @pl.when(f == 0)
    def _():
        for uv, _, x_ref, xn, acc in blocks:
            @pl.when(uv[u] > 0)
            def _():
                acc[...] = jnp.zeros_like(acc)
                xn[...] = _bf(_rms_rows(x_ref[...], g_ref[...]))

    for uv, _, _, xn, acc in blocks:
        @pl.when(uv[u] > 0)
        def _():
            x = xn[...]
            mid = _bf(_silu(_dot(x, _bf(w1_ref[0]))) * _dot(x, _bf(w3_ref[0])))
            acc[...] += _dot(mid, _bf(w2_ref[0]))

    @pl.when(f == nf - 1)
    def _():
        def out_copy(ub, acc):
            first = pl.multiple_of(ub[u] * MOE_BLOCK, MOE_BLOCK)
            return pltpu.make_async_copy(acc, yb_ref.at[pl.ds(first, MOE_BLOCK)], sem)

        for uv, ub, _, _, acc in blocks:
            @pl.when(uv[u] > 0)
            def _():
                out_copy(ub, acc).start()
        for uv, ub, _, _, acc in blocks:
            @pl.when(uv[u] > 0)
            def _():
                out_copy(ub, acc).wait()


def _moe_ffn(xb, g, w1, w3, w2, unit_e, unit_b0, unit_b1, unit_v0, unit_v1, spare_start, spare_flag, *, tf):
    n_rows, d = xb.shape
    ff = w1.shape[-1]
    nf = ff // tf
    n_units = unit_e.shape[0]
    wcol = lambda u, f, ue, b0, b1, v0, *_: (ue[u], 0, f * v0[u])
    wrow = lambda u, f, ue, b0, b1, v0, *_: (ue[u], f * v0[u], 0)
    return pl.pallas_call(
        functools.partial(_moe_ffn_kernel, nf=nf),
        grid_spec=pltpu.PrefetchScalarGridSpec(
            num_scalar_prefetch=7,
            grid=(n_units, nf),
            in_specs=[pl.BlockSpec((MOE_BLOCK, d), lambda u, f, ue, b0, *_: (b0[u], 0)),
                      pl.BlockSpec((MOE_BLOCK, d), lambda u, f, ue, b0, b1, *_: (b1[u], 0)),
                      pl.BlockSpec((1, d), lambda u, f, *_: (0, 0)),
                      pl.BlockSpec((1, d, tf), wcol),
                      pl.BlockSpec((1, d, tf), wcol),
                      pl.BlockSpec((1, tf, d), wrow)],
            out_specs=pl.BlockSpec(memory_space=pl.ANY),
            scratch_shapes=[pltpu.VMEM((MOE_BLOCK, d), BF16), pltpu.VMEM((MOE_BLOCK, d), BF16),
                            pltpu.VMEM((MOE_BLOCK, d), F32), pltpu.VMEM((MOE_BLOCK, d), F32),
                            pltpu.SemaphoreType.DMA(())]),
        out_shape=jax.ShapeDtypeStruct((n_rows, d), F32),
        compiler_params=_cparams(("arbitrary", "arbitrary")),
        name="moe_experts",
    )(unit_e, unit_b0, unit_b1, unit_v0, unit_v1, spare_start, spare_flag, xb, xb, g.reshape(1, d), w1, w3, w2)


def _combine_kernel(d0_ref, d1_ref, yb_ref, h_ref, rf_ref, g_ref, o_ref, buf0, buf1, sem, *, final_norm):
    tm = h_ref.shape[0]
    base = pl.program_id(0) * tm

    def copies(rw):
        return (_row_copy(yb_ref, d0_ref[base + rw], buf0, rw, sem),
                _row_copy(yb_ref, d1_ref[base + rw], buf1, rw, sem))

    def start(rw, _):
        for cp in copies(rw):
            cp.start()
        return 0

    def wait(rw, _):
        for cp in copies(rw):
            cp.wait()
        return 0

    lax.fori_loop(0, tm, start, 0, unroll=DMA_ISSUE_UNROLL)
    lax.fori_loop(0, tm, wait, 0, unroll=DMA_ISSUE_UNROLL)
    w0 = rf_ref[:, ROUTE_W0:ROUTE_W0 + 1]
    w1 = rf_ref[:, ROUTE_W1:ROUTE_W1 + 1]
    out = h_ref[...] + (buf0[...] * w0 + buf1[...] * w1)
    o_ref[...] = _rms_rows(out, g_ref[...]) if final_norm else out


def _combine(yb, h, route_f, dest0, dest1, final_g, *, tm=256):
    t, d = h.shape
    g = jnp.ones((1, d), F32) if final_g is None else final_g.reshape(1, d)
    return pl.pallas_call(
        functools.partial(_combine_kernel, final_norm=final_g is not None),
        grid_spec=pltpu.PrefetchScalarGridSpec(
            num_scalar_prefetch=2,
            grid=(t // tm,),
            in_specs=[pl.BlockSpec(memory_space=pl.ANY),
                      pl.BlockSpec((tm, d), lambda i, d0, d1: (i, 0)),
                      pl.BlockSpec((tm, LANES), lambda i, d0, d1: (i, 0)),
                      pl.BlockSpec((1, d), lambda i, d0, d1: (0, 0))],
            out_specs=pl.BlockSpec((tm, d), lambda i, d0, d1: (i, 0)),
            scratch_shapes=[pltpu.VMEM((tm, d), F32), pltpu.VMEM((tm, d), F32),
                            pltpu.SemaphoreType.DMA(())]),
        out_shape=jax.ShapeDtypeStruct((t, d), F32),
        compiler_params=_cparams(("arbitrary",)),
        name="moe_combine",
    )(dest0, dest1, yb, h, route_f, g)


def _moe(h, g, router_w, w1, w3, w2, final_g=None):
    t, d = h.shape
    route_i, route_f, counts_f = _router(h, g, router_w)
    counts = counts_f[0, :N_EXPERTS].astype(I32)
    padded = (counts + MOE_BLOCK - 1) // MOE_BLOCK * MOE_BLOCK
    pad_end = jnp.cumsum(padded)
    pad_start = pad_end - padded
    n_rows = (-(-(t * 2) // MOE_BLOCK) + N_EXPERTS) * MOE_BLOCK
    n_blocks = n_rows // MOE_BLOCK
    blocks_e = padded // MOE_BLOCK
    units_e = (blocks_e + MOE_UNIT_BLOCKS - 1) // MOE_UNIT_BLOCKS
    unit_end = jnp.cumsum(units_e)
    n_units = n_blocks // MOE_UNIT_BLOCKS + N_EXPERTS
    uidx = jnp.arange(n_units, dtype=I32)
    unit_e = jnp.minimum(jnp.sum((uidx[:, None] >= unit_end[None, :]).astype(I32), axis=1), N_EXPERTS - 1)
    within = uidx - (unit_end - units_e)[unit_e]
    left = jnp.where(uidx < unit_end[N_EXPERTS - 1], blocks_e[unit_e] - MOE_UNIT_BLOCKS * within, 0)
    unit_v0 = (left >= 1).astype(I32)
    unit_v1 = (left >= 2).astype(I32)
    unit_b0 = jnp.where(left >= 1, pad_start[unit_e] // MOE_BLOCK + MOE_UNIT_BLOCKS * within, 0).astype(I32)
    unit_b1 = jnp.where(left >= 2, unit_b0 + 1, unit_b0).astype(I32)
    dest0 = pad_start[route_i[:, ROUTE_E0]] + route_i[:, ROUTE_RANK0]
    dest1 = pad_start[route_i[:, ROUTE_E1]] + route_i[:, ROUTE_RANK1]
    spare = pad_end[N_EXPERTS - 1] // MOE_BLOCK + jnp.arange(N_EXPERTS, dtype=I32)
    zero_start = jnp.concatenate([pad_end - MOE_BLOCK, jnp.minimum(spare, n_blocks - 1) * MOE_BLOCK])
    zero_flag = jnp.concatenate([padded > 0, spare < n_blocks]).astype(I32)
    xb = _dispatch(h, dest0, dest1, zero_start.astype(I32), zero_flag, n_rows)
    yb = _moe_ffn(xb, g, w1, w3, w2, unit_e, unit_b0, unit_b1, unit_v0, unit_v1,
                  zero_start[N_EXPERTS:].astype(I32), zero_flag[N_EXPERTS:], tf=256)
    return _combine(yb, h, route_f, dest0, dest1, final_g)


def _pad_cols(w, width):
    return jnp.concatenate([w, jnp.zeros((w.shape[0], width - w.shape[1]), w.dtype)], axis=1)


def _split_w_in(w_in):
    bw = BRANCH_WIDTH
    d = w_in.shape[0]
    widths = (3 * bw, bw, GDN_HEADS, GDN_HEADS, 3 * bw + RWKV_DECAY_LORA + RWKV_ICLR_LORA + RWKV_GATE_LORA,
              bw, GLA_QW, GLA_QW, bw, bw, GLA_GATE_RANK, N_BRANCH * d)
    offs = [0]
    for w in widths:
        offs.append(offs[-1] + w)
    col = lambda i: w_in[:, offs[i]:offs[i + 1]]
    gdn = jnp.concatenate([col(0), col(1), _pad_cols(jnp.concatenate([col(2), col(3)], axis=1), LANES)], axis=1)
    rw = col(4)
    o = 3 * bw
    rwkv = jnp.concatenate([rw[:, :o],
                            _pad_cols(rw[:, o:o + RWKV_DECAY_LORA], LANES),
                            _pad_cols(rw[:, o + RWKV_DECAY_LORA:o + RWKV_DECAY_LORA + RWKV_ICLR_LORA], LANES),
                            rw[:, o + RWKV_DECAY_LORA + RWKV_ICLR_LORA:]], axis=1)
    pool = col(5)
    gla = jnp.concatenate([col(6), col(7), col(8), col(9), _pad_cols(col(10), LANES)], axis=1)
    gate = col(11)
    return tuple(_bf(w) for w in (gdn, rwkv, pool, gla, gate))


def _mixer(hn, h, layer, batch, seq, w_in, gdn_conv_w, gdn_a_log, gdn_dt_bias, gdn_norm_g,
           rwkv_mu, rwkv_w0, rwkv_w_up, rwkv_a0, rwkv_a_up, rwkv_g_up, rwkv_k_k, rwkv_k_a,
           rwkv_r_k, rwkv_ln_g, rwkv_ln_b, pool_w, pool_scale, gla_f_up, gla_f_bias, gla_norm_g,
           gate_bias, branch_proj, w_out):
    l = layer
    w_gdn, w_rwkv, w_pool, w_gla, w_gate = _split_w_in(w_in[l])
    tm = 512
    p_gdn = _matmul(hn, w_gdn, tm=tm, tn=GDN_WIDTH, name="proj_gdn")
    p_rwkv = _matmul(hn, w_rwkv, tm=tm, tn=RWKV_WIDTH, name="proj_rwkv")
    p_pool = _matmul(hn, w_pool, tm=tm, tn=BRANCH_WIDTH, name="proj_pool")
    p_gla = _matmul(hn, w_gla, tm=tm, tn=GLA_WIDTH, name="proj_gla")
    p_gate = _matmul(hn, w_gate, tm=tm, tn=1024, name="proj_gate")
    seq_tile = 256
    kw = dict(batch=batch, seq=seq, tile=seq_tile)
    y_gdn = _gdn_branch(p_gdn, gdn_conv_w[l], gdn_a_log[l], gdn_dt_bias[l], gdn_norm_g[l], **kw)
    y_rwkv = _rwkv_branch(p_rwkv, rwkv_mu[l], rwkv_w0[l], rwkv_w_up[l], rwkv_a0[l], rwkv_a_up[l],
                          rwkv_g_up[l], rwkv_k_k[l], rwkv_k_a[l], rwkv_r_k[l].reshape(-1),
                          rwkv_ln_g[l], rwkv_ln_b[l], **kw)
    y_pool = _pool_branch(p_pool, pool_w[l], pool_scale[l], **kw)
    y_gla = _gla_branch(p_gla, gla_f_up[l], gla_f_bias[l], gla_norm_g[l], **kw)
    mixed = _merge((y_gdn, y_rwkv, y_pool, y_gla), branch_proj[l], p_gate, gate_bias[l], tm=512, tn=512)
    return _matmul(mixed, _bf(w_out[l]), tm=512, tn=1024, residual=h, name="out_proj")


def kernel(x, norm1_g, w_in, gdn_conv_w, gdn_a_log, gdn_dt_bias, gdn_norm_g, rwkv_mu, rwkv_w0, rwkv_w_up, rwkv_a0, rwkv_a_up, rwkv_g_up, rwkv_k_k, rwkv_k_a, rwkv_r_k, rwkv_ln_g, rwkv_ln_b, pool_w, pool_scale, gla_f_up, gla_f_bias, gla_norm_g, gate_bias, branch_proj, w_out, norm2_g, ffn_w1, ffn_w3, ffn_w2, moe_router, moe_w1, moe_w3, moe_w2, final_norm_g):
    batch, seq, d = x.shape
    depth = norm1_g.shape[0]
    h = x.reshape(batch * seq, d)
    for layer in range(depth):
        hn = _rmsnorm(h, norm1_g[layer], BF16)
        h = _mixer(hn, h, layer, batch, seq, w_in, gdn_conv_w, gdn_a_log, gdn_dt_bias, gdn_norm_g,
                   rwkv_mu, rwkv_w0, rwkv_w_up, rwkv_a0, rwkv_a_up, rwkv_g_up, rwkv_k_k, rwkv_k_a,
                   rwkv_r_k, rwkv_ln_g, rwkv_ln_b, pool_w, pool_scale, gla_f_up, gla_f_bias,
                   gla_norm_g, gate_bias, branch_proj, w_out)
        i = layer // 2
        last = layer == depth - 1
        if layer % 2 == 0:
            h = _ffn(h, norm2_g[layer], _bf(ffn_w1[i]), _bf(ffn_w3[i]), _bf(ffn_w2[i]), tm=512, tf=512)
            if last:
                h = _rmsnorm(h, final_norm_g, F32)
        else:
            h = _moe(h, norm2_g[layer], moe_router[i], moe_w1[i], moe_w3[i], moe_w2[i],
                     final_g=final_norm_g if last else None)
    return h.reshape(batch, seq, d)
```

```python
import functools

import jax
import jax.numpy as jnp
from jax import lax
from jax.experimental import pallas as pl
from jax.experimental.pallas import tpu as pltpu

F32 = jnp.float32
BF16 = jnp.bfloat16
I32 = jnp.int32

NORM_EPS = 1e-6
CHUNK = 64
N_BRANCH = 4
BRANCH_WIDTH = 512
GDN_HEADS = 4
GDN_HEAD_DIM = 128
GDN_CONV = 4
RWKV_HEAD_DIM = 64
RWKV_DECAY_LORA = 64
RWKV_ICLR_LORA = 64
RWKV_GATE_LORA = 128
RWKV_DECAY_SCALE = 0.606531
RWKV_LN_EPS = 64e-5
POOL_WINDOWS = (2, 4, 8, 16)
POOL_MAX_WINDOW = 16
GLA_HEADS = 4
GLA_KEY_DIM = 64
GLA_VAL_DIM = 128
GLA_GATE_RANK = 16
GLA_GATE_NORM = 16.0
N_EXPERTS = 8
MOE_BLOCK = 512

LANES = 128
SUBLANES = 8
VMEM_LIMIT_BYTES = 56 * 1024 * 1024

INV_BLOCK = 16
INV_HI = False


def _cparams(sem):
    return pltpu.CompilerParams(dimension_semantics=sem, vmem_limit_bytes=VMEM_LIMIT_BYTES)


def _dot(a, b):
    return jnp.dot(a, b, preferred_element_type=F32)


def _dot_nt(a, b):
    return lax.dot_general(a, b, (((1,), (1,)), ((), ())), preferred_element_type=F32)


def _dot_tn(a, b):
    return lax.dot_general(a, b, (((0,), (0,)), ((), ())), preferred_element_type=F32)


def _bf(x):
    return x.astype(BF16)


def _split_terms(x, terms):
    out = []
    rem = x
    for i in range(terms):
        xi = _bf(rem)
        out.append(xi)
        if i + 1 < terms:
            rem = rem - xi.astype(F32)
    return out


def _dot_sel_r(x, sel, terms=3):
    parts = [_dot(t, sel) for t in _split_terms(x, terms)]
    return functools.reduce(lambda a, b: a + b, parts)


def _dot_sel_l(sel, x, terms=3):
    parts = [_dot(sel, t) for t in _split_terms(x, terms)]
    return functools.reduce(lambda a, b: a + b, parts)


def _stacked_sel_r(xs, sel, terms=2):
    n = len(xs)
    rows = xs[0].shape[0]
    parts = [_split_terms(x, terms) for x in xs]
    out = _dot(jnp.concatenate([p[t] for t in range(terms) for p in parts], axis=0), sel)
    res = []
    for i in range(n):
        acc = out[i * rows:(i + 1) * rows]
        for t in range(1, terms):
            acc = acc + out[(t * n + i) * rows:(t * n + i + 1) * rows]
        res.append(acc)
    return res


def _dot_hi(a, b):
    ah = _bf(a)
    al = _bf(a - ah.astype(F32))
    bh = _bf(b)
    bl = _bf(b - bh.astype(F32))
    return _dot(ah, bh) + _dot(ah, bl) + _dot(al, bh)


def _sigmoid(x):
    return jax.nn.sigmoid(x)


def _silu(x):
    return x * jax.nn.sigmoid(x)


def _softplus(x):
    return jnp.maximum(x, 0.0) + jnp.log1p(jnp.exp(-jnp.abs(x)))


def _iota2(n, m, dim):
    return lax.broadcasted_iota(I32, (n, m), dim)


def _tri_inclusive_bf16(n):
    r = _iota2(n, n, 0)
    c = _iota2(n, n, 1)
    return (c <= r).astype(BF16)


class _RowBlock:
    def __init__(self, nblk):
        self.nblk = nblk
        n = nblk * CHUNK
        self._shift = CHUNK.bit_length() - 1
        row = _iota2(CHUNK, n, 0)
        colw = _iota2(CHUNK, n, 1) & (CHUNK - 1)
        self.eye = (row == colw).astype(F32)
        self.strict = colw < row
        self.causal = colw <= row
        ish = INV_BLOCK.bit_length() - 1
        self.same_diag_block = (row >> ish) == (colw >> ish)
        self._bd = {}

    def bd_mask(self, w):
        if w not in self._bd:
            n = self.nblk * CHUNK
            rb = _iota2(n, self.nblk * w, 0) >> self._shift
            cb = _iota2(n, self.nblk * w, 1) >> (w.bit_length() - 1)
            self._bd[w] = (rb == cb).astype(BF16)
        return self._bd[w]

    def rhs(self, y, hi):
        mask = self.bd_mask(y.shape[1] // self.nblk)
        yh = _bf(y)
        bdh = jnp.concatenate([yh] * self.nblk, axis=0) * mask
        if not hi:
            return (bdh, None)
        yl = _bf(y - yh.astype(F32))
        return (bdh, jnp.concatenate([yl] * self.nblk, axis=0) * mask)


def _rb_mul(x, rhs):
    bdh, bdl = rhs
    xh = _bf(x)
    out = _dot(xh, bdh)
    if bdl is not None:
        xl = _bf(x - xh.astype(F32))
        out = out + _dot(xh, bdl) + _dot(xl, bdh)
    return out


def _rb_unit_lower_inverse(rb, lows, hi):
    a = [jnp.where(rb.same_diag_block, -l, 0.0) for l in lows]
    off = [jnp.where(rb.same_diag_block, 0.0, l) for l in lows]
    t = [rb.eye + x for x in a]
    p = a
    rp = [rb.rhs(x, hi) for x in p]
    k = 2
    while k < INV_BLOCK:
        p = [_rb_mul(x, r) for x, r in zip(p, rp)]
        rp = [rb.rhs(x, hi) for x in p]
        t = [x + _rb_mul(x, r) for x, r in zip(t, rp)]
        k *= 2
    roff = [rb.rhs(x, hi) for x in off]
    nm = [_rb_mul(x, r) for x, r in zip(t, roff)]
    acc = [rb.eye - x for x in nm]
    rn = [rb.rhs(x, hi) for x in nm]
    pw = nm
    for j in range(2, CHUNK // INV_BLOCK):
        pw = [_rb_mul(x, r) for x, r in zip(pw, rn)]
        acc = [x + y if j % 2 == 0 else x - y for x, y in zip(acc, pw)]
    rt = [rb.rhs(x, hi) for x in t]
    return [_rb_mul(x, r) for x, r in zip(acc, rt)]


def _rmsnorm_kernel(x_ref, g_ref, o_ref):
    x = x_ref[...]
    ms = jnp.mean(x * x, axis=-1, keepdims=True)
    o_ref[...] = ((x * lax.rsqrt(ms + NORM_EPS)) * g_ref[...]).astype(o_ref.dtype)


def _rmsnorm(x, g, out_dtype, tm=512):
    t, d = x.shape
    return pl.pallas_call(
        _rmsnorm_kernel,
        grid=(t // tm,),
        in_specs=[pl.BlockSpec((tm, d), lambda i: (i, 0)),
                  pl.BlockSpec((1, d), lambda i: (0, 0))],
        out_specs=pl.BlockSpec((tm, d), lambda i: (i, 0)),
        out_shape=jax.ShapeDtypeStruct((t, d), out_dtype),
        compiler_params=_cparams(("parallel",)),
        name="rmsnorm",
    )(x, g.reshape(1, d))


def _mm_kernel(a_ref, b_ref, o_ref):
    o_ref[...] = _dot(a_ref[...], b_ref[...]).astype(o_ref.dtype)


def _mm_res_kernel(a_ref, b_ref, r_ref, o_ref):
    o_ref[...] = (r_ref[...] + _dot(a_ref[...], b_ref[...])).astype(o_ref.dtype)


def _matmul(a, b, *, tm, tn, out_dtype=F32, residual=None, name="matmul"):
    m, k = a.shape
    n = b.shape[1]
    grid = (n // tn, m // tm)
    in_specs = [pl.BlockSpec((tm, k), lambda j, i: (i, 0)),
                pl.BlockSpec((k, tn), lambda j, i: (0, j))]
    args = [a, b]
    kern = _mm_kernel
    if residual is not None:
        in_specs.append(pl.BlockSpec((tm, tn), lambda j, i: (i, j)))
        args.append(residual)
        kern = _mm_res_kernel
    return pl.pallas_call(
        kern,
        grid=grid,
        in_specs=in_specs,
        out_specs=pl.BlockSpec((tm, tn), lambda j, i: (i, j)),
        out_shape=jax.ShapeDtypeStruct((m, n), out_dtype),
        compiler_params=_cparams(("parallel", "parallel")),
        name=name,
    )(*args)


GDN_QKV = 3 * BRANCH_WIDTH
GDN_Z_OFF = GDN_QKV
GDN_BA_OFF = GDN_QKV + BRANCH_WIDTH
GDN_WIDTH = GDN_BA_OFF + LANES


def _gdn_kernel(p_ref, cw_ref, alog_ref, dtb_ref, ng_ref, o_ref,
                xbuf, carry, ybuf, bbuf, gbuf, state, *, tile, nb):
    s = pl.program_id(0)

    @pl.when(s == 0)
    def _():
        state[...] = jnp.zeros_like(state)
        carry[...] = jnp.zeros_like(carry)

    cw = cw_ref[...]
    for bi in range(nb):
        xbuf[bi, 0:SUBLANES, :] = carry[bi]
        xbuf[bi, SUBLANES:SUBLANES + tile, :] = p_ref[bi, :, 0:GDN_QKV]
        acc = xbuf[bi, SUBLANES:SUBLANES + tile, :] * cw[GDN_CONV - 1:GDN_CONV, :]
        for j in range(GDN_CONV - 1):
            off = SUBLANES - (GDN_CONV - 1) + j
            acc = acc + xbuf[bi, off:off + tile, :] * cw[j:j + 1, :]
        carry[bi] = xbuf[bi, tile:tile + SUBLANES, :]
        ybuf[bi] = _silu(acc)
        ba = p_ref[bi, :, GDN_BA_OFF:GDN_BA_OFF + LANES]
        bbuf[bi] = _sigmoid(ba)
        gbuf[bi] = -jnp.exp(alog_ref[...]) * _softplus(ba + dtb_ref[...])

    tri = _tri_inclusive_bf16(CHUNK)
    rb = _RowBlock(GDN_HEADS)
    lane_blk = _iota2(CHUNK, GDN_HEADS * CHUNK, 1) >> (CHUNK.bit_length() - 1)
    ng = ng_ref[...]
    hd = GDN_HEAD_DIM
    bw = BRANCH_WIDTH
    heads = range(GDN_HEADS)
    hsl = [slice(h * hd, (h + 1) * hd) for h in heads]

    def per_head_lanes(cols):
        return jnp.concatenate([jnp.broadcast_to(c, (c.shape[0], hd)) for c in cols], axis=1)

    def per_head_blocks(cols):
        out = cols[GDN_HEADS - 1]
        for h in range(GDN_HEADS - 2, -1, -1):
            out = jnp.where(lane_blk == h, cols[h], out)
        return out

    def l2n(x):
        return jnp.concatenate(
            [x[:, s_] * lax.rsqrt(jnp.sum(x[:, s_] * x[:, s_], axis=-1, keepdims=True) + 1e-6) for s_ in hsl],
            axis=1)

    def chunk_body(ci, _):
        r0 = pl.multiple_of(ci * CHUNK, CHUNK)
        rows = pl.ds(r0, CHUNK)
        bs = range(nb)
        yc = [ybuf[bi, rows, :] for bi in bs]
        bet = [bbuf[bi, rows, :] for bi in bs]
        gc = [_dot_sel_l(tri, gbuf[bi, rows, :]) for bi in bs]
        q = [l2n(x[:, 0:bw]) * (hd ** -0.5) for x in yc]
        k = [l2n(x[:, bw:2 * bw]) for x in yc]
        v = [x[:, 2 * bw:3 * bw] for x in yc]
        gcols = [[x[:, GDN_HEADS + h:GDN_HEADS + h + 1] for h in heads] for x in gc]
        glast = [[x[CHUNK - 1:CHUNK, GDN_HEADS + h:GDN_HEADS + h + 1] for h in heads] for x in gc]
        beta_l = [per_head_lanes([x[:, h:h + 1] for h in heads]) for x in bet]
        gcol_l = [per_head_lanes(c) for c in gcols]
        gcol_b = [per_head_blocks(c) for c in gcols]
        grow_b = [jnp.sum(jnp.where(rb.eye > 0, x, 0.0), axis=0, keepdims=True) for x in gcol_b]
        decay = [jnp.where(rb.causal, jnp.exp(jnp.where(rb.causal, x - y, 0.0)), 0.0)
                 for x, y in zip(gcol_b, grow_b)]
        eg = [jnp.exp(x) for x in gcol_l]
        kb = [x * y for x, y in zip(k, beta_l)]
        lhs = [_bf(jnp.concatenate([x, y], axis=0)) for x, y in zip(kb, q)]
        kr = [rb.rhs(x, False)[0] for x in k]
        pm = [_dot_nt(x, y) for x, y in zip(lhs, kr)]
        lower = [jnp.where(rb.strict, x[:CHUNK] * d, 0.0) for x, d in zip(pm, decay)]
        attn = [x[CHUNK:] * d for x, d in zip(pm, decay)]
        minv = _rb_unit_lower_inverse(rb, lower, INV_HI)
        u = [_rb_mul(m, rb.rhs(x * y, INV_HI)) for m, x, y in zip(minv, v, beta_l)]
        w = [_rb_mul(m, rb.rhs(x * y, INV_HI)) for m, x, y in zip(minv, kb, eg)]
        qd = [x * y for x, y in zip(q, eg)]
        it = [(bi, h) for bi in bs for h in heads]
        sts = [state[bi * GDN_HEADS + h] for bi, h in it]
        stb = [_bf(x) for x in sts]
        wq = [_dot(_bf(jnp.concatenate([w[bi][:, hsl[h]], qd[bi][:, hsl[h]]], axis=0)), sb)
              for (bi, h), sb in zip(it, stb)]
        wst = [x[:CHUNK] for x in wq]
        qst = [x[CHUNK:] for x in wq]
        v_new = [jnp.concatenate([u[bi][:, hsl[h]] - wst[bi * GDN_HEADS + h] for h in heads], axis=1) for bi in bs]
        av = [_rb_mul(a, rb.rhs(x, False)) for a, x in zip(attn, v_new)]
        k_tail = [_bf(k[bi][:, hsl[h]] * jnp.exp(glast[bi][h] - gcols[bi][h])) for bi, h in it]
        upd = [_dot_tn(kt, _bf(v_new[bi][:, hsl[h]])) for (bi, h), kt in zip(it, k_tail)]
        for i, (bi, h) in enumerate(it):
            state[i] = sts[i] * jnp.exp(glast[bi][h]) + upd[i]
            o = qst[i] + av[bi][:, hsl[h]]
            z = p_ref[bi, rows, GDN_Z_OFF + h * hd:GDN_Z_OFF + (h + 1) * hd]
            on = (o * lax.rsqrt(jnp.mean(o * o, axis=-1, keepdims=True) + NORM_EPS)) * ng
            o_ref[bi, rows, hsl[h]] = (on * _silu(z)).astype(o_ref.dtype)
        return 0

    lax.fori_loop(0, tile // CHUNK, chunk_body, 0)


def _gdn_branch(proj, conv_w, a_log, dt_bias, norm_g, *, batch, seq, tile):
    zeros = jnp.zeros((LANES,), F32)
    alog_p = zeros.at[GDN_HEADS:2 * GDN_HEADS].set(a_log).reshape(1, LANES)
    dtb_p = zeros.at[GDN_HEADS:2 * GDN_HEADS].set(dt_bias).reshape(1, LANES)
    const = lambda s: (0, 0)
    out = pl.pallas_call(
        functools.partial(_gdn_kernel, tile=tile, nb=batch),
        grid=(seq // tile,),
        in_specs=[pl.BlockSpec((batch, tile, GDN_WIDTH), lambda s: (0, s, 0)),
                  pl.BlockSpec((GDN_CONV, GDN_QKV), const),
                  pl.BlockSpec((1, LANES), const),
                  pl.BlockSpec((1, LANES), const),
                  pl.BlockSpec((1, GDN_HEAD_DIM), const)],
        out_specs=pl.BlockSpec((batch, tile, BRANCH_WIDTH), lambda s: (0, s, 0)),
        out_shape=jax.ShapeDtypeStruct((batch, seq, BRANCH_WIDTH), BF16),
        scratch_shapes=[pltpu.VMEM((batch, tile + SUBLANES, GDN_QKV), F32),
                        pltpu.VMEM((batch, SUBLANES, GDN_QKV), F32),
                        pltpu.VMEM((batch, tile, GDN_QKV), F32),
                        pltpu.VMEM((batch, tile, LANES), F32),
                        pltpu.VMEM((batch, tile, LANES), F32),
                        pltpu.VMEM((batch * GDN_HEADS, GDN_HEAD_DIM, GDN_HEAD_DIM), F32)],
        compiler_params=_cparams(("arbitrary",)),
        name="gdn_branch",
    )(proj.reshape(batch, seq, GDN_WIDTH), conv_w, alog_p, dtb_p, norm_g.reshape(1, GDN_HEAD_DIM))
    return out.reshape(batch * seq, BRANCH_WIDTH)


RWKV_R_OFF = 0
RWKV_K_OFF = BRANCH_WIDTH
RWKV_V_OFF = 2 * BRANCH_WIDTH
RWKV_WD_OFF = 3 * BRANCH_WIDTH
RWKV_AD_OFF = RWKV_WD_OFF + LANES
RWKV_GD_OFF = RWKV_AD_OFF + LANES
RWKV_WIDTH = RWKV_GD_OFF + LANES
RWKV_GROUP_HEADS = 4
RWKV_GROUP_W = RWKV_GROUP_HEADS * RWKV_HEAD_DIM


def _rwkv_kernel(p_ref, mu_ref, w0_ref, wup_ref, a0_ref, aup_ref, gup_ref, kk_ref, ka_ref,
                 rk_ref, lng_ref, lnb_ref, o_ref, xbuf, carry, hsbuf, state, *, tile, nb):
    s = pl.program_id(0)

    @pl.when(s == 0)
    def _():
        state[...] = jnp.zeros_like(state)
        carry[...] = jnp.zeros_like(carry)

    for bi in range(nb):
        xbuf[bi, 0:SUBLANES, :] = carry[bi]
        xbuf[bi, SUBLANES:SUBLANES + tile, :] = p_ref[bi]
        hr = xbuf[bi, SUBLANES:SUBLANES + tile, :]
        prev = xbuf[bi, SUBLANES - 1:SUBLANES - 1 + tile, :]
        carry[bi] = xbuf[bi, tile:tile + SUBLANES, :]
        hsbuf[bi] = hr + (prev - hr) * mu_ref[...]

    tri = _tri_inclusive_bf16(CHUNK)
    rb = _RowBlock(RWKV_GROUP_HEADS)
    gw = RWKV_GROUP_W
    seg = rb.bd_mask(RWKV_HEAD_DIM)
    segf = seg.astype(F32)
    inv_hd = 1.0 / RWKV_HEAD_DIM
    ng = BRANCH_WIDTH // gw
    items = [(bi, gi) for bi in range(nb) for gi in range(ng)]
    sl = [slice(gi * gw, (gi + 1) * gw) for _, gi in items]
    bidx = [bi for bi, _ in items]

    def chunk_body(ci, _):
        r0 = pl.multiple_of(ci * CHUNK, CHUNK)
        rows = pl.ds(r0, CHUNK)
        hs = [hsbuf[bi, rows, :] for bi in range(nb)]
        rv = [x[:, RWKV_R_OFF:RWKV_R_OFF + BRANCH_WIDTH] for x in hs]
        kv = [x[:, RWKV_K_OFF:RWKV_K_OFF + BRANCH_WIDTH] for x in hs]
        vv = [x[:, RWKV_V_OFF:RWKV_V_OFF + BRANCH_WIDTH] for x in hs]
        lora_in = jnp.concatenate([x[:, RWKV_WD_OFF:RWKV_WIDTH] for x in hs], axis=0)
        dw = _dot(_bf(jnp.tanh(lora_in[:, 0:LANES])), wup_ref[...])
        da = _dot(_bf(lora_in[:, LANES:2 * LANES]), aup_ref[...])
        gate_all = _dot(_bf(_sigmoid(lora_in[:, 2 * LANES:3 * LANES])), gup_ref[...])
        rowsl = [slice(bi * CHUNK, (bi + 1) * CHUNK) for bi in range(nb)]
        log_w = [-RWKV_DECAY_SCALE * _sigmoid(w0_ref[...] + dw[r_]) for r_ in rowsl]
        a_lr = [_sigmoid(a0_ref[...] + da[r_]) for r_ in rowsl]
        gate = [gate_all[r_] for r_ in rowsl]
        kkr = [x * kk_ref[...] for x in kv]
        kmod = [x * (1.0 + (a - 1.0) * ka_ref[...]) for x, a in zip(kv, a_lr)]
        rk = [x * y * rk_ref[...] for x, y in zip(rv, kmod)]
        g = [_dot_sel_l(tri, x) for x in log_w]
        egn = [jnp.exp(-x) for x in g]
        glast = [x[CHUNK - 1:CHUNK, :] for x in g]
        etail = [jnp.exp(gl - x) for gl, x in zip(glast, g)]
        dec = [jnp.exp(x) for x in glast]
        r_t = [x * jnp.exp(y) for x, y in zip(rv, g)]
        g_prev = [x - y for x, y in zip(g, log_w)]
        kkss = _stacked_sel_r([kkr[bi][:, s_] * kkr[bi][:, s_] for bi, s_ in zip(bidx, sl)], seg)
        bon = _stacked_sel_r([rk[bi][:, s_] for bi, s_ in zip(bidx, sl)], seg)
        kk = [kkr[bi][:, s_] * lax.rsqrt(x + 1e-6) for bi, s_, x in zip(bidx, sl, kkss)]
        b = [x * a_lr[bi][:, s_] for bi, s_, x in zip(bidx, sl, kk)]
        a_t = [-x * jnp.exp(g_prev[bi][:, s_]) for bi, s_, x in zip(bidx, sl, kk)]
        vs = [vv[bi][:, s_] for bi, s_ in zip(bidx, sl)]
        km = [kmod[bi][:, s_] for bi, s_ in zip(bidx, sl)]
        lhs = [_bf(jnp.concatenate([x, r_t[bi][:, s_]], axis=0)) for bi, s_, x in zip(bidx, sl, a_t)]
        rbt = [rb.rhs(x * egn[bi][:, s_], False)[0] for bi, s_, x in zip(bidx, sl, b)]
        rkt = [rb.rhs(x * egn[bi][:, s_], False)[0] for bi, s_, x in zip(bidx, sl, km)]
        pb = [_dot_nt(x, y) for x, y in zip(lhs, rbt)]
        pk = [_dot_nt(x, y) for x, y in zip(lhs, rkt)]
        minv = _rb_unit_lower_inverse(rb, [jnp.where(rb.strict, -x[:CHUNK], 0.0) for x in pb], INV_HI)
        rv_rhs = [rb.rhs(x, False) for x in vs]
        a_k = [jnp.concatenate([jnp.where(rb.strict, x[:CHUNK], 0.0), jnp.where(rb.causal, x[CHUNK:], 0.0)],
                               axis=0) for x in pk]
        a_kv = [_rb_mul(x, r) for x, r in zip(a_k, rv_rhs)]
        akv = [x[:CHUNK] for x in a_kv]
        arkv = [x[CHUNK:] for x in a_kv]
        a_rb = [jnp.where(rb.causal, x[CHUNK:], 0.0) for x in pb]
        sts = [state[i] for i in range(len(items))]
        init = [_dot_nt(x, _bf(st)) for x, st in zip(lhs, sts)]
        u = [_rb_mul(m, rb.rhs(x[:CHUNK] + y, INV_HI)) for m, x, y in zip(minv, init, akv)]
        y = [x[CHUNK:] + _rb_mul(p, rb.rhs(q, False)) + w for x, p, q, w in zip(init, a_rb, u, arkv)]
        tails = [_bf(jnp.concatenate([x * etail[bi][:, s_], k_ * etail[bi][:, s_]], axis=0))
                 for bi, s_, x, k_ in zip(bidx, sl, b, km)]
        upd = [_dot_tn(_bf(jnp.concatenate([x, v_], axis=0)), t) for x, v_, t in zip(u, vs, tails)]
        for i, (bi, s_) in enumerate(zip(bidx, sl)):
            state[i] = sts[i] * dec[bi][:, s_] + upd[i] * segf
        mean = [x * inv_hd for x in _stacked_sel_r(y, seg)]
        yc = [x - m for x, m in zip(y, mean)]
        var = [x * inv_hd for x in _stacked_sel_r([x * x for x in yc], seg)]
        for i, (bi, s_) in enumerate(zip(bidx, sl)):
            yn = yc[i] * lax.rsqrt(var[i] + RWKV_LN_EPS) * lng_ref[:, s_] + lnb_ref[:, s_]
            o_ref[bi, rows, s_] = ((yn + bon[i] * vs[i]) * gate[bi][:, s_]).astype(o_ref.dtype)
        return 0

    lax.fori_loop(0, tile // CHUNK, chunk_body, 0)


def _pad_rows(w, rows):
    return jnp.zeros((rows,) + w.shape[1:], w.dtype).at[:w.shape[0]].set(w)


def _rwkv_mu_layout(mu):
    z = jnp.zeros((LANES - RWKV_DECAY_LORA,), mu.dtype)
    o = 3 * BRANCH_WIDTH
    return jnp.concatenate([mu[:o], mu[o:o + RWKV_DECAY_LORA], z,
                            mu[o + RWKV_DECAY_LORA:o + RWKV_DECAY_LORA + RWKV_ICLR_LORA], z,
                            mu[o + RWKV_DECAY_LORA + RWKV_ICLR_LORA:]])


def _rwkv_branch(proj, mu, w0, w_up, a0, a_up, g_up, k_k, k_a, r_k, ln_g, ln_b, *, batch, seq, tile):
    bw = BRANCH_WIDTH
    row = lambda x: x.reshape(1, -1).astype(F32)
    const = lambda s: (0, 0)
    vec = pl.BlockSpec((1, bw), const)
    n_state = batch * (bw // RWKV_GROUP_W)
    out = pl.pallas_call(
        functools.partial(_rwkv_kernel, tile=tile, nb=batch),
        grid=(seq // tile,),
        in_specs=[pl.BlockSpec((batch, tile, RWKV_WIDTH), lambda s: (0, s, 0)),
                  pl.BlockSpec((1, RWKV_WIDTH), const),
                  vec, pl.BlockSpec((LANES, bw), const),
                  vec, pl.BlockSpec((LANES, bw), const),
                  pl.BlockSpec((LANES, bw), const),
                  vec, vec, vec, vec, vec],
        out_specs=pl.BlockSpec((batch, tile, bw), lambda s: (0, s, 0)),
        out_shape=jax.ShapeDtypeStruct((batch, seq, bw), BF16),
        scratch_shapes=[pltpu.VMEM((batch, tile + SUBLANES, RWKV_WIDTH), F32),
                        pltpu.VMEM((batch, SUBLANES, RWKV_WIDTH), F32),
                        pltpu.VMEM((batch, tile, RWKV_WIDTH), F32),
                        pltpu.VMEM((n_state, RWKV_GROUP_W, RWKV_GROUP_W), F32)],
        compiler_params=_cparams(("arbitrary",)),
        name="rwkv_branch",
    )(proj.reshape(batch, seq, RWKV_WIDTH), row(_rwkv_mu_layout(mu)), row(w0), _bf(_pad_rows(w_up, LANES)),
      row(a0), _bf(_pad_rows(a_up, LANES)), _bf(g_up), row(k_k), row(k_a), row(r_k), row(ln_g), row(ln_b))
    return out.reshape(batch * seq, bw)


def _pool_kernel(u_ref, pw_ref, ps_ref, o_ref, xbuf, sbuf, carry, *, tile):
    s = pl.program_id(1)

    @pl.when(s == 0)
    def _():
        carry[...] = jnp.zeros_like(carry)

    hist = POOL_MAX_WINDOW
    n = tile + hist
    xbuf[0:hist, :] = carry[...]
    xbuf[hist:n, :] = u_ref[...]
    carry[...] = xbuf[tile:n, :]
    pos = s * tile + _iota2(tile, LANES, 0)
    gw = LANES
    for gi, win in enumerate(POOL_WINDOWS):
        sl = slice(gi * gw, (gi + 1) * gw)
        sbuf[...] = xbuf[:, sl]
        span = 1
        while span < win:
            sbuf[span:n, :] = sbuf[span:n, :] + sbuf[0:n - span, :]
            span *= 2
        x = xbuf[hist:n, sl]
        count = jnp.minimum(pos + 1, win).astype(F32)
        pooled = sbuf[hist:n, :] / count - x
        y = _dot(_bf(pooled), pw_ref[gi])
        o_ref[:, sl] = (y * ps_ref[:, sl]).astype(o_ref.dtype)


def _pool_branch(u, pool_w, pool_scale, *, batch, seq, tile):
    ns = seq // tile
    bw = BRANCH_WIDTH
    return pl.pallas_call(
        functools.partial(_pool_kernel, tile=tile),
        grid=(batch, ns),
        in_specs=[pl.BlockSpec((tile, bw), lambda b, s: (b * ns + s, 0)),
                  pl.BlockSpec((len(POOL_WINDOWS), LANES, LANES), lambda b, s: (0, 0, 0)),
                  pl.BlockSpec((1, bw), lambda b, s: (0, 0))],
        out_specs=pl.BlockSpec((tile, bw), lambda b, s: (b * ns + s, 0)),
        out_shape=jax.ShapeDtypeStruct((batch * seq, bw), BF16),
        scratch_shapes=[pltpu.VMEM((tile + POOL_MAX_WINDOW, bw), F32),
                        pltpu.VMEM((tile + POOL_MAX_WINDOW, LANES), F32),
                        pltpu.VMEM((POOL_MAX_WINDOW, bw), F32)],
        compiler_params=_cparams(("parallel", "arbitrary")),
        name="pool_branch",
    )(u, _bf(pool_w), pool_scale.reshape(1, bw))


GLA_QW = GLA_HEADS * GLA_KEY_DIM
GLA_Q_OFF = 0
GLA_K_OFF = GLA_QW
GLA_V_OFF = 2 * GLA_QW
GLA_G_OFF = GLA_V_OFF + BRANCH_WIDTH
GLA_F_OFF = GLA_G_OFF + BRANCH_WIDTH
GLA_WIDTH = GLA_F_OFF + LANES


def _gla_kernel(p_ref, fup_ref, fb_ref, ng_ref, o_ref, state, *, tile, nb):
    s = pl.program_id(0)

    @pl.when(s == 0)
    def _():
        state[...] = jnp.zeros_like(state)

    tri = _tri_inclusive_bf16(CHUNK)
    rb = _RowBlock(GLA_HEADS)
    vh = _iota2(BRANCH_WIDTH, GLA_QW, 0) >> (GLA_VAL_DIM.bit_length() - 1)
    kh = _iota2(BRANCH_WIDTH, GLA_QW, 1) >> (GLA_KEY_DIM.bit_length() - 1)
    same_head = (vh == kh).astype(F32)
    ng = ng_ref[...]
    dv = GLA_VAL_DIM
    bs = range(nb)

    def chunk_body(ci, _):
        r0 = pl.multiple_of(ci * CHUNK, CHUNK)
        rows = pl.ds(r0, CHUNK)
        fl = jnp.concatenate([p_ref[bi, rows, GLA_F_OFF:GLA_F_OFF + LANES] for bi in bs], axis=0)
        logits = _dot(_bf(fl), fup_ref[...]) + fb_ref[...]
        log_f = -_softplus(-logits) / GLA_GATE_NORM
        gc = [_dot_sel_l(tri, log_f[bi * CHUNK:(bi + 1) * CHUNK]) for bi in bs]
        q = [p_ref[bi, rows, GLA_Q_OFF:GLA_Q_OFF + GLA_QW] * (GLA_KEY_DIM ** -0.5) for bi in bs]
        k = [p_ref[bi, rows, GLA_K_OFF:GLA_K_OFF + GLA_QW] for bi in bs]
        v = [p_ref[bi, rows, GLA_V_OFF:GLA_V_OFF + BRANCH_WIDTH] for bi in bs]
        q_dec = [_bf(x * jnp.exp(g)) for x, g in zip(q, gc)]
        k_dec = [x * jnp.exp(-g) for x, g in zip(k, gc)]
        glast = [g[CHUNK - 1:CHUNK, :] for g in gc]
        k_tail = [_bf(x * jnp.exp(gl - g)) for x, gl, g in zip(k, glast, gc)]
        attn = [_dot_nt(x, rb.rhs(y, False)[0]) for x, y in zip(q_dec, k_dec)]
        intra = [_rb_mul(jnp.where(rb.causal, a, 0.0), rb.rhs(x, False)) for a, x in zip(attn, v)]
        sts = [state[bi] for bi in bs]
        inter = [_dot_nt(x, _bf(st)) for x, st in zip(q_dec, sts)]
        upd = [_dot_tn(_bf(x), y) for x, y in zip(v, k_tail)]
        for bi in bs:
            state[bi] = sts[bi] * jnp.exp(glast[bi]) + upd[bi] * same_head
            o = intra[bi] + inter[bi]
            for h in range(GLA_HEADS):
                oh = o[:, h * dv:(h + 1) * dv]
                gate = p_ref[bi, rows, GLA_G_OFF + h * dv:GLA_G_OFF + (h + 1) * dv]
                on = (oh * lax.rsqrt(jnp.mean(oh * oh, axis=-1, keepdims=True) + NORM_EPS)) * ng
                o_ref[bi, rows, h * dv:(h + 1) * dv] = (on * _silu(gate)).astype(o_ref.dtype)
        return 0

    lax.fori_loop(0, tile // CHUNK, chunk_body, 0)


def _gla_branch(proj, f_up, f_bias, norm_g, *, batch, seq, tile):
    const = lambda s: (0, 0)
    out = pl.pallas_call(
        functools.partial(_gla_kernel, tile=tile, nb=batch),
        grid=(seq // tile,),
        in_specs=[pl.BlockSpec((batch, tile, GLA_WIDTH), lambda s: (0, s, 0)),
                  pl.BlockSpec((LANES, GLA_QW), const),
                  pl.BlockSpec((1, GLA_QW), const),
                  pl.BlockSpec((1, GLA_VAL_DIM), const)],
        out_specs=pl.BlockSpec((batch, tile, BRANCH_WIDTH), lambda s: (0, s, 0)),
        out_shape=jax.ShapeDtypeStruct((batch, seq, BRANCH_WIDTH), BF16),
        scratch_shapes=[pltpu.VMEM((batch, BRANCH_WIDTH, GLA_QW), F32)],
        compiler_params=_cparams(("arbitrary",)),
        name="gla_branch",
    )(proj.reshape(batch, seq, GLA_WIDTH), _bf(_pad_rows(f_up, LANES)), f_bias.reshape(1, GLA_QW),
      norm_g.reshape(1, GLA_VAL_DIM))
    return out.reshape(batch * seq, BRANCH_WIDTH)


def _merge_kernel(hn_ref, y0, y1, y2, y3, bp_ref, wg0, wg1, wg2, wg3, gb_ref, o_ref):
    ys = (y0, y1, y2, y3)
    wgs = (wg0, wg1, wg2, wg3)
    hn = hn_ref[...]
    acc = None
    for i in range(N_BRANCH):
        logit = _dot(hn, wgs[i][...]) + gb_ref[i]
        gate = 0.5 * (jnp.tanh(0.5 * logit) + 1.0)
        term = gate * _dot(ys[i][...], bp_ref[i])
        acc = term if acc is None else acc + term
    o_ref[...] = acc.astype(o_ref.dtype)


def _merge(hn, ys, branch_proj, w_gate, gate_bias, *, tm, tn):
    t, dm = hn.shape
    d = branch_proj.shape[-1]
    nj = d // tn
    y_spec = pl.BlockSpec((tm, BRANCH_WIDTH), lambda j, i: (i, 0))
    wg_specs = [pl.BlockSpec((dm, tn), functools.partial(lambda j, i, b: (0, b * nj + j), b=b))
                for b in range(N_BRANCH)]
    return pl.pallas_call(
        _merge_kernel,
        grid=(nj, t // tm),
        in_specs=[pl.BlockSpec((tm, dm), lambda j, i: (i, 0))]
        + [y_spec] * N_BRANCH
        + [pl.BlockSpec((N_BRANCH, BRANCH_WIDTH, tn), lambda j, i: (0, 0, j))]
        + wg_specs
        + [pl.BlockSpec((N_BRANCH, 1, tn), lambda j, i: (0, 0, j))],
        out_specs=pl.BlockSpec((tm, tn), lambda j, i: (i, j)),
        out_shape=jax.ShapeDtypeStruct((t, d), BF16),
        compiler_params=_cparams(("parallel", "parallel")),
        name="merge",
    )(hn, *ys, _bf(branch_proj), w_gate, w_gate, w_gate, w_gate, gate_bias.reshape(N_BRANCH, 1, d))


def _rms_rows(x, g):
    return (x * lax.rsqrt(jnp.mean(x * x, axis=-1, keepdims=True) + NORM_EPS)) * g


def _ffn_kernel(h_ref, g_ref, w1_ref, w3_ref, w2_ref, o_ref, xn_ref, acc_ref, *, nf):
    f = pl.program_id(1)

    @pl.when(f == 0)
    def _():
        acc_ref[...] = jnp.zeros_like(acc_ref)
        xn_ref[...] = _bf(_rms_rows(h_ref[...], g_ref[...]))

    x = xn_ref[...]
    mid = _bf(_silu(_dot(x, w1_ref[...])) * _dot(x, w3_ref[...]))
    acc_ref[...] += _dot(mid, w2_ref[...])

    @pl.when(f == nf - 1)
    def _():
        o_ref[...] = h_ref[...] + acc_ref[...]


def _ffn(h, g, w1, w3, w2, *, tm, tf):
    t, d = h.shape
    ff = w1.shape[1]
    nf = ff // tf
    return pl.pallas_call(
        functools.partial(_ffn_kernel, nf=nf),
        grid=(t // tm, nf),
        in_specs=[pl.BlockSpec((tm, d), lambda i, f: (i, 0)),
                  pl.BlockSpec((1, d), lambda i, f: (0, 0)),
                  pl.BlockSpec((d, tf), lambda i, f: (0, f)),
                  pl.BlockSpec((d, tf), lambda i, f: (0, f)),
                  pl.BlockSpec((tf, d), lambda i, f: (f, 0))],
        out_specs=pl.BlockSpec((tm, d), lambda i, f: (i, 0)),
        out_shape=jax.ShapeDtypeStruct((t, d), F32),
        scratch_shapes=[pltpu.VMEM((tm, d), BF16), pltpu.VMEM((tm, d), F32)],
        compiler_params=_cparams(("parallel", "arbitrary")),
        name="ffn",
    )(h, g.reshape(1, d), w1, w3, w2)


ROUTE_E0, ROUTE_E1, ROUTE_RANK0, ROUTE_RANK1 = 0, 1, 2, 3
ROUTE_W0, ROUTE_W1 = 0, 1


def _router_kernel(h_ref, g_ref, rw_ref, ri_ref, rf_ref, cnt_ref, run):
    i = pl.program_id(0)

    @pl.when(i == 0)
    def _():
        run[...] = jnp.zeros_like(run)

    tm = h_ref.shape[0]
    hn = _bf(_rms_rows(h_ref[...], g_ref[...]))
    logits = _dot(hn, rw_ref[...])
    lane = _iota2(tm, LANES, 1)
    neg = jnp.float32(-jnp.inf)
    lg = jnp.where(lane < N_EXPERTS, logits, neg)
    m1 = jnp.max(lg, axis=-1, keepdims=True)
    e0 = jnp.min(jnp.where(lg == m1, lane, LANES), axis=-1, keepdims=True)
    lg2 = jnp.where(lane == e0, neg, lg)
    m2 = jnp.max(lg2, axis=-1, keepdims=True)
    e1 = jnp.min(jnp.where(lg2 == m2, lane, LANES), axis=-1, keepdims=True)
    ex = jnp.exp(m2 - m1)
    den = 1.0 + ex
    w0 = 1.0 / den
    w1 = ex / den
    hit0 = lane == e0
    hit1 = lane == e1
    onehot = (hit0 | hit1).astype(F32)
    rr = _iota2(tm, tm, 0)
    cc = _iota2(tm, tm, 1)
    before = _dot((cc < rr).astype(BF16), _bf(onehot)) + run[...]
    rank0 = jnp.sum(jnp.where(hit0, before, 0.0), axis=-1, keepdims=True).astype(I32)
    rank1 = jnp.sum(jnp.where(hit1, before, 0.0), axis=-1, keepdims=True).astype(I32)
    run[...] += jnp.sum(onehot, axis=0, keepdims=True)
    cnt_ref[...] = run[...]
    ri_ref[...] = jnp.where(lane == ROUTE_E0, e0,
                            jnp.where(lane == ROUTE_E1, e1,
                                      jnp.where(lane == ROUTE_RANK0, rank0,
                                                jnp.where(lane == ROUTE_RANK1, rank1, 0))))
    rf_ref[...] = jnp.where(lane == ROUTE_W0, w0, jnp.where(lane == ROUTE_W1, w1, 0.0))


def _router(h, g, router_w, *, tm=512):
    t, d = h.shape
    rw = _bf(jnp.zeros((d, LANES), F32).at[:, :N_EXPERTS].set(router_w))
    return pl.pallas_call(
        _router_kernel,
        grid=(t // tm,),
        in_specs=[pl.BlockSpec((tm, d), lambda i: (i, 0)),
                  pl.BlockSpec((1, d), lambda i: (0, 0)),
                  pl.BlockSpec((d, LANES), lambda i: (0, 0))],
        out_specs=[pl.BlockSpec((tm, LANES), lambda i: (i, 0)),
                   pl.BlockSpec((tm, LANES), lambda i: (i, 0)),
                   pl.BlockSpec((1, LANES), lambda i: (0, 0))],
        out_shape=[jax.ShapeDtypeStruct((t, LANES), I32),
                   jax.ShapeDtypeStruct((t, LANES), F32),
                   jax.ShapeDtypeStruct((1, LANES), F32)],
        scratch_shapes=[pltpu.VMEM((1, LANES), F32)],
        compiler_params=_cparams(("arbitrary",)),
        name="moe_router",
    )(h, g.reshape(1, d), rw)


def _row_copy(src_ref, src_row, dst_ref, dst_row, sem):
    return pltpu.make_async_copy(src_ref.at[pl.ds(src_row, 1)], dst_ref.at[pl.ds(dst_row, 1)], sem)


DMA_ISSUE_UNROLL = 8


def _dispatch_kernel(d0_ref, d1_ref, zs_ref, zf_ref, x_ref, xb_ref, zbuf, sem, zsem):
    tm = x_ref.shape[0]
    step = pl.program_id(0)
    base = step * tm

    @pl.when(step == 0)
    def _():
        zbuf[...] = jnp.zeros_like(zbuf)

        def zero_copy(j):
            first = pl.multiple_of(zs_ref[j], MOE_BLOCK)
            return pltpu.make_async_copy(zbuf, xb_ref.at[pl.ds(first, MOE_BLOCK)], zsem)

        for j in range(2 * N_EXPERTS):
            @pl.when(zf_ref[j] > 0)
            def _():
                zero_copy(j).start()
        for j in range(2 * N_EXPERTS):
            @pl.when(zf_ref[j] > 0)
            def _():
                zero_copy(j).wait()

    def copies(rw):
        return (_row_copy(x_ref, rw, xb_ref, d0_ref[base + rw], sem),
                _row_copy(x_ref, rw, xb_ref, d1_ref[base + rw], sem))

    def start(rw, _):
        for cp in copies(rw):
            cp.start()
        return 0

    def wait(rw, _):
        for cp in copies(rw):
            cp.wait()
        return 0

    lax.fori_loop(0, tm, start, 0, unroll=DMA_ISSUE_UNROLL)
    lax.fori_loop(0, tm, wait, 0, unroll=DMA_ISSUE_UNROLL)


def _dispatch(x, dest0, dest1, zero_start, zero_flag, n_rows, *, tm=512):
    t, d = x.shape
    return pl.pallas_call(
        _dispatch_kernel,
        grid_spec=pltpu.PrefetchScalarGridSpec(
            num_scalar_prefetch=4,
            grid=(t // tm,),
            in_specs=[pl.BlockSpec((tm, d), lambda i, *_: (i, 0))],
            out_specs=pl.BlockSpec(memory_space=pl.ANY),
            scratch_shapes=[pltpu.VMEM((MOE_BLOCK, d), F32),
                            pltpu.SemaphoreType.DMA(()), pltpu.SemaphoreType.DMA(())]),
        out_shape=jax.ShapeDtypeStruct((n_rows, d), F32),
        compiler_params=_cparams(("arbitrary",)),
        name="moe_dispatch",
    )(dest0, dest1, zero_start, zero_flag, x)


MOE_UNIT_BLOCKS = 2


def _moe_ffn_kernel(ue_ref, ub0_ref, ub1_ref, uv0_ref, uv1_ref, zs_ref, zf_ref, x0_ref, x1_ref, g_ref,
                    w1_ref, w3_ref, w2_ref, yb_ref, xn0, xn1, acc0, acc1, sem, *, nf):
    del ue_ref
    u = pl.program_id(0)
    f = pl.program_id(1)
    blocks = ((uv0_ref, ub0_ref, x0_ref, xn0, acc0), (uv1_ref, ub1_ref, x1_ref, xn1, acc1))

    @pl.when((u == 0) & (f == 0))
    def _():
        acc1[...] = jnp.zeros_like(acc1)

        def zero_copy(j):
            first = pl.multiple_of(zs_ref[j], MOE_BLOCK)
            return pltpu.make_async_copy(acc1, yb_ref.at[pl.ds(first, MOE_BLOCK)], sem)

        for j in range(N_EXPERTS):
            @pl.when(zf_ref[j] > 0)
            def _():
                zero_copy(j).start()
        for j in range(N_EXPERTS):
            @pl.when(zf_ref[j] > 0)
            def _():
                zero_copy(j).wait()

    @pl.when(f == 0)
    def _():
        for uv, _, x_ref, xn, acc in blocks:
            @pl.when(uv[u] > 0)
            def _():
                acc[...] = jnp.zeros_like(acc)
                xn[...] = _bf(_rms_rows(x_ref[...], g_ref[...]))

    for uv, _, _, xn, acc in blocks:
        @pl.when(uv[u] > 0)
        def _():
            x = xn[...]
            mid = _bf(_silu(_dot(x, _bf(w1_ref[0]))) * _dot(x, _bf(w3_ref[0])))
            acc[...] += _dot(mid, _bf(w2_ref[0]))

    @pl.when(f == nf - 1)
    def _():
        def out_copy(ub, acc):
            first = pl.multiple_of(ub[u] * MOE_BLOCK, MOE_BLOCK)
            return pltpu.make_async_copy(acc, yb_ref.at[pl.ds(first, MOE_BLOCK)], sem)

        for uv, ub, _, _, acc in blocks:
            @pl.when(uv[u] > 0)
            def _():
                out_copy(ub, acc).start()
        for uv, ub, _, _, acc in blocks:
            @pl.when(uv[u] > 0)
            def _():
                out_copy(ub, acc).wait()


def _moe_ffn(xb, g, w1, w3, w2, unit_e, unit_b0, unit_b1, unit_v0, unit_v1, spare_start, spare_flag, *, tf):
    n_rows, d = xb.shape
    ff = w1.shape[-1]
    nf = ff // tf
    n_units = unit_e.shape[0]
    wcol = lambda u, f, ue, b0, b1, v0, *_: (ue[u], 0, f * v0[u])
    wrow = lambda u, f, ue, b0, b1, v0, *_: (ue[u], f * v0[u], 0)
    return pl.pallas_call(
        functools.partial(_moe_ffn_kernel, nf=nf),
        grid_spec=pltpu.PrefetchScalarGridSpec(
            num_scalar_prefetch=7,
            grid=(n_units, nf),
            in_specs=[pl.BlockSpec((MOE_BLOCK, d), lambda u, f, ue, b0, *_: (b0[u], 0),
                                   pipeline_mode=pl.Buffered(1)),
                      pl.BlockSpec((MOE_BLOCK, d), lambda u, f, ue, b0, b1, *_: (b1[u], 0),
                                   pipeline_mode=pl.Buffered(1)),
                      pl.BlockSpec((1, d), lambda u, f, *_: (0, 0)),
                      pl.BlockSpec((1, d, tf), wcol),
                      pl.BlockSpec((1, d, tf), wcol),
                      pl.BlockSpec((1, tf, d), wrow)],
            out_specs=pl.BlockSpec(memory_space=pl.ANY),
            scratch_shapes=[pltpu.VMEM((MOE_BLOCK, d), BF16), pltpu.VMEM((MOE_BLOCK, d), BF16),
                            pltpu.VMEM((MOE_BLOCK, d), F32), pltpu.VMEM((MOE_BLOCK, d), F32),
                            pltpu.SemaphoreType.DMA(())]),
        out_shape=jax.ShapeDtypeStruct((n_rows, d), F32),
        compiler_params=_cparams(("arbitrary", "arbitrary")),
        name="moe_experts",
    )(unit_e, unit_b0, unit_b1, unit_v0, unit_v1, spare_start, spare_flag, xb, xb, g.reshape(1, d), w1, w3, w2)


def _combine_kernel(d0_ref, d1_ref, yb_ref, h_ref, rf_ref, g_ref, o_ref, buf0, buf1, sem, *, final_norm):
    tm = h_ref.shape[0]
    base = pl.program_id(0) * tm

    def copies(rw):
        return (_row_copy(yb_ref, d0_ref[base + rw], buf0, rw, sem),
                _row_copy(yb_ref, d1_ref[base + rw], buf1, rw, sem))

    def start(rw, _):
        for cp in copies(rw):
            cp.start()
        return 0

    def wait(rw, _):
        for cp in copies(rw):
            cp.wait()
        return 0

    lax.fori_loop(0, tm, start, 0, unroll=DMA_ISSUE_UNROLL)
    lax.fori_loop(0, tm, wait, 0, unroll=DMA_ISSUE_UNROLL)
    w0 = rf_ref[:, ROUTE_W0:ROUTE_W0 + 1]
    w1 = rf_ref[:, ROUTE_W1:ROUTE_W1 + 1]
    out = h_ref[...] + (buf0[...] * w0 + buf1[...] * w1)
    o_ref[...] = _rms_rows(out, g_ref[...]) if final_norm else out


def _combine(yb, h, route_f, dest0, dest1, final_g, *, tm=256):
    t, d = h.shape
    g = jnp.ones((1, d), F32) if final_g is None else final_g.reshape(1, d)
    return pl.pallas_call(
        functools.partial(_combine_kernel, final_norm=final_g is not None),
        grid_spec=pltpu.PrefetchScalarGridSpec(
            num_scalar_prefetch=2,
            grid=(t // tm,),
            in_specs=[pl.BlockSpec(memory_space=pl.ANY),
                      pl.BlockSpec((tm, d), lambda i, d0, d1: (i, 0)),
                      pl.BlockSpec((tm, LANES), lambda i, d0, d1: (i, 0)),
                      pl.BlockSpec((1, d), lambda i, d0, d1: (0, 0))],
            out_specs=pl.BlockSpec((tm, d), lambda i, d0, d1: (i, 0)),
            scratch_shapes=[pltpu.VMEM((tm, d), F32), pltpu.VMEM((tm, d), F32),
                            pltpu.SemaphoreType.DMA(())]),
        out_shape=jax.ShapeDtypeStruct((t, d), F32),
        compiler_params=_cparams(("arbitrary",)),
        name="moe_combine",
    )(dest0, dest1, yb, h, route_f, g)


def _moe(h, g, router_w, w1, w3, w2, final_g=None):
    t, d = h.shape
    route_i, route_f, counts_f = _router(h, g, router_w)
    counts = counts_f[0, :N_EXPERTS].astype(I32)
    padded = (counts + MOE_BLOCK - 1) // MOE_BLOCK * MOE_BLOCK
    pad_end = jnp.cumsum(padded)
    pad_start = pad_end - padded
    n_rows = (-(-(t * 2) // MOE_BLOCK) + N_EXPERTS) * MOE_BLOCK
    n_blocks = n_rows // MOE_BLOCK
    blocks_e = padded // MOE_BLOCK
    units_e = (blocks_e + MOE_UNIT_BLOCKS - 1) // MOE_UNIT_BLOCKS
    unit_end = jnp.cumsum(units_e)
    n_units = n_blocks // MOE_UNIT_BLOCKS + N_EXPERTS
    uidx = jnp.arange(n_units, dtype=I32)
    unit_e = jnp.minimum(jnp.sum((uidx[:, None] >= unit_end[None, :]).astype(I32), axis=1), N_EXPERTS - 1)
    within = uidx - (unit_end - units_e)[unit_e]
    left = jnp.where(uidx < unit_end[N_EXPERTS - 1], blocks_e[unit_e] - MOE_UNIT_BLOCKS * within, 0)
    unit_v0 = (left >= 1).astype(I32)
    unit_v1 = (left >= 2).astype(I32)
    unit_b0 = jnp.where(left >= 1, pad_start[unit_e] // MOE_BLOCK + MOE_UNIT_BLOCKS * within, 0).astype(I32)
    unit_b1 = jnp.where(left >= 2, unit_b0 + 1, unit_b0).astype(I32)
    dest0 = pad_start[route_i[:, ROUTE_E0]] + route_i[:, ROUTE_RANK0]
    dest1 = pad_start[route_i[:, ROUTE_E1]] + route_i[:, ROUTE_RANK1]
    spare = pad_end[N_EXPERTS - 1] // MOE_BLOCK + jnp.arange(N_EXPERTS, dtype=I32)
    zero_start = jnp.concatenate([pad_end - MOE_BLOCK, jnp.minimum(spare, n_blocks - 1) * MOE_BLOCK])
    zero_flag = jnp.concatenate([padded > 0, spare < n_blocks]).astype(I32)
    xb = _dispatch(h, dest0, dest1, zero_start.astype(I32), zero_flag, n_rows)
    yb = _moe_ffn(xb, g, w1, w3, w2, unit_e, unit_b0, unit_b1, unit_v0, unit_v1,
                  zero_start[N_EXPERTS:].astype(I32), zero_flag[N_EXPERTS:], tf=512)
    return _combine(yb, h, route_f, dest0, dest1, final_g)


def _pad_cols(w, width):
    return jnp.concatenate([w, jnp.zeros((w.shape[0], width - w.shape[1]), w.dtype)], axis=1)


def _split_w_in(w_in):
    bw = BRANCH_WIDTH
    d = w_in.shape[0]
    widths = (3 * bw, bw, GDN_HEADS, GDN_HEADS, 3 * bw + RWKV_DECAY_LORA + RWKV_ICLR_LORA + RWKV_GATE_LORA,
              bw, GLA_QW, GLA_QW, bw, bw, GLA_GATE_RANK, N_BRANCH * d)
    offs = [0]
    for w in widths:
        offs.append(offs[-1] + w)
    col = lambda i: w_in[:, offs[i]:offs[i + 1]]
    gdn = jnp.concatenate([col(0), col(1), _pad_cols(jnp.concatenate([col(2), col(3)], axis=1), LANES)], axis=1)
    rw = col(4)
    o = 3 * bw
    rwkv = jnp.concatenate([rw[:, :o],
                            _pad_cols(rw[:, o:o + RWKV_DECAY_LORA], LANES),
                            _pad_cols(rw[:, o + RWKV_DECAY_LORA:o + RWKV_DECAY_LORA + RWKV_ICLR_LORA], LANES),
                            rw[:, o + RWKV_DECAY_LORA + RWKV_ICLR_LORA:]], axis=1)
    pool = col(5)
    gla = jnp.concatenate([col(6), col(7), col(8), col(9), _pad_cols(col(10), LANES)], axis=1)
    gate = col(11)
    return tuple(_bf(w) for w in (gdn, rwkv, pool, gla, gate))


def _mixer(hn, h, layer, batch, seq, w_in, gdn_conv_w, gdn_a_log, gdn_dt_bias, gdn_norm_g,
           rwkv_mu, rwkv_w0, rwkv_w_up, rwkv_a0, rwkv_a_up, rwkv_g_up, rwkv_k_k, rwkv_k_a,
           rwkv_r_k, rwkv_ln_g, rwkv_ln_b, pool_w, pool_scale, gla_f_up, gla_f_bias, gla_norm_g,
           gate_bias, branch_proj, w_out):
    l = layer
    w_gdn, w_rwkv, w_pool, w_gla, w_gate = _split_w_in(w_in[l])
    tm = 512
    p_gdn = _matmul(hn, w_gdn, tm=tm, tn=GDN_WIDTH, name="proj_gdn")
    p_rwkv = _matmul(hn, w_rwkv, tm=tm, tn=RWKV_WIDTH, name="proj_rwkv")
    p_pool = _matmul(hn, w_pool, tm=tm, tn=BRANCH_WIDTH, name="proj_pool")
    p_gla = _matmul(hn, w_gla, tm=tm, tn=GLA_WIDTH, name="proj_gla")
    seq_tile = 256
    kw = dict(batch=batch, seq=seq, tile=seq_tile)
    y_gdn = _gdn_branch(p_gdn, gdn_conv_w[l], gdn_a_log[l], gdn_dt_bias[l], gdn_norm_g[l], **kw)
    y_rwkv = _rwkv_branch(p_rwkv, rwkv_mu[l], rwkv_w0[l], rwkv_w_up[l], rwkv_a0[l], rwkv_a_up[l],
                          rwkv_g_up[l], rwkv_k_k[l], rwkv_k_a[l], rwkv_r_k[l].reshape(-1),
                          rwkv_ln_g[l], rwkv_ln_b[l], **kw)
    y_pool = _pool_branch(p_pool, pool_w[l], pool_scale[l], **kw)
    y_gla = _gla_branch(p_gla, gla_f_up[l], gla_f_bias[l], gla_norm_g[l], **kw)
    mixed = _merge(hn, (y_gdn, y_rwkv, y_pool, y_gla), branch_proj[l], w_gate, gate_bias[l], tm=512, tn=512)
    return _matmul(mixed, _bf(w_out[l]), tm=512, tn=1024, residual=h, name="out_proj")


def kernel(x, norm1_g, w_in, gdn_conv_w, gdn_a_log, gdn_dt_bias, gdn_norm_g, rwkv_mu, rwkv_w0, rwkv_w_up, rwkv_a0, rwkv_a_up, rwkv_g_up, rwkv_k_k, rwkv_k_a, rwkv_r_k, rwkv_ln_g, rwkv_ln_b, pool_w, pool_scale, gla_f_up, gla_f_bias, gla_norm_g, gate_bias, branch_proj, w_out, norm2_g, ffn_w1, ffn_w3, ffn_w2, moe_router, moe_w1, moe_w3, moe_w2, final_norm_g):
    batch, seq, d = x.shape
    depth = norm1_g.shape[0]
    h = x.reshape(batch * seq, d)
    for layer in range(depth):
        hn = _rmsnorm(h, norm1_g[layer], BF16)
        h = _mixer(hn, h, layer, batch, seq, w_in, gdn_conv_w, gdn_a_log, gdn_dt_bias, gdn_norm_g,
                   rwkv_mu, rwkv_w0, rwkv_w_up, rwkv_a0, rwkv_a_up, rwkv_g_up, rwkv_k_k, rwkv_k_a,
                   rwkv_r_k, rwkv_ln_g, rwkv_ln_b, pool_w, pool_scale, gla_f_up, gla_f_bias,
                   gla_norm_g, gate_bias, branch_proj, w_out)
        i = layer // 2
        last = layer == depth - 1
        if layer % 2 == 0:
            h = _ffn(h, norm2_g[layer], _bf(ffn_w1[i]), _bf(ffn_w3[i]), _bf(ffn_w2[i]), tm=512, tf=512)
            if last:
                h = _rmsnorm(h, final_norm_g, F32)
        else:
            h = _moe(h, norm2_g[layer], moe_router[i], moe_w1[i], moe_w3[i], moe_w2[i],
                     final_g=final_norm_g if last else None)
    return h.reshape(batch, seq, d)
```

```python
import functools

import jax
import jax.numpy as jnp
from jax import lax
from jax.experimental import pallas as pl
from jax.experimental.pallas import tpu as pltpu

F32 = jnp.float32
BF16 = jnp.bfloat16
I32 = jnp.int32

NORM_EPS = 1e-6
CHUNK = 64
N_BRANCH = 4
BRANCH_WIDTH = 512
GDN_HEADS = 4
GDN_HEAD_DIM = 128
GDN_CONV = 4
RWKV_HEAD_DIM = 64
RWKV_DECAY_LORA = 64
RWKV_ICLR_LORA = 64
RWKV_GATE_LORA = 128
RWKV_DECAY_SCALE = 0.606531
RWKV_LN_EPS = 64e-5
POOL_WINDOWS = (2, 4, 8, 16)
POOL_MAX_WINDOW = 16
GLA_HEADS = 4
GLA_KEY_DIM = 64
GLA_VAL_DIM = 128
GLA_GATE_RANK = 16
GLA_GATE_NORM = 16.0
N_EXPERTS = 8
MOE_BLOCK = 512

LANES = 128
SUBLANES = 8
VMEM_LIMIT_BYTES = 56 * 1024 * 1024

INV_BLOCK = 16
INV_HI = False
SEQ_CHUNKS_PER_STEP = 2


def _cparams(sem):
    return pltpu.CompilerParams(dimension_semantics=sem, vmem_limit_bytes=VMEM_LIMIT_BYTES)


def _dot(a, b):
    return jnp.dot(a, b, preferred_element_type=F32)


def _dot_nt(a, b):
    return lax.dot_general(a, b, (((1,), (1,)), ((), ())), preferred_element_type=F32)


def _dot_tn(a, b):
    return lax.dot_general(a, b, (((0,), (0,)), ((), ())), preferred_element_type=F32)


def _bf(x):
    return x.astype(BF16)


def _split_terms(x, terms):
    out = []
    rem = x
    for i in range(terms):
        xi = _bf(rem)
        out.append(xi)
        if i + 1 < terms:
            rem = rem - xi.astype(F32)
    return out


def _dot_sel_r(x, sel, terms=3):
    parts = [_dot(t, sel) for t in _split_terms(x, terms)]
    return functools.reduce(lambda a, b: a + b, parts)


def _dot_sel_l(sel, x, terms=3):
    parts = [_dot(sel, t) for t in _split_terms(x, terms)]
    return functools.reduce(lambda a, b: a + b, parts)


def _stacked_sel_r(xs, sel, terms=2):
    n = len(xs)
    rows = xs[0].shape[0]
    parts = [_split_terms(x, terms) for x in xs]
    out = _dot(jnp.concatenate([p[t] for t in range(terms) for p in parts], axis=0), sel)
    res = []
    for i in range(n):
        acc = out[i * rows:(i + 1) * rows]
        for t in range(1, terms):
            acc = acc + out[(t * n + i) * rows:(t * n + i + 1) * rows]
        res.append(acc)
    return res


def _dot_hi(a, b):
    ah = _bf(a)
    al = _bf(a - ah.astype(F32))
    bh = _bf(b)
    bl = _bf(b - bh.astype(F32))
    return _dot(ah, bh) + _dot(ah, bl) + _dot(al, bh)


def _sigmoid(x):
    return jax.nn.sigmoid(x)


def _silu(x):
    return x * jax.nn.sigmoid(x)


def _softplus(x):
    return jnp.maximum(x, 0.0) + jnp.log1p(jnp.exp(-jnp.abs(x)))


def _iota2(n, m, dim):
    return lax.broadcasted_iota(I32, (n, m), dim)


def _tri_inclusive_bf16(n):
    r = _iota2(n, n, 0)
    c = _iota2(n, n, 1)
    return (c <= r).astype(BF16)


class _RowBlock:
    def __init__(self, nblk):
        self.nblk = nblk
        n = nblk * CHUNK
        self._shift = CHUNK.bit_length() - 1
        row = _iota2(CHUNK, n, 0)
        colw = _iota2(CHUNK, n, 1) & (CHUNK - 1)
        self.eye = (row == colw).astype(F32)
        self.strict = colw < row
        self.causal = colw <= row
        ish = INV_BLOCK.bit_length() - 1
        self.same_diag_block = (row >> ish) == (colw >> ish)
        self._bd = {}

    def bd_mask(self, w):
        if w not in self._bd:
            n = self.nblk * CHUNK
            rb = _iota2(n, self.nblk * w, 0) >> self._shift
            cb = _iota2(n, self.nblk * w, 1) >> (w.bit_length() - 1)
            self._bd[w] = (rb == cb).astype(BF16)
        return self._bd[w]

    def rhs(self, y, hi):
        mask = self.bd_mask(y.shape[1] // self.nblk)
        yh = _bf(y)
        bdh = jnp.concatenate([yh] * self.nblk, axis=0) * mask
        if not hi:
            return (bdh, None)
        yl = _bf(y - yh.astype(F32))
        return (bdh, jnp.concatenate([yl] * self.nblk, axis=0) * mask)


def _rb_mul(x, rhs):
    bdh, bdl = rhs
    xh = _bf(x)
    out = _dot(xh, bdh)
    if bdl is not None:
        xl = _bf(x - xh.astype(F32))
        out = out + _dot(xh, bdl) + _dot(xl, bdh)
    return out


def _rb_unit_lower_inverse(rb, lows, hi):
    a = [jnp.where(rb.same_diag_block, -l, 0.0) for l in lows]
    off = [jnp.where(rb.same_diag_block, 0.0, l) for l in lows]
    t = [rb.eye + x for x in a]
    p = a
    rp = [rb.rhs(x, hi) for x in p]
    k = 2
    while k < INV_BLOCK:
        p = [_rb_mul(x, r) for x, r in zip(p, rp)]
        rp = [rb.rhs(x, hi) for x in p]
        t = [x + _rb_mul(x, r) for x, r in zip(t, rp)]
        k *= 2
    roff = [rb.rhs(x, hi) for x in off]
    nm = [_rb_mul(x, r) for x, r in zip(t, roff)]
    acc = [rb.eye - x for x in nm]
    rn = [rb.rhs(x, hi) for x in nm]
    pw = nm
    for j in range(2, CHUNK // INV_BLOCK):
        pw = [_rb_mul(x, r) for x, r in zip(pw, rn)]
        acc = [x + y if j % 2 == 0 else x - y for x, y in zip(acc, pw)]
    rt = [rb.rhs(x, hi) for x in t]
    return [_rb_mul(x, r) for x, r in zip(acc, rt)]


def _rmsnorm_kernel(x_ref, g_ref, o_ref):
    x = x_ref[...]
    ms = jnp.mean(x * x, axis=-1, keepdims=True)
    o_ref[...] = ((x * lax.rsqrt(ms + NORM_EPS)) * g_ref[...]).astype(o_ref.dtype)


def _rmsnorm(x, g, out_dtype, tm=512):
    t, d = x.shape
    return pl.pallas_call(
        _rmsnorm_kernel,
        grid=(t // tm,),
        in_specs=[pl.BlockSpec((tm, d), lambda i: (i, 0)),
                  pl.BlockSpec((1, d), lambda i: (0, 0))],
        out_specs=pl.BlockSpec((tm, d), lambda i: (i, 0)),
        out_shape=jax.ShapeDtypeStruct((t, d), out_dtype),
        compiler_params=_cparams(("parallel",)),
        name="rmsnorm",
    )(x, g.reshape(1, d))


def _mm_kernel(a_ref, b_ref, o_ref):
    o_ref[...] = _dot(a_ref[...], b_ref[...]).astype(o_ref.dtype)


def _mm_res_kernel(a_ref, b_ref, r_ref, o_ref):
    o_ref[...] = (r_ref[...] + _dot(a_ref[...], b_ref[...])).astype(o_ref.dtype)


def _matmul(a, b, *, tm, tn, out_dtype=F32, residual=None, name="matmul"):
    m, k = a.shape
    n = b.shape[1]
    grid = (n // tn, m // tm)
    in_specs = [pl.BlockSpec((tm, k), lambda j, i: (i, 0)),
                pl.BlockSpec((k, tn), lambda j, i: (0, j))]
    args = [a, b]
    kern = _mm_kernel
    if residual is not None:
        in_specs.append(pl.BlockSpec((tm, tn), lambda j, i: (i, j)))
        args.append(residual)
        kern = _mm_res_kernel
    return pl.pallas_call(
        kern,
        grid=grid,
        in_specs=in_specs,
        out_specs=pl.BlockSpec((tm, tn), lambda j, i: (i, j)),
        out_shape=jax.ShapeDtypeStruct((m, n), out_dtype),
        compiler_params=_cparams(("parallel", "parallel")),
        name=name,
    )(*args)


GDN_QKV = 3 * BRANCH_WIDTH
GDN_Z_OFF = GDN_QKV
GDN_BA_OFF = GDN_QKV + BRANCH_WIDTH
GDN_WIDTH = GDN_BA_OFF + LANES


def _gdn_kernel(p_ref, cw_ref, alog_ref, dtb_ref, ng_ref, o_ref,
                xbuf, carry, ybuf, bbuf, gbuf, state, *, tile, nb):
    s = pl.program_id(0)

    @pl.when(s == 0)
    def _():
        state[...] = jnp.zeros_like(state)
        carry[...] = jnp.zeros_like(carry)

    cw = cw_ref[...]
    for bi in range(nb):
        xbuf[bi, 0:SUBLANES, :] = carry[bi]
        xbuf[bi, SUBLANES:SUBLANES + tile, :] = p_ref[bi, :, 0:GDN_QKV]
        acc = xbuf[bi, SUBLANES:SUBLANES + tile, :] * cw[GDN_CONV - 1:GDN_CONV, :]
        for j in range(GDN_CONV - 1):
            off = SUBLANES - (GDN_CONV - 1) + j
            acc = acc + xbuf[bi, off:off + tile, :] * cw[j:j + 1, :]
        carry[bi] = xbuf[bi, tile:tile + SUBLANES, :]
        ybuf[bi] = _silu(acc)
        ba = p_ref[bi, :, GDN_BA_OFF:GDN_BA_OFF + LANES]
        bbuf[bi] = _sigmoid(ba)
        gbuf[bi] = -jnp.exp(alog_ref[...]) * _softplus(ba + dtb_ref[...])

    tri = _tri_inclusive_bf16(CHUNK)
    rb = _RowBlock(GDN_HEADS)
    lane_blk = _iota2(CHUNK, GDN_HEADS * CHUNK, 1) >> (CHUNK.bit_length() - 1)
    ng = ng_ref[...]
    hd = GDN_HEAD_DIM
    bw = BRANCH_WIDTH
    heads = range(GDN_HEADS)
    hsl = [slice(h * hd, (h + 1) * hd) for h in heads]

    def per_head_lanes(cols):
        return jnp.concatenate([jnp.broadcast_to(c, (c.shape[0], hd)) for c in cols], axis=1)

    def per_head_blocks(cols):
        out = cols[GDN_HEADS - 1]
        for h in range(GDN_HEADS - 2, -1, -1):
            out = jnp.where(lane_blk == h, cols[h], out)
        return out

    def l2n(x):
        return jnp.concatenate(
            [x[:, s_] * lax.rsqrt(jnp.sum(x[:, s_] * x[:, s_], axis=-1, keepdims=True) + 1e-6) for s_ in hsl],
            axis=1)

    def chunk_body(ci, _):
        base = pl.multiple_of(ci * (SEQ_CHUNKS_PER_STEP * CHUNK), SEQ_CHUNKS_PER_STEP * CHUNK)
        vrows = [(pl.ds(base + c * CHUNK, CHUNK), bi) for c in range(SEQ_CHUNKS_PER_STEP) for bi in range(nb)]
        yc = [ybuf[bi, r_, :] for r_, bi in vrows]
        bet = [bbuf[bi, r_, :] for r_, bi in vrows]
        gc = [_dot_sel_l(tri, gbuf[bi, r_, :]) for r_, bi in vrows]
        q = [l2n(x[:, 0:bw]) * (hd ** -0.5) for x in yc]
        k = [l2n(x[:, bw:2 * bw]) for x in yc]
        v = [x[:, 2 * bw:3 * bw] for x in yc]
        gcols = [[x[:, GDN_HEADS + h:GDN_HEADS + h + 1] for h in heads] for x in gc]
        glast = [[x[CHUNK - 1:CHUNK, GDN_HEADS + h:GDN_HEADS + h + 1] for h in heads] for x in gc]
        beta_l = [per_head_lanes([x[:, h:h + 1] for h in heads]) for x in bet]
        gcol_l = [per_head_lanes(c) for c in gcols]
        gcol_b = [per_head_blocks(c) for c in gcols]
        grow_b = [jnp.sum(jnp.where(rb.eye > 0, x, 0.0), axis=0, keepdims=True) for x in gcol_b]
        decay = [jnp.where(rb.causal, jnp.exp(jnp.where(rb.causal, x - y, 0.0)), 0.0)
                 for x, y in zip(gcol_b, grow_b)]
        eg = [jnp.exp(x) for x in gcol_l]
        kb = [x * y for x, y in zip(k, beta_l)]
        lhs = [_bf(jnp.concatenate([x, y], axis=0)) for x, y in zip(kb, q)]
        kr = [rb.rhs(x, False)[0] for x in k]
        pm = [_dot_nt(x, y) for x, y in zip(lhs, kr)]
        lower = [jnp.where(rb.strict, x[:CHUNK] * d, 0.0) for x, d in zip(pm, decay)]
        attn = [x[CHUNK:] * d for x, d in zip(pm, decay)]
        minv = _rb_unit_lower_inverse(rb, lower, INV_HI)
        u = [_rb_mul(m, rb.rhs(x * y, INV_HI)) for m, x, y in zip(minv, v, beta_l)]
        w = [_rb_mul(m, rb.rhs(x * y, INV_HI)) for m, x, y in zip(minv, kb, eg)]
        qd = [x * y for x, y in zip(q, eg)]
        k_tail = [[_bf(k[vi][:, hsl[h]] * jnp.exp(glast[vi][h] - gcols[vi][h])) for h in heads]
                  for vi in range(len(vrows))]
        for c in range(SEQ_CHUNKS_PER_STEP):
            it = [(c * nb + bi, bi * GDN_HEADS + h, h) for bi in range(nb) for h in heads]
            sts = [state[si] for _, si, _ in it]
            stb = [_bf(x) for x in sts]
            wq = [_dot(_bf(jnp.concatenate([w[vi][:, hsl[h]], qd[vi][:, hsl[h]]], axis=0)), sb)
                  for (vi, _, h), sb in zip(it, stb)]
            v_new = {vi: jnp.concatenate([u[vi][:, hsl[h]] - wq[bi * GDN_HEADS + h][:CHUNK] for h in heads], axis=1)
                     for bi, vi in enumerate(range(c * nb, (c + 1) * nb))}
            av = {vi: _rb_mul(attn[vi], rb.rhs(x, False)) for vi, x in v_new.items()}
            upd = [_dot_tn(k_tail[vi][h], _bf(v_new[vi][:, hsl[h]])) for vi, _, h in it]
            for i, (vi, si, h) in enumerate(it):
                r_, bi = vrows[vi]
                state[si] = sts[i] * jnp.exp(glast[vi][h]) + upd[i]
                o = wq[i][CHUNK:] + av[vi][:, hsl[h]]
                z = p_ref[bi, r_, GDN_Z_OFF + h * hd:GDN_Z_OFF + (h + 1) * hd]
                on = (o * lax.rsqrt(jnp.mean(o * o, axis=-1, keepdims=True) + NORM_EPS)) * ng
                o_ref[bi, r_, hsl[h]] = (on * _silu(z)).astype(o_ref.dtype)
        return 0

    lax.fori_loop(0, tile // (SEQ_CHUNKS_PER_STEP * CHUNK), chunk_body, 0)


def _gdn_branch(proj, conv_w, a_log, dt_bias, norm_g, *, batch, seq, tile):
    zeros = jnp.zeros((LANES,), F32)
    alog_p = zeros.at[GDN_HEADS:2 * GDN_HEADS].set(a_log).reshape(1, LANES)
    dtb_p = zeros.at[GDN_HEADS:2 * GDN_HEADS].set(dt_bias).reshape(1, LANES)
    const = lambda s: (0, 0)
    out = pl.pallas_call(
        functools.partial(_gdn_kernel, tile=tile, nb=batch),
        grid=(seq // tile,),
        in_specs=[pl.BlockSpec((batch, tile, GDN_WIDTH), lambda s: (0, s, 0)),
                  pl.BlockSpec((GDN_CONV, GDN_QKV), const),
                  pl.BlockSpec((1, LANES), const),
                  pl.BlockSpec((1, LANES), const),
                  pl.BlockSpec((1, GDN_HEAD_DIM), const)],
        out_specs=pl.BlockSpec((batch, tile, BRANCH_WIDTH), lambda s: (0, s, 0)),
        out_shape=jax.ShapeDtypeStruct((batch, seq, BRANCH_WIDTH), BF16),
        scratch_shapes=[pltpu.VMEM((batch, tile + SUBLANES, GDN_QKV), F32),
                        pltpu.VMEM((batch, SUBLANES, GDN_QKV), F32),
                        pltpu.VMEM((batch, tile, GDN_QKV), F32),
                        pltpu.VMEM((batch, tile, LANES), F32),
                        pltpu.VMEM((batch, tile, LANES), F32),
                        pltpu.VMEM((batch * GDN_HEADS, GDN_HEAD_DIM, GDN_HEAD_DIM), F32)],
        compiler_params=_cparams(("arbitrary",)),
        name="gdn_branch",
    )(proj.reshape(batch, seq, GDN_WIDTH), conv_w, alog_p, dtb_p, norm_g.reshape(1, GDN_HEAD_DIM))
    return out.reshape(batch * seq, BRANCH_WIDTH)


RWKV_R_OFF = 0
RWKV_K_OFF = BRANCH_WIDTH
RWKV_V_OFF = 2 * BRANCH_WIDTH
RWKV_WD_OFF = 3 * BRANCH_WIDTH
RWKV_AD_OFF = RWKV_WD_OFF + LANES
RWKV_GD_OFF = RWKV_AD_OFF + LANES
RWKV_WIDTH = RWKV_GD_OFF + LANES
RWKV_GROUP_HEADS = 4
RWKV_GROUP_W = RWKV_GROUP_HEADS * RWKV_HEAD_DIM


def _rwkv_kernel(p_ref, mu_ref, w0_ref, wup_ref, a0_ref, aup_ref, gup_ref, kk_ref, ka_ref,
                 rk_ref, lng_ref, lnb_ref, o_ref, xbuf, carry, hsbuf, state, *, tile, nb):
    s = pl.program_id(0)

    @pl.when(s == 0)
    def _():
        state[...] = jnp.zeros_like(state)
        carry[...] = jnp.zeros_like(carry)

    for bi in range(nb):
        xbuf[bi, 0:SUBLANES, :] = carry[bi]
        xbuf[bi, SUBLANES:SUBLANES + tile, :] = p_ref[bi]
        hr = xbuf[bi, SUBLANES:SUBLANES + tile, :]
        prev = xbuf[bi, SUBLANES - 1:SUBLANES - 1 + tile, :]
        carry[bi] = xbuf[bi, tile:tile + SUBLANES, :]
        hsbuf[bi] = hr + (prev - hr) * mu_ref[...]

    tri = _tri_inclusive_bf16(CHUNK)
    rb = _RowBlock(RWKV_GROUP_HEADS)
    gw = RWKV_GROUP_W
    seg = rb.bd_mask(RWKV_HEAD_DIM)
    segf = seg.astype(F32)
    inv_hd = 1.0 / RWKV_HEAD_DIM
    ng = BRANCH_WIDTH // gw
    items = [(bi, gi) for bi in range(nb) for gi in range(ng)]
    sl = [slice(gi * gw, (gi + 1) * gw) for _, gi in items]
    bidx = [bi for bi, _ in items]

    def chunk_body(ci, _):
        r0 = pl.multiple_of(ci * CHUNK, CHUNK)
        rows = pl.ds(r0, CHUNK)
        hs = [hsbuf[bi, rows, :] for bi in range(nb)]
        rv = [x[:, RWKV_R_OFF:RWKV_R_OFF + BRANCH_WIDTH] for x in hs]
        kv = [x[:, RWKV_K_OFF:RWKV_K_OFF + BRANCH_WIDTH] for x in hs]
        vv = [x[:, RWKV_V_OFF:RWKV_V_OFF + BRANCH_WIDTH] for x in hs]
        lora_in = jnp.concatenate([x[:, RWKV_WD_OFF:RWKV_WIDTH] for x in hs], axis=0)
        dw = _dot(_bf(jnp.tanh(lora_in[:, 0:LANES])), wup_ref[...])
        da = _dot(_bf(lora_in[:, LANES:2 * LANES]), aup_ref[...])
        gate_all = _dot(_bf(_sigmoid(lora_in[:, 2 * LANES:3 * LANES])), gup_ref[...])
        rowsl = [slice(bi * CHUNK, (bi + 1) * CHUNK) for bi in range(nb)]
        log_w = [-RWKV_DECAY_SCALE * _sigmoid(w0_ref[...] + dw[r_]) for r_ in rowsl]
        a_lr = [_sigmoid(a0_ref[...] + da[r_]) for r_ in rowsl]
        gate = [gate_all[r_] for r_ in rowsl]
        kkr = [x * kk_ref[...] for x in kv]
        kmod = [x * (1.0 + (a - 1.0) * ka_ref[...]) for x, a in zip(kv, a_lr)]
        rk = [x * y * rk_ref[...] for x, y in zip(rv, kmod)]
        g = [_dot_sel_l(tri, x) for x in log_w]
        egn = [jnp.exp(-x) for x in g]
        glast = [x[CHUNK - 1:CHUNK, :] for x in g]
        etail = [jnp.exp(gl - x) for gl, x in zip(glast, g)]
        dec = [jnp.exp(x) for x in glast]
        r_t = [x * jnp.exp(y) for x, y in zip(rv, g)]
        g_prev = [x - y for x, y in zip(g, log_w)]
        kkss = _stacked_sel_r([kkr[bi][:, s_] * kkr[bi][:, s_] for bi, s_ in zip(bidx, sl)], seg)
        bon = _stacked_sel_r([rk[bi][:, s_] for bi, s_ in zip(bidx, sl)], seg)
        kk = [kkr[bi][:, s_] * lax.rsqrt(x + 1e-6) for bi, s_, x in zip(bidx, sl, kkss)]
        b = [x * a_lr[bi][:, s_] for bi, s_, x in zip(bidx, sl, kk)]
        a_t = [-x * jnp.exp(g_prev[bi][:, s_]) for bi, s_, x in zip(bidx, sl, kk)]
        vs = [vv[bi][:, s_] for bi, s_ in zip(bidx, sl)]
        km = [kmod[bi][:, s_] for bi, s_ in zip(bidx, sl)]
        lhs = [_bf(jnp.concatenate([x, r_t[bi][:, s_]], axis=0)) for bi, s_, x in zip(bidx, sl, a_t)]
        rbt = [rb.rhs(x * egn[bi][:, s_], False)[0] for bi, s_, x in zip(bidx, sl, b)]
        rkt = [rb.rhs(x * egn[bi][:, s_], False)[0] for bi, s_, x in zip(bidx, sl, km)]
        pb = [_dot_nt(x, y) for x, y in zip(lhs, rbt)]
        pk = [_dot_nt(x, y) for x, y in zip(lhs, rkt)]
        minv = _rb_unit_lower_inverse(rb, [jnp.where(rb.strict, -x[:CHUNK], 0.0) for x in pb], INV_HI)
        rv_rhs = [rb.rhs(x, False) for x in vs]
        a_k = [jnp.concatenate([jnp.where(rb.strict, x[:CHUNK], 0.0), jnp.where(rb.causal, x[CHUNK:], 0.0)],
                               axis=0) for x in pk]
        a_kv = [_rb_mul(x, r) for x, r in zip(a_k, rv_rhs)]
        akv = [x[:CHUNK] for x in a_kv]
        arkv = [x[CHUNK:] for x in a_kv]
        a_rb = [jnp.where(rb.causal, x[CHUNK:], 0.0) for x in pb]
        sts = [state[i] for i in range(len(items))]
        init = [_dot_nt(x, _bf(st)) for x, st in zip(lhs, sts)]
        u = [_rb_mul(m, rb.rhs(x[:CHUNK] + y, INV_HI)) for m, x, y in zip(minv, init, akv)]
        y = [x[CHUNK:] + _rb_mul(p, rb.rhs(q, False)) + w for x, p, q, w in zip(init, a_rb, u, arkv)]
        tails = [_bf(jnp.concatenate([x * etail[bi][:, s_], k_ * etail[bi][:, s_]], axis=0))
                 for bi, s_, x, k_ in zip(bidx, sl, b, km)]
        upd = [_dot_tn(_bf(jnp.concatenate([x, v_], axis=0)), t) for x, v_, t in zip(u, vs, tails)]
        for i, (bi, s_) in enumerate(zip(bidx, sl)):
            state[i] = sts[i] * dec[bi][:, s_] + upd[i] * segf
        mean = [x * inv_hd for x in _stacked_sel_r(y, seg)]
        yc = [x - m for x, m in zip(y, mean)]
        var = [x * inv_hd for x in _stacked_sel_r([x * x for x in yc], seg)]
        for i, (bi, s_) in enumerate(zip(bidx, sl)):
            yn = yc[i] * lax.rsqrt(var[i] + RWKV_LN_EPS) * lng_ref[:, s_] + lnb_ref[:, s_]
            o_ref[bi, rows, s_] = ((yn + bon[i] * vs[i]) * gate[bi][:, s_]).astype(o_ref.dtype)
        return 0

    lax.fori_loop(0, tile // CHUNK, chunk_body, 0)


def _pad_rows(w, rows):
    return jnp.zeros((rows,) + w.shape[1:], w.dtype).at[:w.shape[0]].set(w)


def _rwkv_mu_layout(mu):
    z = jnp.zeros((LANES - RWKV_DECAY_LORA,), mu.dtype)
    o = 3 * BRANCH_WIDTH
    return jnp.concatenate([mu[:o], mu[o:o + RWKV_DECAY_LORA], z,
                            mu[o + RWKV_DECAY_LORA:o + RWKV_DECAY_LORA + RWKV_ICLR_LORA], z,
                            mu[o + RWKV_DECAY_LORA + RWKV_ICLR_LORA:]])


def _rwkv_branch(proj, mu, w0, w_up, a0, a_up, g_up, k_k, k_a, r_k, ln_g, ln_b, *, batch, seq, tile):
    bw = BRANCH_WIDTH
    row = lambda x: x.reshape(1, -1).astype(F32)
    const = lambda s: (0, 0)
    vec = pl.BlockSpec((1, bw), const)
    n_state = batch * (bw // RWKV_GROUP_W)
    out = pl.pallas_call(
        functools.partial(_rwkv_kernel, tile=tile, nb=batch),
        grid=(seq // tile,),
        in_specs=[pl.BlockSpec((batch, tile, RWKV_WIDTH), lambda s: (0, s, 0)),
                  pl.BlockSpec((1, RWKV_WIDTH), const),
                  vec, pl.BlockSpec((LANES, bw), const),
                  vec, pl.BlockSpec((LANES, bw), const),
                  pl.BlockSpec((LANES, bw), const),
                  vec, vec, vec, vec, vec],
        out_specs=pl.BlockSpec((batch, tile, bw), lambda s: (0, s, 0)),
        out_shape=jax.ShapeDtypeStruct((batch, seq, bw), BF16),
        scratch_shapes=[pltpu.VMEM((batch, tile + SUBLANES, RWKV_WIDTH), F32),
                        pltpu.VMEM((batch, SUBLANES, RWKV_WIDTH), F32),
                        pltpu.VMEM((batch, tile, RWKV_WIDTH), F32),
                        pltpu.VMEM((n_state, RWKV_GROUP_W, RWKV_GROUP_W), F32)],
        compiler_params=_cparams(("arbitrary",)),
        name="rwkv_branch",
    )(proj.reshape(batch, seq, RWKV_WIDTH), row(_rwkv_mu_layout(mu)), row(w0), _bf(_pad_rows(w_up, LANES)),
      row(a0), _bf(_pad_rows(a_up, LANES)), _bf(g_up), row(k_k), row(k_a), row(r_k), row(ln_g), row(ln_b))
    return out.reshape(batch * seq, bw)


def _pool_kernel(u_ref, pw_ref, ps_ref, o_ref, xbuf, sbuf, carry, *, tile):
    s = pl.program_id(1)

    @pl.when(s == 0)
    def _():
        carry[...] = jnp.zeros_like(carry)

    hist = POOL_MAX_WINDOW
    n = tile + hist
    xbuf[0:hist, :] = carry[...]
    xbuf[hist:n, :] = u_ref[...]
    carry[...] = xbuf[tile:n, :]
    pos = s * tile + _iota2(tile, LANES, 0)
    gw = LANES
    for gi, win in enumerate(POOL_WINDOWS):
        sl = slice(gi * gw, (gi + 1) * gw)
        sbuf[...] = xbuf[:, sl]
        span = 1
        while span < win:
            sbuf[span:n, :] = sbuf[span:n, :] + sbuf[0:n - span, :]
            span *= 2
        x = xbuf[hist:n, sl]
        count = jnp.minimum(pos + 1, win).astype(F32)
        pooled = sbuf[hist:n, :] / count - x
        y = _dot(_bf(pooled), pw_ref[gi])
        o_ref[:, sl] = (y * ps_ref[:, sl]).astype(o_ref.dtype)


def _pool_branch(u, pool_w, pool_scale, *, batch, seq, tile):
    ns = seq // tile
    bw = BRANCH_WIDTH
    return pl.pallas_call(
        functools.partial(_pool_kernel, tile=tile),
        grid=(batch, ns),
        in_specs=[pl.BlockSpec((tile, bw), lambda b, s: (b * ns + s, 0)),
                  pl.BlockSpec((len(POOL_WINDOWS), LANES, LANES), lambda b, s: (0, 0, 0)),
                  pl.BlockSpec((1, bw), lambda b, s: (0, 0))],
        out_specs=pl.BlockSpec((tile, bw), lambda b, s: (b * ns + s, 0)),
        out_shape=jax.ShapeDtypeStruct((batch * seq, bw), BF16),
        scratch_shapes=[pltpu.VMEM((tile + POOL_MAX_WINDOW, bw), F32),
                        pltpu.VMEM((tile + POOL_MAX_WINDOW, LANES), F32),
                        pltpu.VMEM((POOL_MAX_WINDOW, bw), F32)],
        compiler_params=_cparams(("parallel", "arbitrary")),
        name="pool_branch",
    )(u, _bf(pool_w), pool_scale.reshape(1, bw))


GLA_QW = GLA_HEADS * GLA_KEY_DIM
GLA_Q_OFF = 0
GLA_K_OFF = GLA_QW
GLA_V_OFF = 2 * GLA_QW
GLA_G_OFF = GLA_V_OFF + BRANCH_WIDTH
GLA_F_OFF = GLA_G_OFF + BRANCH_WIDTH
GLA_WIDTH = GLA_F_OFF + LANES


def _gla_kernel(p_ref, fup_ref, fb_ref, ng_ref, o_ref, state, *, tile, nb):
    s = pl.program_id(0)

    @pl.when(s == 0)
    def _():
        state[...] = jnp.zeros_like(state)

    tri = _tri_inclusive_bf16(CHUNK)
    rb = _RowBlock(GLA_HEADS)
    vh = _iota2(BRANCH_WIDTH, GLA_QW, 0) >> (GLA_VAL_DIM.bit_length() - 1)
    kh = _iota2(BRANCH_WIDTH, GLA_QW, 1) >> (GLA_KEY_DIM.bit_length() - 1)
    same_head = (vh == kh).astype(F32)
    ng = ng_ref[...]
    dv = GLA_VAL_DIM
    bs = range(nb)

    def chunk_body(ci, _):
        r0 = pl.multiple_of(ci * CHUNK, CHUNK)
        rows = pl.ds(r0, CHUNK)
        fl = jnp.concatenate([p_ref[bi, rows, GLA_F_OFF:GLA_F_OFF + LANES] for bi in bs], axis=0)
        logits = _dot(_bf(fl), fup_ref[...]) + fb_ref[...]
        log_f = -_softplus(-logits) / GLA_GATE_NORM
        gc = [_dot_sel_l(tri, log_f[bi * CHUNK:(bi + 1) * CHUNK]) for bi in bs]
        q = [p_ref[bi, rows, GLA_Q_OFF:GLA_Q_OFF + GLA_QW] * (GLA_KEY_DIM ** -0.5) for bi in bs]
        k = [p_ref[bi, rows, GLA_K_OFF:GLA_K_OFF + GLA_QW] for bi in bs]
        v = [p_ref[bi, rows, GLA_V_OFF:GLA_V_OFF + BRANCH_WIDTH] for bi in bs]
        q_dec = [_bf(x * jnp.exp(g)) for x, g in zip(q, gc)]
        k_dec = [x * jnp.exp(-g) for x, g in zip(k, gc)]
        glast = [g[CHUNK - 1:CHUNK, :] for g in gc]
        k_tail = [_bf(x * jnp.exp(gl - g)) for x, gl, g in zip(k, glast, gc)]
        attn = [_dot_nt(x, rb.rhs(y, False)[0]) for x, y in zip(q_dec, k_dec)]
        intra = [_rb_mul(jnp.where(rb.causal, a, 0.0), rb.rhs(x, False)) for a, x in zip(attn, v)]
        sts = [state[bi] for bi in bs]
        inter = [_dot_nt(x, _bf(st)) for x, st in zip(q_dec, sts)]
        upd = [_dot_tn(_bf(x), y) for x, y in zip(v, k_tail)]
        for bi in bs:
            state[bi] = sts[bi] * jnp.exp(glast[bi]) + upd[bi] * same_head
            o = intra[bi] + inter[bi]
            for h in range(GLA_HEADS):
                oh = o[:, h * dv:(h + 1) * dv]
                gate = p_ref[bi, rows, GLA_G_OFF + h * dv:GLA_G_OFF + (h + 1) * dv]
                on = (oh * lax.rsqrt(jnp.mean(oh * oh, axis=-1, keepdims=True) + NORM_EPS)) * ng
                o_ref[bi, rows, h * dv:(h + 1) * dv] = (on * _silu(gate)).astype(o_ref.dtype)
        return 0

    lax.fori_loop(0, tile // CHUNK, chunk_body, 0)


def _gla_branch(proj, f_up, f_bias, norm_g, *, batch, seq, tile):
    const = lambda s: (0, 0)
    out = pl.pallas_call(
        functools.partial(_gla_kernel, tile=tile, nb=batch),
        grid=(seq // tile,),
        in_specs=[pl.BlockSpec((batch, tile, GLA_WIDTH), lambda s: (0, s, 0)),
                  pl.BlockSpec((LANES, GLA_QW), const),
                  pl.BlockSpec((1, GLA_QW), const),
                  pl.BlockSpec((1, GLA_VAL_DIM), const)],
        out_specs=pl.BlockSpec((batch, tile, BRANCH_WIDTH), lambda s: (0, s, 0)),
        out_shape=jax.ShapeDtypeStruct((batch, seq, BRANCH_WIDTH), BF16),
        scratch_shapes=[pltpu.VMEM((batch, BRANCH_WIDTH, GLA_QW), F32)],
        compiler_params=_cparams(("arbitrary",)),
        name="gla_branch",
    )(proj.reshape(batch, seq, GLA_WIDTH), _bf(_pad_rows(f_up, LANES)), f_bias.reshape(1, GLA_QW),
      norm_g.reshape(1, GLA_VAL_DIM))
    return out.reshape(batch * seq, BRANCH_WIDTH)


def _merge_kernel(hn_ref, y0, y1, y2, y3, bp_ref, wg0, wg1, wg2, wg3, gb_ref, o_ref):
    ys = (y0, y1, y2, y3)
    wgs = (wg0, wg1, wg2, wg3)
    hn = hn_ref[...]
    acc = None
    for i in range(N_BRANCH):
        logit = _dot(hn, wgs[i][...]) + gb_ref[i]
        gate = 0.5 * (jnp.tanh(0.5 * logit) + 1.0)
        term = gate * _dot(ys[i][...], bp_ref[i])
        acc = term if acc is None else acc + term
    o_ref[...] = acc.astype(o_ref.dtype)


def _merge(hn, ys, branch_proj, w_gate, gate_bias, *, tm, tn):
    t, dm = hn.shape
    d = branch_proj.shape[-1]
    nj = d // tn
    y_spec = pl.BlockSpec((tm, BRANCH_WIDTH), lambda j, i: (i, 0))
    wg_specs = [pl.BlockSpec((dm, tn), functools.partial(lambda j, i, b: (0, b * nj + j), b=b))
                for b in range(N_BRANCH)]
    return pl.pallas_call(
        _merge_kernel,
        grid=(nj, t // tm),
        in_specs=[pl.BlockSpec((tm, dm), lambda j, i: (i, 0))]
        + [y_spec] * N_BRANCH
        + [pl.BlockSpec((N_BRANCH, BRANCH_WIDTH, tn), lambda j, i: (0, 0, j))]
        + wg_specs
        + [pl.BlockSpec((N_BRANCH, 1, tn), lambda j, i: (0, 0, j))],
        out_specs=pl.BlockSpec((tm, tn), lambda j, i: (i, j)),
        out_shape=jax.ShapeDtypeStruct((t, d), BF16),
        compiler_params=_cparams(("parallel", "parallel")),
        name="merge",
    )(hn, *ys, _bf(branch_proj), w_gate, w_gate, w_gate, w_gate, gate_bias.reshape(N_BRANCH, 1, d))


def _rms_rows(x, g):
    return (x * lax.rsqrt(jnp.mean(x * x, axis=-1, keepdims=True) + NORM_EPS)) * g


def _ffn_kernel(h_ref, g_ref, w1_ref, w3_ref, w2_ref, o_ref, xn_ref, acc_ref, *, nf):
    f = pl.program_id(1)

    @pl.when(f == 0)
    def _():
        acc_ref[...] = jnp.zeros_like(acc_ref)
        xn_ref[...] = _bf(_rms_rows(h_ref[...], g_ref[...]))

    x = xn_ref[...]
    mid = _bf(_silu(_dot(x, w1_ref[...])) * _dot(x, w3_ref[...]))
    acc_ref[...] += _dot(mid, w2_ref[...])

    @pl.when(f == nf - 1)
    def _():
        o_ref[...] = h_ref[...] + acc_ref[...]


def _ffn(h, g, w1, w3, w2, *, tm, tf):
    t, d = h.shape
    ff = w1.shape[1]
    nf = ff // tf
    return pl.pallas_call(
        functools.partial(_ffn_kernel, nf=nf),
        grid=(t // tm, nf),
        in_specs=[pl.BlockSpec((tm, d), lambda i, f: (i, 0)),
                  pl.BlockSpec((1, d), lambda i, f: (0, 0)),
                  pl.BlockSpec((d, tf), lambda i, f: (0, f)),
                  pl.BlockSpec((d, tf), lambda i, f: (0, f)),
                  pl.BlockSpec((tf, d), lambda i, f: (f, 0))],
        out_specs=pl.BlockSpec((tm, d), lambda i, f: (i, 0)),
        out_shape=jax.ShapeDtypeStruct((t, d), F32),
        scratch_shapes=[pltpu.VMEM((tm, d), BF16), pltpu.VMEM((tm, d), F32)],
        compiler_params=_cparams(("parallel", "arbitrary")),
        name="ffn",
    )(h, g.reshape(1, d), w1, w3, w2)


ROUTE_E0, ROUTE_E1, ROUTE_RANK0, ROUTE_RANK1 = 0, 1, 2, 3
ROUTE_W0, ROUTE_W1 = 0, 1


def _router_kernel(h_ref, g_ref, rw_ref, ri_ref, rf_ref, cnt_ref, run):
    i = pl.program_id(0)

    @pl.when(i == 0)
    def _():
        run[...] = jnp.zeros_like(run)

    tm = h_ref.shape[0]
    hn = _bf(_rms_rows(h_ref[...], g_ref[...]))
    logits = _dot(hn, rw_ref[...])
    lane = _iota2(tm, LANES, 1)
    neg = jnp.float32(-jnp.inf)
    lg = jnp.where(lane < N_EXPERTS, logits, neg)
    m1 = jnp.max(lg, axis=-1, keepdims=True)
    e0 = jnp.min(jnp.where(lg == m1, lane, LANES), axis=-1, keepdims=True)
    lg2 = jnp.where(lane == e0, neg, lg)
    m2 = jnp.max(lg2, axis=-1, keepdims=True)
    e1 = jnp.min(jnp.where(lg2 == m2, lane, LANES), axis=-1, keepdims=True)
    ex = jnp.exp(m2 - m1)
    den = 1.0 + ex
    w0 = 1.0 / den
    w1 = ex / den
    hit0 = lane == e0
    hit1 = lane == e1
    onehot = (hit0 | hit1).astype(F32)
    rr = _iota2(tm, tm, 0)
    cc = _iota2(tm, tm, 1)
    before = _dot((cc < rr).astype(BF16), _bf(onehot)) + run[...]
    rank0 = jnp.sum(jnp.where(hit0, before, 0.0), axis=-1, keepdims=True).astype(I32)
    rank1 = jnp.sum(jnp.where(hit1, before, 0.0), axis=-1, keepdims=True).astype(I32)
    run[...] += jnp.sum(onehot, axis=0, keepdims=True)
    cnt_ref[...] = run[...]
    ri_ref[...] = jnp.where(lane == ROUTE_E0, e0,
                            jnp.where(lane == ROUTE_E1, e1,
                                      jnp.where(lane == ROUTE_RANK0, rank0,
                                                jnp.where(lane == ROUTE_RANK1, rank1, 0))))
    rf_ref[...] = jnp.where(lane == ROUTE_W0, w0, jnp.where(lane == ROUTE_W1, w1, 0.0))


def _router(h, g, router_w, *, tm=512):
    t, d = h.shape
    rw = _bf(jnp.zeros((d, LANES), F32).at[:, :N_EXPERTS].set(router_w))
    return pl.pallas_call(
        _router_kernel,
        grid=(t // tm,),
        in_specs=[pl.BlockSpec((tm, d), lambda i: (i, 0)),
                  pl.BlockSpec((1, d), lambda i: (0, 0)),
                  pl.BlockSpec((d, LANES), lambda i: (0, 0))],
        out_specs=[pl.BlockSpec((tm, LANES), lambda i: (i, 0)),
                   pl.BlockSpec((tm, LANES), lambda i: (i, 0)),
                   pl.BlockSpec((1, LANES), lambda i: (0, 0))],
        out_shape=[jax.ShapeDtypeStruct((t, LANES), I32),
                   jax.ShapeDtypeStruct((t, LANES), F32),
                   jax.ShapeDtypeStruct((1, LANES), F32)],
        scratch_shapes=[pltpu.VMEM((1, LANES), F32)],
        compiler_params=_cparams(("arbitrary",)),
        name="moe_router",
    )(h, g.reshape(1, d), rw)


def _row_copy(src_ref, src_row, dst_ref, dst_row, sem):
    return pltpu.make_async_copy(src_ref.at[pl.ds(src_row, 1)], dst_ref.at[pl.ds(dst_row, 1)], sem)


DMA_ISSUE_UNROLL = 8


def _dispatch_kernel(d0_ref, d1_ref, zs_ref, zf_ref, x_ref, xb_ref, zbuf, sem, zsem):
    tm = x_ref.shape[0]
    step = pl.program_id(0)
    base = step * tm

    @pl.when(step == 0)
    def _():
        zbuf[...] = jnp.zeros_like(zbuf)

        def zero_copy(j):
            first = pl.multiple_of(zs_ref[j], MOE_BLOCK)
            return pltpu.make_async_copy(zbuf, xb_ref.at[pl.ds(first, MOE_BLOCK)], zsem)

        for j in range(2 * N_EXPERTS):
            @pl.when(zf_ref[j] > 0)
            def _():
                zero_copy(j).start()
        for j in range(2 * N_EXPERTS):
            @pl.when(zf_ref[j] > 0)
            def _():
                zero_copy(j).wait()

    def copies(rw):
        return (_row_copy(x_ref, rw, xb_ref, d0_ref[base + rw], sem),
                _row_copy(x_ref, rw, xb_ref, d1_ref[base + rw], sem))

    def start(rw, _):
        for cp in copies(rw):
            cp.start()
        return 0

    def wait(rw, _):
        for cp in copies(rw):
            cp.wait()
        return 0

    lax.fori_loop(0, tm, start, 0, unroll=DMA_ISSUE_UNROLL)
    lax.fori_loop(0, tm, wait, 0, unroll=DMA_ISSUE_UNROLL)


def _dispatch(x, dest0, dest1, zero_start, zero_flag, n_rows, *, tm=512):
    t, d = x.shape
    return pl.pallas_call(
        _dispatch_kernel,
        grid_spec=pltpu.PrefetchScalarGridSpec(
            num_scalar_prefetch=4,
            grid=(t // tm,),
            in_specs=[pl.BlockSpec((tm, d), lambda i, *_: (i, 0))],
            out_specs=pl.BlockSpec(memory_space=pl.ANY),
            scratch_shapes=[pltpu.VMEM((MOE_BLOCK, d), F32),
                            pltpu.SemaphoreType.DMA(()), pltpu.SemaphoreType.DMA(())]),
        out_shape=jax.ShapeDtypeStruct((n_rows, d), F32),
        compiler_params=_cparams(("arbitrary",)),
        name="moe_dispatch",
    )(dest0, dest1, zero_start, zero_flag, x)


MOE_UNIT_BLOCKS = 3


def _moe_ffn_kernel(ue_ref, ub_ref, uv_ref, zs_ref, zf_ref, *refs, nf):
    del ue_ref
    nu = MOE_UNIT_BLOCKS
    x_refs = refs[:nu]
    g_ref, w1_ref, w3_ref, w2_ref, yb_ref = refs[nu:nu + 5]
    xns = refs[nu + 5:2 * nu + 5]
    accs = refs[2 * nu + 5:3 * nu + 5]
    sem = refs[3 * nu + 5]
    u = pl.program_id(0)
    f = pl.program_id(1)
    slot = [u * nu + j for j in range(nu)]

    @pl.when((u == 0) & (f == 0))
    def _():
        zsrc = accs[nu - 1]
        zsrc[...] = jnp.zeros_like(zsrc)

        def zero_copy(j):
            first = pl.multiple_of(zs_ref[j], MOE_BLOCK)
            return pltpu.make_async_copy(zsrc, yb_ref.at[pl.ds(first, MOE_BLOCK)], sem)

        for j in range(N_EXPERTS):
            @pl.when(zf_ref[j] > 0)
            def _():
                zero_copy(j).start()
        for j in range(N_EXPERTS):
            @pl.when(zf_ref[j] > 0)
            def _():
                zero_copy(j).wait()

    @pl.when(f == 0)
    def _():
        for j in range(nu):
            @pl.when(uv_ref[slot[j]] > 0)
            def _():
                accs[j][...] = jnp.zeros_like(accs[j])
                xns[j][...] = _bf(_rms_rows(x_refs[j][...], g_ref[...]))

    for j in range(nu):
        @pl.when(uv_ref[slot[j]] > 0)
        def _():
            x = xns[j][...]
            mid = _bf(_silu(_dot(x, _bf(w1_ref[0]))) * _dot(x, _bf(w3_ref[0])))
            accs[j][...] += _dot(mid, _bf(w2_ref[0]))

    @pl.when(f == nf - 1)
    def _():
        def out_copy(j):
            first = pl.multiple_of(ub_ref[slot[j]] * MOE_BLOCK, MOE_BLOCK)
            return pltpu.make_async_copy(accs[j], yb_ref.at[pl.ds(first, MOE_BLOCK)], sem)

        for j in range(nu):
            @pl.when(uv_ref[slot[j]] > 0)
            def _():
                out_copy(j).start()
        for j in range(nu):
            @pl.when(uv_ref[slot[j]] > 0)
            def _():
                out_copy(j).wait()


def _moe_ffn(xb, g, w1, w3, w2, unit_e, unit_b, unit_v, spare_start, spare_flag, *, tf):
    n_rows, d = xb.shape
    ff = w1.shape[-1]
    nf = ff // tf
    nu = MOE_UNIT_BLOCKS
    n_units = unit_e.shape[0]
    wcol = lambda u, f, ue, ub, uv, *_: (ue[u], 0, f * uv[u * nu])
    wrow = lambda u, f, ue, ub, uv, *_: (ue[u], f * uv[u * nu], 0)
    x_specs = [pl.BlockSpec((MOE_BLOCK, d), functools.partial(lambda u, f, ue, ub, *_, j: (ub[u * nu + j], 0), j=j),
                            pipeline_mode=pl.Buffered(1)) for j in range(nu)]
    return pl.pallas_call(
        functools.partial(_moe_ffn_kernel, nf=nf),
        grid_spec=pltpu.PrefetchScalarGridSpec(
            num_scalar_prefetch=5,
            grid=(n_units, nf),
            in_specs=x_specs + [pl.BlockSpec((1, d), lambda u, f, *_: (0, 0)),
                                pl.BlockSpec((1, d, tf), wcol),
                                pl.BlockSpec((1, d, tf), wcol),
                                pl.BlockSpec((1, tf, d), wrow)],
            out_specs=pl.BlockSpec(memory_space=pl.ANY),
            scratch_shapes=[pltpu.VMEM((MOE_BLOCK, d), BF16)] * nu + [pltpu.VMEM((MOE_BLOCK, d), F32)] * nu
            + [pltpu.SemaphoreType.DMA(())]),
        out_shape=jax.ShapeDtypeStruct((n_rows, d), F32),
        compiler_params=_cparams(("arbitrary", "arbitrary")),
        name="moe_experts",
    )(unit_e, unit_b, unit_v, spare_start, spare_flag, *([xb] * nu), g.reshape(1, d), w1, w3, w2)


def _combine_kernel(d0_ref, d1_ref, yb_ref, h_ref, rf_ref, g_ref, o_ref, buf0, buf1, sem, *, final_norm):
    tm = h_ref.shape[0]
    base = pl.program_id(0) * tm

    def copies(rw):
        return (_row_copy(yb_ref, d0_ref[base + rw], buf0, rw, sem),
                _row_copy(yb_ref, d1_ref[base + rw], buf1, rw, sem))

    def start(rw, _):
        for cp in copies(rw):
            cp.start()
        return 0

    def wait(rw, _):
        for cp in copies(rw):
            cp.wait()
        return 0

    lax.fori_loop(0, tm, start, 0, unroll=DMA_ISSUE_UNROLL)
    lax.fori_loop(0, tm, wait, 0, unroll=DMA_ISSUE_UNROLL)
    w0 = rf_ref[:, ROUTE_W0:ROUTE_W0 + 1]
    w1 = rf_ref[:, ROUTE_W1:ROUTE_W1 + 1]
    out = h_ref[...] + (buf0[...] * w0 + buf1[...] * w1)
    o_ref[...] = _rms_rows(out, g_ref[...]) if final_norm else out


def _combine(yb, h, route_f, dest0, dest1, final_g, *, tm=256):
    t, d = h.shape
    g = jnp.ones((1, d), F32) if final_g is None else final_g.reshape(1, d)
    return pl.pallas_call(
        functools.partial(_combine_kernel, final_norm=final_g is not None),
        grid_spec=pltpu.PrefetchScalarGridSpec(
            num_scalar_prefetch=2,
            grid=(t // tm,),
            in_specs=[pl.BlockSpec(memory_space=pl.ANY),
                      pl.BlockSpec((tm, d), lambda i, d0, d1: (i, 0)),
                      pl.BlockSpec((tm, LANES), lambda i, d0, d1: (i, 0)),
                      pl.BlockSpec((1, d), lambda i, d0, d1: (0, 0))],
            out_specs=pl.BlockSpec((tm, d), lambda i, d0, d1: (i, 0)),
            scratch_shapes=[pltpu.VMEM((tm, d), F32), pltpu.VMEM((tm, d), F32),
                            pltpu.SemaphoreType.DMA(())]),
        out_shape=jax.ShapeDtypeStruct((t, d), F32),
        compiler_params=_cparams(("arbitrary",)),
        name="moe_combine",
    )(dest0, dest1, yb, h, route_f, g)


def _moe(h, g, router_w, w1, w3, w2, final_g=None):
    t, d = h.shape
    route_i, route_f, counts_f = _router(h, g, router_w)
    counts = counts_f[0, :N_EXPERTS].astype(I32)
    padded = (counts + MOE_BLOCK - 1) // MOE_BLOCK * MOE_BLOCK
    pad_end = jnp.cumsum(padded)
    pad_start = pad_end - padded
    n_rows = (-(-(t * 2) // MOE_BLOCK) + N_EXPERTS) * MOE_BLOCK
    n_blocks = n_rows // MOE_BLOCK
    nu = MOE_UNIT_BLOCKS
    blocks_e = padded // MOE_BLOCK
    units_e = (blocks_e + nu - 1) // nu
    unit_end = jnp.cumsum(units_e)
    n_units = -(-n_blocks // nu) + N_EXPERTS
    uidx = jnp.arange(n_units, dtype=I32)
    unit_e = jnp.minimum(jnp.sum((uidx[:, None] >= unit_end[None, :]).astype(I32), axis=1), N_EXPERTS - 1)
    within = uidx - (unit_end - units_e)[unit_e]
    left = jnp.where(uidx < unit_end[N_EXPERTS - 1], blocks_e[unit_e] - nu * within, 0)
    first_blk = pad_start[unit_e] // MOE_BLOCK + nu * within
    slot_j = jnp.arange(nu, dtype=I32)[None, :]
    unit_v = (slot_j < left[:, None]).astype(I32)
    unit_b = jnp.where(unit_v > 0, first_blk[:, None] + slot_j, jnp.where(left > 0, first_blk, 0)[:, None])
    unit_v = unit_v.reshape(-1)
    unit_b = unit_b.reshape(-1).astype(I32)
    dest0 = pad_start[route_i[:, ROUTE_E0]] + route_i[:, ROUTE_RANK0]
    dest1 = pad_start[route_i[:, ROUTE_E1]] + route_i[:, ROUTE_RANK1]
    spare = pad_end[N_EXPERTS - 1] // MOE_BLOCK + jnp.arange(N_EXPERTS, dtype=I32)
    zero_start = jnp.concatenate([pad_end - MOE_BLOCK, jnp.minimum(spare, n_blocks - 1) * MOE_BLOCK])
    zero_flag = jnp.concatenate([padded > 0, spare < n_blocks]).astype(I32)
    xb = _dispatch(h, dest0, dest1, zero_start.astype(I32), zero_flag, n_rows)
    yb = _moe_ffn(xb, g, w1, w3, w2, unit_e, unit_b, unit_v,
                  zero_start[N_EXPERTS:].astype(I32), zero_flag[N_EXPERTS:], tf=256)
    return _combine(yb, h, route_f, dest0, dest1, final_g)


def _pad_cols(w, width):
    return jnp.concatenate([w, jnp.zeros((w.shape[0], width - w.shape[1]), w.dtype)], axis=1)


def _split_w_in(w_in):
    bw = BRANCH_WIDTH
    d = w_in.shape[0]
    widths = (3 * bw, bw, GDN_HEADS, GDN_HEADS, 3 * bw + RWKV_DECAY_LORA + RWKV_ICLR_LORA + RWKV_GATE_LORA,
              bw, GLA_QW, GLA_QW, bw, bw, GLA_GATE_RANK, N_BRANCH * d)
    offs = [0]
    for w in widths:
        offs.append(offs[-1] + w)
    col = lambda i: w_in[:, offs[i]:offs[i + 1]]
    gdn = jnp.concatenate([col(0), col(1), _pad_cols(jnp.concatenate([col(2), col(3)], axis=1), LANES)], axis=1)
    rw = col(4)
    o = 3 * bw
    rwkv = jnp.concatenate([rw[:, :o],
                            _pad_cols(rw[:, o:o + RWKV_DECAY_LORA], LANES),
                            _pad_cols(rw[:, o + RWKV_DECAY_LORA:o + RWKV_DECAY_LORA + RWKV_ICLR_LORA], LANES),
                            rw[:, o + RWKV_DECAY_LORA + RWKV_ICLR_LORA:]], axis=1)
    pool = col(5)
    gla = jnp.concatenate([col(6), col(7), col(8), col(9), _pad_cols(col(10), LANES)], axis=1)
    gate = col(11)
    return tuple(_bf(w) for w in (gdn, rwkv, pool, gla, gate))


def _mixer(hn, h, layer, batch, seq, w_in, gdn_conv_w, gdn_a_log, gdn_dt_bias, gdn_norm_g,
           rwkv_mu, rwkv_w0, rwkv_w_up, rwkv_a0, rwkv_a_up, rwkv_g_up, rwkv_k_k, rwkv_k_a,
           rwkv_r_k, rwkv_ln_g, rwkv_ln_b, pool_w, pool_scale, gla_f_up, gla_f_bias, gla_norm_g,
           gate_bias, branch_proj, w_out):
    l = layer
    w_gdn, w_rwkv, w_pool, w_gla, w_gate = _split_w_in(w_in[l])
    tm = 512
    p_gdn = _matmul(hn, w_gdn, tm=tm, tn=GDN_WIDTH, name="proj_gdn")
    p_rwkv = _matmul(hn, w_rwkv, tm=tm, tn=RWKV_WIDTH, name="proj_rwkv")
    p_pool = _matmul(hn, w_pool, tm=tm, tn=BRANCH_WIDTH, name="proj_pool")
    p_gla = _matmul(hn, w_gla, tm=tm, tn=GLA_WIDTH, name="proj_gla")
    seq_tile = 256
    kw = dict(batch=batch, seq=seq, tile=seq_tile)
    y_gdn = _gdn_branch(p_gdn, gdn_conv_w[l], gdn_a_log[l], gdn_dt_bias[l], gdn_norm_g[l], **kw)
    y_rwkv = _rwkv_branch(p_rwkv, rwkv_mu[l], rwkv_w0[l], rwkv_w_up[l], rwkv_a0[l], rwkv_a_up[l],
                          rwkv_g_up[l], rwkv_k_k[l], rwkv_k_a[l], rwkv_r_k[l].reshape(-1),
                          rwkv_ln_g[l], rwkv_ln_b[l], **kw)
    y_pool = _pool_branch(p_pool, pool_w[l], pool_scale[l], **kw)
    y_gla = _gla_branch(p_gla, gla_f_up[l], gla_f_bias[l], gla_norm_g[l], **kw)
    mixed = _merge(hn, (y_gdn, y_rwkv, y_pool, y_gla), branch_proj[l], w_gate, gate_bias[l], tm=512, tn=512)
    return _matmul(mixed, _bf(w_out[l]), tm=512, tn=1024, residual=h, name="out_proj")


def kernel(x, norm1_g, w_in, gdn_conv_w, gdn_a_log, gdn_dt_bias, gdn_norm_g, rwkv_mu, rwkv_w0, rwkv_w_up, rwkv_a0, rwkv_a_up, rwkv_g_up, rwkv_k_k, rwkv_k_a, rwkv_r_k, rwkv_ln_g, rwkv_ln_b, pool_w, pool_scale, gla_f_up, gla_f_bias, gla_norm_g, gate_bias, branch_proj, w_out, norm2_g, ffn_w1, ffn_w3, ffn_w2, moe_router, moe_w1, moe_w3, moe_w2, final_norm_g):
    batch, seq, d = x.shape
    depth = norm1_g.shape[0]
    h = x.reshape(batch * seq, d)
    for layer in range(depth):
        hn = _rmsnorm(h, norm1_g[layer], BF16)
        h = _mixer(hn, h, layer, batch, seq, w_in, gdn_conv_w, gdn_a_log, gdn_dt_bias, gdn_norm_g,
                   rwkv_mu, rwkv_w0, rwkv_w_up, rwkv_a0, rwkv_a_up, rwkv_g_up, rwkv_k_k, rwkv_k_a,
                   rwkv_r_k, rwkv_ln_g, rwkv_ln_b, pool_w, pool_scale, gla_f_up, gla_f_bias,
                   gla_norm_g, gate_bias, branch_proj, w_out)
        i = layer // 2
        last = layer == depth - 1
        if layer % 2 == 0:
            h = _ffn(h, norm2_g[layer], _bf(ffn_w1[i]), _bf(ffn_w3[i]), _bf(ffn_w2[i]), tm=512, tf=512)
            if last:
                h = _rmsnorm(h, final_norm_g, F32)
        else:
            h = _moe(h, norm2_g[layer], moe_router[i], moe_w1[i], moe_w3[i], moe_w2[i],
                     final_g=final_norm_g if last else None)
    return h.reshape(batch, seq, d)
```

```python
import functools

import jax
import jax.numpy as jnp
from jax import lax
from jax.experimental import pallas as pl
from jax.experimental.pallas import tpu as pltpu

F32 = jnp.float32
BF16 = jnp.bfloat16
I32 = jnp.int32

NORM_EPS = 1e-6
CHUNK = 64
N_BRANCH = 4
BRANCH_WIDTH = 512
GDN_HEADS = 4
GDN_HEAD_DIM = 128
GDN_CONV = 4
RWKV_HEAD_DIM = 64
RWKV_DECAY_LORA = 64
RWKV_ICLR_LORA = 64
RWKV_GATE_LORA = 128
RWKV_DECAY_SCALE = 0.606531
RWKV_LN_EPS = 64e-5
POOL_WINDOWS = (2, 4, 8, 16)
POOL_MAX_WINDOW = 16
GLA_HEADS = 4
GLA_KEY_DIM = 64
GLA_VAL_DIM = 128
GLA_GATE_RANK = 16
GLA_GATE_NORM = 16.0
N_EXPERTS = 8
MOE_BLOCK = 512

LANES = 128
SUBLANES = 8
VMEM_LIMIT_BYTES = 56 * 1024 * 1024

INV_BLOCK = 16
INV_HI = False
SEQ_CHUNKS_PER_STEP = 2


def _cparams(sem):
    return pltpu.CompilerParams(dimension_semantics=sem, vmem_limit_bytes=VMEM_LIMIT_BYTES)


def _dot(a, b):
    return jnp.dot(a, b, preferred_element_type=F32)


def _dot_nt(a, b):
    return lax.dot_general(a, b, (((1,), (1,)), ((), ())), preferred_element_type=F32)


def _dot_tn(a, b):
    return lax.dot_general(a, b, (((0,), (0,)), ((), ())), preferred_element_type=F32)


def _bf(x):
    return x.astype(BF16)


def _split_terms(x, terms):
    out = []
    rem = x
    for i in range(terms):
        xi = _bf(rem)
        out.append(xi)
        if i + 1 < terms:
            rem = rem - xi.astype(F32)
    return out


def _dot_sel_r(x, sel, terms=3):
    parts = [_dot(t, sel) for t in _split_terms(x, terms)]
    return functools.reduce(lambda a, b: a + b, parts)


def _dot_sel_l(sel, x, terms=3):
    parts = [_dot(sel, t) for t in _split_terms(x, terms)]
    return functools.reduce(lambda a, b: a + b, parts)


def _stacked_sel_r(xs, sel, terms=2):
    n = len(xs)
    rows = xs[0].shape[0]
    parts = [_split_terms(x, terms) for x in xs]
    out = _dot(jnp.concatenate([p[t] for t in range(terms) for p in parts], axis=0), sel)
    res = []
    for i in range(n):
        acc = out[i * rows:(i + 1) * rows]
        for t in range(1, terms):
            acc = acc + out[(t * n + i) * rows:(t * n + i + 1) * rows]
        res.append(acc)
    return res


def _dot_hi(a, b):
    ah = _bf(a)
    al = _bf(a - ah.astype(F32))
    bh = _bf(b)
    bl = _bf(b - bh.astype(F32))
    return _dot(ah, bh) + _dot(ah, bl) + _dot(al, bh)


def _sigmoid(x):
    return jax.nn.sigmoid(x)


def _silu(x):
    return x * jax.nn.sigmoid(x)


def _softplus(x):
    return jnp.maximum(x, 0.0) + jnp.log1p(jnp.exp(-jnp.abs(x)))


def _iota2(n, m, dim):
    return lax.broadcasted_iota(I32, (n, m), dim)


def _tri_inclusive_bf16(n):
    r = _iota2(n, n, 0)
    c = _iota2(n, n, 1)
    return (c <= r).astype(BF16)


class _RowBlock:
    def __init__(self, nblk):
        self.nblk = nblk
        n = nblk * CHUNK
        self._shift = CHUNK.bit_length() - 1
        row = _iota2(CHUNK, n, 0)
        colw = _iota2(CHUNK, n, 1) & (CHUNK - 1)
        self.eye = (row == colw).astype(F32)
        self.strict = colw < row
        self.causal = colw <= row
        ish = INV_BLOCK.bit_length() - 1
        self.same_diag_block = (row >> ish) == (colw >> ish)
        self._bd = {}

    def bd_mask(self, w):
        if w not in self._bd:
            n = self.nblk * CHUNK
            rb = _iota2(n, self.nblk * w, 0) >> self._shift
            cb = _iota2(n, self.nblk * w, 1) >> (w.bit_length() - 1)
            self._bd[w] = (rb == cb).astype(BF16)
        return self._bd[w]

    def rhs(self, y, hi):
        mask = self.bd_mask(y.shape[1] // self.nblk)
        yh = _bf(y)
        bdh = jnp.concatenate([yh] * self.nblk, axis=0) * mask
        if not hi:
            return (bdh, None)
        yl = _bf(y - yh.astype(F32))
        return (bdh, jnp.concatenate([yl] * self.nblk, axis=0) * mask)


def _rb_mul(x, rhs):
    bdh, bdl = rhs
    xh = _bf(x)
    out = _dot(xh, bdh)
    if bdl is not None:
        xl = _bf(x - xh.astype(F32))
        out = out + _dot(xh, bdl) + _dot(xl, bdh)
    return out


def _rb_unit_lower_inverse(rb, lows, hi):
    a = [jnp.where(rb.same_diag_block, -l, 0.0) for l in lows]
    off = [jnp.where(rb.same_diag_block, 0.0, l) for l in lows]
    t = [rb.eye + x for x in a]
    p = a
    rp = [rb.rhs(x, hi) for x in p]
    k = 2
    while k < INV_BLOCK:
        p = [_rb_mul(x, r) for x, r in zip(p, rp)]
        rp = [rb.rhs(x, hi) for x in p]
        t = [x + _rb_mul(x, r) for x, r in zip(t, rp)]
        k *= 2
    roff = [rb.rhs(x, hi) for x in off]
    nm = [_rb_mul(x, r) for x, r in zip(t, roff)]
    acc = [rb.eye - x for x in nm]
    rn = [rb.rhs(x, hi) for x in nm]
    pw = nm
    for j in range(2, CHUNK // INV_BLOCK):
        pw = [_rb_mul(x, r) for x, r in zip(pw, rn)]
        acc = [x + y if j % 2 == 0 else x - y for x, y in zip(acc, pw)]
    rt = [rb.rhs(x, hi) for x in t]
    return [_rb_mul(x, r) for x, r in zip(acc, rt)]


def _rmsnorm_kernel(x_ref, g_ref, o_ref):
    x = x_ref[...]
    ms = jnp.mean(x * x, axis=-1, keepdims=True)
    o_ref[...] = ((x * lax.rsqrt(ms + NORM_EPS)) * g_ref[...]).astype(o_ref.dtype)


def _rmsnorm(x, g, out_dtype, tm=512):
    t, d = x.shape
    return pl.pallas_call(
        _rmsnorm_kernel,
        grid=(t // tm,),
        in_specs=[pl.BlockSpec((tm, d), lambda i: (i, 0)),
                  pl.BlockSpec((1, d), lambda i: (0, 0))],
        out_specs=pl.BlockSpec((tm, d), lambda i: (i, 0)),
        out_shape=jax.ShapeDtypeStruct((t, d), out_dtype),
        compiler_params=_cparams(("parallel",)),
        name="rmsnorm",
    )(x, g.reshape(1, d))


def _mm_kernel(a_ref, b_ref, o_ref):
    o_ref[...] = _dot(a_ref[...], b_ref[...]).astype(o_ref.dtype)


def _mm_res_kernel(a_ref, b_ref, r_ref, o_ref):
    o_ref[...] = (r_ref[...] + _dot(a_ref[...], b_ref[...])).astype(o_ref.dtype)


def _matmul(a, b, *, tm, tn, out_dtype=F32, residual=None, name="matmul"):
    m, k = a.shape
    n = b.shape[1]
    grid = (n // tn, m // tm)
    in_specs = [pl.BlockSpec((tm, k), lambda j, i: (i, 0)),
                pl.BlockSpec((k, tn), lambda j, i: (0, j))]
    args = [a, b]
    kern = _mm_kernel
    if residual is not None:
        in_specs.append(pl.BlockSpec((tm, tn), lambda j, i: (i, j)))
        args.append(residual)
        kern = _mm_res_kernel
    return pl.pallas_call(
        kern,
        grid=grid,
        in_specs=in_specs,
        out_specs=pl.BlockSpec((tm, tn), lambda j, i: (i, j)),
        out_shape=jax.ShapeDtypeStruct((m, n), out_dtype),
        compiler_params=_cparams(("parallel", "parallel")),
        name=name,
    )(*args)


GDN_QKV = 3 * BRANCH_WIDTH
GDN_Z_OFF = GDN_QKV
GDN_BA_OFF = GDN_QKV + BRANCH_WIDTH
GDN_WIDTH = GDN_BA_OFF + LANES


def _gdn_kernel(p_ref, cw_ref, alog_ref, dtb_ref, ng_ref, o_ref,
                xbuf, carry, ybuf, bbuf, gbuf, state, *, tile, nb):
    s = pl.program_id(0)

    @pl.when(s == 0)
    def _():
        state[...] = jnp.zeros_like(state)
        carry[...] = jnp.zeros_like(carry)

    cw = cw_ref[...]
    for bi in range(nb):
        xbuf[bi, 0:SUBLANES, :] = carry[bi]
        xbuf[bi, SUBLANES:SUBLANES + tile, :] = p_ref[bi, :, 0:GDN_QKV]
        acc = xbuf[bi, SUBLANES:SUBLANES + tile, :] * cw[GDN_CONV - 1:GDN_CONV, :]
        for j in range(GDN_CONV - 1):
            off = SUBLANES - (GDN_CONV - 1) + j
            acc = acc + xbuf[bi, off:off + tile, :] * cw[j:j + 1, :]
        carry[bi] = xbuf[bi, tile:tile + SUBLANES, :]
        ybuf[bi] = _silu(acc)
        ba = p_ref[bi, :, GDN_BA_OFF:GDN_BA_OFF + LANES]
        bbuf[bi] = _sigmoid(ba)
        gbuf[bi] = -jnp.exp(alog_ref[...]) * _softplus(ba + dtb_ref[...])

    tri = _tri_inclusive_bf16(CHUNK)
    rb = _RowBlock(GDN_HEADS)
    lane_blk = _iota2(CHUNK, GDN_HEADS * CHUNK, 1) >> (CHUNK.bit_length() - 1)
    ng = ng_ref[...]
    hd = GDN_HEAD_DIM
    bw = BRANCH_WIDTH
    heads = range(GDN_HEADS)
    hsl = [slice(h * hd, (h + 1) * hd) for h in heads]

    def per_head_lanes(cols):
        return jnp.concatenate([jnp.broadcast_to(c, (c.shape[0], hd)) for c in cols], axis=1)

    def per_head_blocks(cols):
        out = cols[GDN_HEADS - 1]
        for h in range(GDN_HEADS - 2, -1, -1):
            out = jnp.where(lane_blk == h, cols[h], out)
        return out

    def l2n(x):
        return jnp.concatenate(
            [x[:, s_] * lax.rsqrt(jnp.sum(x[:, s_] * x[:, s_], axis=-1, keepdims=True) + 1e-6) for s_ in hsl],
            axis=1)

    def chunk_body(ci, _):
        base = pl.multiple_of(ci * (SEQ_CHUNKS_PER_STEP * CHUNK), SEQ_CHUNKS_PER_STEP * CHUNK)
        vrows = [(pl.ds(base + c * CHUNK, CHUNK), bi) for c in range(SEQ_CHUNKS_PER_STEP) for bi in range(nb)]
        yc = [ybuf[bi, r_, :] for r_, bi in vrows]
        bet = [bbuf[bi, r_, :] for r_, bi in vrows]
        gc = [_dot_sel_l(tri, gbuf[bi, r_, :]) for r_, bi in vrows]
        q = [l2n(x[:, 0:bw]) * (hd ** -0.5) for x in yc]
        k = [l2n(x[:, bw:2 * bw]) for x in yc]
        v = [x[:, 2 * bw:3 * bw] for x in yc]
        gcols = [[x[:, GDN_HEADS + h:GDN_HEADS + h + 1] for h in heads] for x in gc]
        glast = [[x[CHUNK - 1:CHUNK, GDN_HEADS + h:GDN_HEADS + h + 1] for h in heads] for x in gc]
        beta_l = [per_head_lanes([x[:, h:h + 1] for h in heads]) for x in bet]
        gcol_l = [per_head_lanes(c) for c in gcols]
        gcol_b = [per_head_blocks(c) for c in gcols]
        grow_b = [jnp.sum(jnp.where(rb.eye > 0, x, 0.0), axis=0, keepdims=True) for x in gcol_b]
        decay = [jnp.where(rb.causal, jnp.exp(jnp.where(rb.causal, x - y, 0.0)), 0.0)
                 for x, y in zip(gcol_b, grow_b)]
        eg = [jnp.exp(x) for x in gcol_l]
        kb = [x * y for x, y in zip(k, beta_l)]
        lhs = [_bf(jnp.concatenate([x, y], axis=0)) for x, y in zip(kb, q)]
        kr = [rb.rhs(x, False)[0] for x in k]
        pm = [_dot_nt(x, y) for x, y in zip(lhs, kr)]
        lower = [jnp.where(rb.strict, x[:CHUNK] * d, 0.0) for x, d in zip(pm, decay)]
        attn = [x[CHUNK:] * d for x, d in zip(pm, decay)]
        minv = _rb_unit_lower_inverse(rb, lower, INV_HI)
        u = [_rb_mul(m, rb.rhs(x * y, INV_HI)) for m, x, y in zip(minv, v, beta_l)]
        w = [_rb_mul(m, rb.rhs(x * y, INV_HI)) for m, x, y in zip(minv, kb, eg)]
        qd = [x * y for x, y in zip(q, eg)]
        k_tail = [[_bf(k[vi][:, hsl[h]] * jnp.exp(glast[vi][h] - gcols[vi][h])) for h in heads]
                  for vi in range(len(vrows))]
        for c in range(SEQ_CHUNKS_PER_STEP):
            it = [(c * nb + bi, bi * GDN_HEADS + h, h) for bi in range(nb) for h in heads]
            sts = [state[si] for _, si, _ in it]
            stb = [_bf(x) for x in sts]
            wq = [_dot(_bf(jnp.concatenate([w[vi][:, hsl[h]], qd[vi][:, hsl[h]]], axis=0)), sb)
                  for (vi, _, h), sb in zip(it, stb)]
            v_new = {vi: jnp.concatenate([u[vi][:, hsl[h]] - wq[bi * GDN_HEADS + h][:CHUNK] for h in heads], axis=1)
                     for bi, vi in enumerate(range(c * nb, (c + 1) * nb))}
            av = {vi: _rb_mul(attn[vi], rb.rhs(x, False)) for vi, x in v_new.items()}
            upd = [_dot_tn(k_tail[vi][h], _bf(v_new[vi][:, hsl[h]])) for vi, _, h in it]
            for i, (vi, si, h) in enumerate(it):
                r_, bi = vrows[vi]
                state[si] = sts[i] * jnp.exp(glast[vi][h]) + upd[i]
                o = wq[i][CHUNK:] + av[vi][:, hsl[h]]
                z = p_ref[bi, r_, GDN_Z_OFF + h * hd:GDN_Z_OFF + (h + 1) * hd]
                on = (o * lax.rsqrt(jnp.mean(o * o, axis=-1, keepdims=True) + NORM_EPS)) * ng
                o_ref[bi, r_, hsl[h]] = (on * _silu(z)).astype(o_ref.dtype)
        return 0

    lax.fori_loop(0, tile // (SEQ_CHUNKS_PER_STEP * CHUNK), chunk_body, 0)


def _gdn_branch(proj, conv_w, a_log, dt_bias, norm_g, *, batch, seq, tile):
    zeros = jnp.zeros((LANES,), F32)
    alog_p = zeros.at[GDN_HEADS:2 * GDN_HEADS].set(a_log).reshape(1, LANES)
    dtb_p = zeros.at[GDN_HEADS:2 * GDN_HEADS].set(dt_bias).reshape(1, LANES)
    const = lambda s: (0, 0)
    out = pl.pallas_call(
        functools.partial(_gdn_kernel, tile=tile, nb=batch),
        grid=(seq // tile,),
        in_specs=[pl.BlockSpec((batch, tile, GDN_WIDTH), lambda s: (0, s, 0)),
                  pl.BlockSpec((GDN_CONV, GDN_QKV), const),
                  pl.BlockSpec((1, LANES), const),
                  pl.BlockSpec((1, LANES), const),
                  pl.BlockSpec((1, GDN_HEAD_DIM), const)],
        out_specs=pl.BlockSpec((batch, tile, BRANCH_WIDTH), lambda s: (0, s, 0)),
        out_shape=jax.ShapeDtypeStruct((batch, seq, BRANCH_WIDTH), BF16),
        scratch_shapes=[pltpu.VMEM((batch, tile + SUBLANES, GDN_QKV), F32),
                        pltpu.VMEM((batch, SUBLANES, GDN_QKV), F32),
                        pltpu.VMEM((batch, tile, GDN_QKV), F32),
                        pltpu.VMEM((batch, tile, LANES), F32),
                        pltpu.VMEM((batch, tile, LANES), F32),
                        pltpu.VMEM((batch * GDN_HEADS, GDN_HEAD_DIM, GDN_HEAD_DIM), F32)],
        compiler_params=_cparams(("arbitrary",)),
        name="gdn_branch",
    )(proj.reshape(batch, seq, GDN_WIDTH), conv_w, alog_p, dtb_p, norm_g.reshape(1, GDN_HEAD_DIM))
    return out.reshape(batch * seq, BRANCH_WIDTH)


RWKV_R_OFF = 0
RWKV_K_OFF = BRANCH_WIDTH
RWKV_V_OFF = 2 * BRANCH_WIDTH
RWKV_WD_OFF = 3 * BRANCH_WIDTH
RWKV_AD_OFF = RWKV_WD_OFF + LANES
RWKV_GD_OFF = RWKV_AD_OFF + LANES
RWKV_WIDTH = RWKV_GD_OFF + LANES
RWKV_GROUP_HEADS = 4
RWKV_GROUP_W = RWKV_GROUP_HEADS * RWKV_HEAD_DIM


def _rwkv_kernel(p_ref, mu_ref, w0_ref, wup_ref, a0_ref, aup_ref, gup_ref, kk_ref, ka_ref,
                 rk_ref, lng_ref, lnb_ref, o_ref, xbuf, carry, hsbuf, state, *, tile, nb):
    s = pl.program_id(0)

    @pl.when(s == 0)
    def _():
        state[...] = jnp.zeros_like(state)
        carry[...] = jnp.zeros_like(carry)

    for bi in range(nb):
        xbuf[bi, 0:SUBLANES, :] = carry[bi]
        xbuf[bi, SUBLANES:SUBLANES + tile, :] = p_ref[bi]
        hr = xbuf[bi, SUBLANES:SUBLANES + tile, :]
        prev = xbuf[bi, SUBLANES - 1:SUBLANES - 1 + tile, :]
        carry[bi] = xbuf[bi, tile:tile + SUBLANES, :]
        hsbuf[bi] = hr + (prev - hr) * mu_ref[...]

    tri = _tri_inclusive_bf16(CHUNK)
    rb = _RowBlock(RWKV_GROUP_HEADS)
    gw = RWKV_GROUP_W
    seg = rb.bd_mask(RWKV_HEAD_DIM)
    segf = seg.astype(F32)
    inv_hd = 1.0 / RWKV_HEAD_DIM
    ng = BRANCH_WIDTH // gw
    items = [(bi, gi) for bi in range(nb) for gi in range(ng)]
    sl = [slice(gi * gw, (gi + 1) * gw) for _, gi in items]
    bidx = [bi for bi, _ in items]

    def chunk_body(ci, _):
        r0 = pl.multiple_of(ci * CHUNK, CHUNK)
        rows = pl.ds(r0, CHUNK)
        hs = [hsbuf[bi, rows, :] for bi in range(nb)]
        rv = [x[:, RWKV_R_OFF:RWKV_R_OFF + BRANCH_WIDTH] for x in hs]
        kv = [x[:, RWKV_K_OFF:RWKV_K_OFF + BRANCH_WIDTH] for x in hs]
        vv = [x[:, RWKV_V_OFF:RWKV_V_OFF + BRANCH_WIDTH] for x in hs]
        lora_in = jnp.concatenate([x[:, RWKV_WD_OFF:RWKV_WIDTH] for x in hs], axis=0)
        dw = _dot(_bf(jnp.tanh(lora_in[:, 0:LANES])), wup_ref[...])
        da = _dot(_bf(lora_in[:, LANES:2 * LANES]), aup_ref[...])
        gate_all = _dot(_bf(_sigmoid(lora_in[:, 2 * LANES:3 * LANES])), gup_ref[...])
        rowsl = [slice(bi * CHUNK, (bi + 1) * CHUNK) for bi in range(nb)]
        log_w = [-RWKV_DECAY_SCALE * _sigmoid(w0_ref[...] + dw[r_]) for r_ in rowsl]
        a_lr = [_sigmoid(a0_ref[...] + da[r_]) for r_ in rowsl]
        gate = [gate_all[r_] for r_ in rowsl]
        kkr = [x * kk_ref[...] for x in kv]
        kmod = [x * (1.0 + (a - 1.0) * ka_ref[...]) for x, a in zip(kv, a_lr)]
        rk = [x * y * rk_ref[...] for x, y in zip(rv, kmod)]
        g = [_dot_sel_l(tri, x) for x in log_w]
        egn = [jnp.exp(-x) for x in g]
        glast = [x[CHUNK - 1:CHUNK, :] for x in g]
        etail = [jnp.exp(gl - x) for gl, x in zip(glast, g)]
        dec = [jnp.exp(x) for x in glast]
        r_t = [x * jnp.exp(y) for x, y in zip(rv, g)]
        g_prev = [x - y for x, y in zip(g, log_w)]
        kkss = _stacked_sel_r([kkr[bi][:, s_] * kkr[bi][:, s_] for bi, s_ in zip(bidx, sl)], seg)
        bon = _stacked_sel_r([rk[bi][:, s_] for bi, s_ in zip(bidx, sl)], seg)
        kk = [kkr[bi][:, s_] * lax.rsqrt(x + 1e-6) for bi, s_, x in zip(bidx, sl, kkss)]
        b = [x * a_lr[bi][:, s_] for bi, s_, x in zip(bidx, sl, kk)]
        a_t = [-x * jnp.exp(g_prev[bi][:, s_]) for bi, s_, x in zip(bidx, sl, kk)]
        vs = [vv[bi][:, s_] for bi, s_ in zip(bidx, sl)]
        km = [kmod[bi][:, s_] for bi, s_ in zip(bidx, sl)]
        lhs = [_bf(jnp.concatenate([x, r_t[bi][:, s_]], axis=0)) for bi, s_, x in zip(bidx, sl, a_t)]
        rbt = [rb.rhs(x * egn[bi][:, s_], False)[0] for bi, s_, x in zip(bidx, sl, b)]
        rkt = [rb.rhs(x * egn[bi][:, s_], False)[0] for bi, s_, x in zip(bidx, sl, km)]
        pb = [_dot_nt(x, y) for x, y in zip(lhs, rbt)]
        pk = [_dot_nt(x, y) for x, y in zip(lhs, rkt)]
        minv = _rb_unit_lower_inverse(rb, [jnp.where(rb.strict, -x[:CHUNK], 0.0) for x in pb], INV_HI)
        rv_rhs = [rb.rhs(x, False) for x in vs]
        a_k = [jnp.concatenate([jnp.where(rb.strict, x[:CHUNK], 0.0), jnp.where(rb.causal, x[CHUNK:], 0.0)],
                               axis=0) for x in pk]
        a_kv = [_rb_mul(x, r) for x, r in zip(a_k, rv_rhs)]
        akv = [x[:CHUNK] for x in a_kv]
        arkv = [x[CHUNK:] for x in a_kv]
        a_rb = [jnp.where(rb.causal, x[CHUNK:], 0.0) for x in pb]
        sts = [state[i] for i in range(len(items))]
        init = [_dot_nt(x, _bf(st)) for x, st in zip(lhs, sts)]
        u = [_rb_mul(m, rb.rhs(x[:CHUNK] + y, INV_HI)) for m, x, y in zip(minv, init, akv)]
        y = [x[CHUNK:] + _rb_mul(p, rb.rhs(q, False)) + w for x, p, q, w in zip(init, a_rb, u, arkv)]
        tails = [_bf(jnp.concatenate([x * etail[bi][:, s_], k_ * etail[bi][:, s_]], axis=0))
                 for bi, s_, x, k_ in zip(bidx, sl, b, km)]
        upd = [_dot_tn(_bf(jnp.concatenate([x, v_], axis=0)), t) for x, v_, t in zip(u, vs, tails)]
        for i, (bi, s_) in enumerate(zip(bidx, sl)):
            state[i] = sts[i] * dec[bi][:, s_] + upd[i] * segf
        mean = [x * inv_hd for x in _stacked_sel_r(y, seg)]
        yc = [x - m for x, m in zip(y, mean)]
        var = [x * inv_hd for x in _stacked_sel_r([x * x for x in yc], seg)]
        for i, (bi, s_) in enumerate(zip(bidx, sl)):
            yn = yc[i] * lax.rsqrt(var[i] + RWKV_LN_EPS) * lng_ref[:, s_] + lnb_ref[:, s_]
            o_ref[bi, rows, s_] = ((yn + bon[i] * vs[i]) * gate[bi][:, s_]).astype(o_ref.dtype)
        return 0

    lax.fori_loop(0, tile // CHUNK, chunk_body, 0)


def _pad_rows(w, rows):
    return jnp.zeros((rows,) + w.shape[1:], w.dtype).at[:w.shape[0]].set(w)


def _rwkv_mu_layout(mu):
    z = jnp.zeros((LANES - RWKV_DECAY_LORA,), mu.dtype)
    o = 3 * BRANCH_WIDTH
    return jnp.concatenate([mu[:o], mu[o:o + RWKV_DECAY_LORA], z,
                            mu[o + RWKV_DECAY_LORA:o + RWKV_DECAY_LORA + RWKV_ICLR_LORA], z,
                            mu[o + RWKV_DECAY_LORA + RWKV_ICLR_LORA:]])


def _rwkv_branch(proj, mu, w0, w_up, a0, a_up, g_up, k_k, k_a, r_k, ln_g, ln_b, *, batch, seq, tile):
    bw = BRANCH_WIDTH
    row = lambda x: x.reshape(1, -1).astype(F32)
    const = lambda s: (0, 0)
    vec = pl.BlockSpec((1, bw), const)
    n_state = batch * (bw // RWKV_GROUP_W)
    out = pl.pallas_call(
        functools.partial(_rwkv_kernel, tile=tile, nb=batch),
        grid=(seq // tile,),
        in_specs=[pl.BlockSpec((batch, tile, RWKV_WIDTH), lambda s: (0, s, 0)),
                  pl.BlockSpec((1, RWKV_WIDTH), const),
                  vec, pl.BlockSpec((LANES, bw), const),
                  vec, pl.BlockSpec((LANES, bw), const),
                  pl.BlockSpec((LANES, bw), const),
                  vec, vec, vec, vec, vec],
        out_specs=pl.BlockSpec((batch, tile, bw), lambda s: (0, s, 0)),
        out_shape=jax.ShapeDtypeStruct((batch, seq, bw), BF16),
        scratch_shapes=[pltpu.VMEM((batch, tile + SUBLANES, RWKV_WIDTH), F32),
                        pltpu.VMEM((batch, SUBLANES, RWKV_WIDTH), F32),
                        pltpu.VMEM((batch, tile, RWKV_WIDTH), F32),
                        pltpu.VMEM((n_state, RWKV_GROUP_W, RWKV_GROUP_W), F32)],
        compiler_params=_cparams(("arbitrary",)),
        name="rwkv_branch",
    )(proj.reshape(batch, seq, RWKV_WIDTH), row(_rwkv_mu_layout(mu)), row(w0), _bf(_pad_rows(w_up, LANES)),
      row(a0), _bf(_pad_rows(a_up, LANES)), _bf(g_up), row(k_k), row(k_a), row(r_k), row(ln_g), row(ln_b))
    return out.reshape(batch * seq, bw)


def _pool_kernel(u_ref, pw_ref, ps_ref, o_ref, xbuf, sbuf, carry, *, tile):
    s = pl.program_id(1)

    @pl.when(s == 0)
    def _():
        carry[...] = jnp.zeros_like(carry)

    hist = POOL_MAX_WINDOW
    n = tile + hist
    xbuf[0:hist, :] = carry[...]
    xbuf[hist:n, :] = u_ref[...]
    carry[...] = xbuf[tile:n, :]
    pos = s * tile + _iota2(tile, LANES, 0)
    gw = LANES
    for gi, win in enumerate(POOL_WINDOWS):
        sl = slice(gi * gw, (gi + 1) * gw)
        sbuf[...] = xbuf[:, sl]
        span = 1
        while span < win:
            sbuf[span:n, :] = sbuf[span:n, :] + sbuf[0:n - span, :]
            span *= 2
        x = xbuf[hist:n, sl]
        count = jnp.minimum(pos + 1, win).astype(F32)
        pooled = sbuf[hist:n, :] / count - x
        y = _dot(_bf(pooled), pw_ref[gi])
        o_ref[:, sl] = (y * ps_ref[:, sl]).astype(o_ref.dtype)


def _pool_branch(u, pool_w, pool_scale, *, batch, seq, tile):
    ns = seq // tile
    bw = BRANCH_WIDTH
    return pl.pallas_call(
        functools.partial(_pool_kernel, tile=tile),
        grid=(batch, ns),
        in_specs=[pl.BlockSpec((tile, bw), lambda b, s: (b * ns + s, 0)),
                  pl.BlockSpec((len(POOL_WINDOWS), LANES, LANES), lambda b, s: (0, 0, 0)),
                  pl.BlockSpec((1, bw), lambda b, s: (0, 0))],
        out_specs=pl.BlockSpec((tile, bw), lambda b, s: (b * ns + s, 0)),
        out_shape=jax.ShapeDtypeStruct((batch * seq, bw), BF16),
        scratch_shapes=[pltpu.VMEM((tile + POOL_MAX_WINDOW, bw), F32),
                        pltpu.VMEM((tile + POOL_MAX_WINDOW, LANES), F32),
                        pltpu.VMEM((POOL_MAX_WINDOW, bw), F32)],
        compiler_params=_cparams(("parallel", "arbitrary")),
        name="pool_branch",
    )(u, _bf(pool_w), pool_scale.reshape(1, bw))


GLA_QW = GLA_HEADS * GLA_KEY_DIM
GLA_Q_OFF = 0
GLA_K_OFF = GLA_QW
GLA_V_OFF = 2 * GLA_QW
GLA_G_OFF = GLA_V_OFF + BRANCH_WIDTH
GLA_F_OFF = GLA_G_OFF + BRANCH_WIDTH
GLA_WIDTH = GLA_F_OFF + LANES


def _gla_kernel(p_ref, fup_ref, fb_ref, ng_ref, o_ref, state, *, tile, nb):
    s = pl.program_id(0)

    @pl.when(s == 0)
    def _():
        state[...] = jnp.zeros_like(state)

    tri = _tri_inclusive_bf16(CHUNK)
    rb = _RowBlock(GLA_HEADS)
    vh = _iota2(BRANCH_WIDTH, GLA_QW, 0) >> (GLA_VAL_DIM.bit_length() - 1)
    kh = _iota2(BRANCH_WIDTH, GLA_QW, 1) >> (GLA_KEY_DIM.bit_length() - 1)
    same_head = (vh == kh).astype(F32)
    ng = ng_ref[...]
    dv = GLA_VAL_DIM
    bs = range(nb)

    def chunk_body(ci, _):
        r0 = pl.multiple_of(ci * CHUNK, CHUNK)
        rows = pl.ds(r0, CHUNK)
        fl = jnp.concatenate([p_ref[bi, rows, GLA_F_OFF:GLA_F_OFF + LANES] for bi in bs], axis=0)
        logits = _dot(_bf(fl), fup_ref[...]) + fb_ref[...]
        log_f = -_softplus(-logits) / GLA_GATE_NORM
        gc = [_dot_sel_l(tri, log_f[bi * CHUNK:(bi + 1) * CHUNK]) for bi in bs]
        q = [p_ref[bi, rows, GLA_Q_OFF:GLA_Q_OFF + GLA_QW] * (GLA_KEY_DIM ** -0.5) for bi in bs]
        k = [p_ref[bi, rows, GLA_K_OFF:GLA_K_OFF + GLA_QW] for bi in bs]
        v = [p_ref[bi, rows, GLA_V_OFF:GLA_V_OFF + BRANCH_WIDTH] for bi in bs]
        q_dec = [_bf(x * jnp.exp(g)) for x, g in zip(q, gc)]
        k_dec = [x * jnp.exp(-g) for x, g in zip(k, gc)]
        glast = [g[CHUNK - 1:CHUNK, :] for g in gc]
        k_tail = [_bf(x * jnp.exp(gl - g)) for x, gl, g in zip(k, glast, gc)]
        attn = [_dot_nt(x, rb.rhs(y, False)[0]) for x, y in zip(q_dec, k_dec)]
        intra = [_rb_mul(jnp.where(rb.causal, a, 0.0), rb.rhs(x, False)) for a, x in zip(attn, v)]
        sts = [state[bi] for bi in bs]
        inter = [_dot_nt(x, _bf(st)) for x, st in zip(q_dec, sts)]
        upd = [_dot_tn(_bf(x), y) for x, y in zip(v, k_tail)]
        for bi in bs:
            state[bi] = sts[bi] * jnp.exp(glast[bi]) + upd[bi] * same_head
            o = intra[bi] + inter[bi]
            for h in range(GLA_HEADS):
                oh = o[:, h * dv:(h + 1) * dv]
                gate = p_ref[bi, rows, GLA_G_OFF + h * dv:GLA_G_OFF + (h + 1) * dv]
                on = (oh * lax.rsqrt(jnp.mean(oh * oh, axis=-1, keepdims=True) + NORM_EPS)) * ng
                o_ref[bi, rows, h * dv:(h + 1) * dv] = (on * _silu(gate)).astype(o_ref.dtype)
        return 0

    lax.fori_loop(0, tile // CHUNK, chunk_body, 0)


def _gla_branch(proj, f_up, f_bias, norm_g, *, batch, seq, tile):
    const = lambda s: (0, 0)
    out = pl.pallas_call(
        functools.partial(_gla_kernel, tile=tile, nb=batch),
        grid=(seq // tile,),
        in_specs=[pl.BlockSpec((batch, tile, GLA_WIDTH), lambda s: (0, s, 0)),
                  pl.BlockSpec((LANES, GLA_QW), const),
                  pl.BlockSpec((1, GLA_QW), const),
                  pl.BlockSpec((1, GLA_VAL_DIM), const)],
        out_specs=pl.BlockSpec((batch, tile, BRANCH_WIDTH), lambda s: (0, s, 0)),
        out_shape=jax.ShapeDtypeStruct((batch, seq, BRANCH_WIDTH), BF16),
        scratch_shapes=[pltpu.VMEM((batch, BRANCH_WIDTH, GLA_QW), F32)],
        compiler_params=_cparams(("arbitrary",)),
        name="gla_branch",
    )(proj.reshape(batch, seq, GLA_WIDTH), _bf(_pad_rows(f_up, LANES)), f_bias.reshape(1, GLA_QW),
      norm_g.reshape(1, GLA_VAL_DIM))
    return out.reshape(batch * seq, BRANCH_WIDTH)


def _merge_kernel(hn_ref, y0, y1, y2, y3, bp_ref, wg0, wg1, wg2, wg3, gb_ref, o_ref):
    ys = (y0, y1, y2, y3)
    wgs = (wg0, wg1, wg2, wg3)
    hn = hn_ref[...]
    acc = None
    for i in range(N_BRANCH):
        logit = _dot(hn, wgs[i][...]) + gb_ref[i]
        gate = 0.5 * (jnp.tanh(0.5 * logit) + 1.0)
        term = gate * _dot(ys[i][...], bp_ref[i])
        acc = term if acc is None else acc + term
    o_ref[...] = acc.astype(o_ref.dtype)


def _merge(hn, ys, branch_proj, w_gate, gate_bias, *, tm, tn):
    t, dm = hn.shape
    d = branch_proj.shape[-1]
    nj = d // tn
    y_spec = pl.BlockSpec((tm, BRANCH_WIDTH), lambda j, i: (i, 0))
    wg_specs = [pl.BlockSpec((dm, tn), functools.partial(lambda j, i, b: (0, b * nj + j), b=b))
                for b in range(N_BRANCH)]
    return pl.pallas_call(
        _merge_kernel,
        grid=(nj, t // tm),
        in_specs=[pl.BlockSpec((tm, dm), lambda j, i: (i, 0))]
        + [y_spec] * N_BRANCH
        + [pl.BlockSpec((N_BRANCH, BRANCH_WIDTH, tn), lambda j, i: (0, 0, j))]
        + wg_specs
        + [pl.BlockSpec((N_BRANCH, 1, tn), lambda j, i: (0, 0, j))],
        out_specs=pl.BlockSpec((tm, tn), lambda j, i: (i, j)),
        out_shape=jax.ShapeDtypeStruct((t, d), BF16),
        compiler_params=_cparams(("parallel", "parallel")),
        name="merge",
    )(hn, *ys, _bf(branch_proj), w_gate, w_gate, w_gate, w_gate, gate_bias.reshape(N_BRANCH, 1, d))


def _rms_rows(x, g):
    return (x * lax.rsqrt(jnp.mean(x * x, axis=-1, keepdims=True) + NORM_EPS)) * g


def _ffn_kernel(h_ref, g_ref, w1_ref, w3_ref, w2_ref, o_ref, xn_ref, acc_ref, *, nf):
    f = pl.program_id(1)

    @pl.when(f == 0)
    def _():
        acc_ref[...] = jnp.zeros_like(acc_ref)
        xn_ref[...] = _bf(_rms_rows(h_ref[...], g_ref[...]))

    x = xn_ref[...]
    mid = _bf(_silu(_dot(x, w1_ref[...])) * _dot(x, w3_ref[...]))
    acc_ref[...] += _dot(mid, w2_ref[...])

    @pl.when(f == nf - 1)
    def _():
        o_ref[...] = h_ref[...] + acc_ref[...]


def _ffn(h, g, w1, w3, w2, *, tm, tf):
    t, d = h.shape
    ff = w1.shape[1]
    nf = ff // tf
    return pl.pallas_call(
        functools.partial(_ffn_kernel, nf=nf),
        grid=(t // tm, nf),
        in_specs=[pl.BlockSpec((tm, d), lambda i, f: (i, 0)),
                  pl.BlockSpec((1, d), lambda i, f: (0, 0)),
                  pl.BlockSpec((d, tf), lambda i, f: (0, f)),
                  pl.BlockSpec((d, tf), lambda i, f: (0, f)),
                  pl.BlockSpec((tf, d), lambda i, f: (f, 0))],
        out_specs=pl.BlockSpec((tm, d), lambda i, f: (i, 0)),
        out_shape=jax.ShapeDtypeStruct((t, d), F32),
        scratch_shapes=[pltpu.VMEM((tm, d), BF16), pltpu.VMEM((tm, d), F32)],
        compiler_params=_cparams(("parallel", "arbitrary")),
        name="ffn",
    )(h, g.reshape(1, d), w1, w3, w2)


ROUTE_E0, ROUTE_E1, ROUTE_RANK0, ROUTE_RANK1 = 0, 1, 2, 3
ROUTE_W0, ROUTE_W1 = 0, 1


def _router_kernel(h_ref, g_ref, rw_ref, ri_ref, rf_ref, cnt_ref, run):
    i = pl.program_id(0)

    @pl.when(i == 0)
    def _():
        run[...] = jnp.zeros_like(run)

    tm = h_ref.shape[0]
    hn = _bf(_rms_rows(h_ref[...], g_ref[...]))
    logits = _dot(hn, rw_ref[...])
    lane = _iota2(tm, LANES, 1)
    neg = jnp.float32(-jnp.inf)
    lg = jnp.where(lane < N_EXPERTS, logits, neg)
    m1 = jnp.max(lg, axis=-1, keepdims=True)
    e0 = jnp.min(jnp.where(lg == m1, lane, LANES), axis=-1, keepdims=True)
    lg2 = jnp.where(lane == e0, neg, lg)
    m2 = jnp.max(lg2, axis=-1, keepdims=True)
    e1 = jnp.min(jnp.where(lg2 == m2, lane, LANES), axis=-1, keepdims=True)
    ex = jnp.exp(m2 - m1)
    den = 1.0 + ex
    w0 = 1.0 / den
    w1 = ex / den
    hit0 = lane == e0
    hit1 = lane == e1
    onehot = (hit0 | hit1).astype(F32)
    rr = _iota2(tm, tm, 0)
    cc = _iota2(tm, tm, 1)
    before = _dot((cc < rr).astype(BF16), _bf(onehot)) + run[...]
    rank0 = jnp.sum(jnp.where(hit0, before, 0.0), axis=-1, keepdims=True).astype(I32)
    rank1 = jnp.sum(jnp.where(hit1, before, 0.0), axis=-1, keepdims=True).astype(I32)
    run[...] += jnp.sum(onehot, axis=0, keepdims=True)
    cnt_ref[...] = run[...]
    ri_ref[...] = jnp.where(lane == ROUTE_E0, e0,
                            jnp.where(lane == ROUTE_E1, e1,
                                      jnp.where(lane == ROUTE_RANK0, rank0,
                                                jnp.where(lane == ROUTE_RANK1, rank1, 0))))
    rf_ref[...] = jnp.where(lane == ROUTE_W0, w0, jnp.where(lane == ROUTE_W1, w1, 0.0))


def _router(h, g, router_w, *, tm=512):
    t, d = h.shape
    rw = _bf(jnp.zeros((d, LANES), F32).at[:, :N_EXPERTS].set(router_w))
    return pl.pallas_call(
        _router_kernel,
        grid=(t // tm,),
        in_specs=[pl.BlockSpec((tm, d), lambda i: (i, 0)),
                  pl.BlockSpec((1, d), lambda i: (0, 0)),
                  pl.BlockSpec((d, LANES), lambda i: (0, 0))],
        out_specs=[pl.BlockSpec((tm, LANES), lambda i: (i, 0)),
                   pl.BlockSpec((tm, LANES), lambda i: (i, 0)),
                   pl.BlockSpec((1, LANES), lambda i: (0, 0))],
        out_shape=[jax.ShapeDtypeStruct((t, LANES), I32),
                   jax.ShapeDtypeStruct((t, LANES), F32),
                   jax.ShapeDtypeStruct((1, LANES), F32)],
        scratch_shapes=[pltpu.VMEM((1, LANES), F32)],
        compiler_params=_cparams(("arbitrary",)),
        name="moe_router",
    )(h, g.reshape(1, d), rw)


def _row_copy(src_ref, src_row, dst_ref, dst_row, sem):
    return pltpu.make_async_copy(src_ref.at[pl.ds(src_row, 1)], dst_ref.at[pl.ds(dst_row, 1)], sem)


DMA_ISSUE_UNROLL = 8


def _dispatch_kernel(d0_ref, d1_ref, zs_ref, zf_ref, x_ref, xb_ref, zbuf, sem, zsem):
    tm = x_ref.shape[0]
    step = pl.program_id(0)
    base = step * tm

    @pl.when(step == 0)
    def _():
        zbuf[...] = jnp.zeros_like(zbuf)

        def zero_copy(j):
            first = pl.multiple_of(zs_ref[j], MOE_BLOCK)
            return pltpu.make_async_copy(zbuf, xb_ref.at[pl.ds(first, MOE_BLOCK)], zsem)

        for j in range(2 * N_EXPERTS):
            @pl.when(zf_ref[j] > 0)
            def _():
                zero_copy(j).start()
        for j in range(2 * N_EXPERTS):
            @pl.when(zf_ref[j] > 0)
            def _():
                zero_copy(j).wait()

    def copies(rw):
        return (_row_copy(x_ref, rw, xb_ref, d0_ref[base + rw], sem),
                _row_copy(x_ref, rw, xb_ref, d1_ref[base + rw], sem))

    def start(rw, _):
        for cp in copies(rw):
            cp.start()
        return 0

    def wait(rw, _):
        for cp in copies(rw):
            cp.wait()
        return 0

    lax.fori_loop(0, tm, start, 0, unroll=DMA_ISSUE_UNROLL)
    lax.fori_loop(0, tm, wait, 0, unroll=DMA_ISSUE_UNROLL)


def _dispatch(x, dest0, dest1, zero_start, zero_flag, n_rows, *, tm=512):
    t, d = x.shape
    return pl.pallas_call(
        _dispatch_kernel,
        grid_spec=pltpu.PrefetchScalarGridSpec(
            num_scalar_prefetch=4,
            grid=(t // tm,),
            in_specs=[pl.BlockSpec((tm, d), lambda i, *_: (i, 0))],
            out_specs=pl.BlockSpec(memory_space=pl.ANY),
            scratch_shapes=[pltpu.VMEM((MOE_BLOCK, d), F32),
                            pltpu.SemaphoreType.DMA(()), pltpu.SemaphoreType.DMA(())]),
        out_shape=jax.ShapeDtypeStruct((n_rows, d), F32),
        compiler_params=_cparams(("arbitrary",)),
        name="moe_dispatch",
    )(dest0, dest1, zero_start, zero_flag, x)


MOE_UNIT_BLOCKS = 2


def _moe_ffn_kernel(ue_ref, ub_ref, uv_ref, zs_ref, zf_ref, *refs, nf):
    del ue_ref
    nu = MOE_UNIT_BLOCKS
    x_refs = refs[:nu]
    g_ref, w1_ref, w3_ref, w2_ref, yb_ref = refs[nu:nu + 5]
    xns = refs[nu + 5:2 * nu + 5]
    accs = refs[2 * nu + 5:3 * nu + 5]
    wb1, wb3, wb2, sem = refs[3 * nu + 5:3 * nu + 9]
    u = pl.program_id(0)
    f = pl.program_id(1)
    slot = [u * nu + j for j in range(nu)]

    @pl.when((u == 0) & (f == 0))
    def _():
        zsrc = accs[nu - 1]
        zsrc[...] = jnp.zeros_like(zsrc)

        def zero_copy(j):
            first = pl.multiple_of(zs_ref[j], MOE_BLOCK)
            return pltpu.make_async_copy(zsrc, yb_ref.at[pl.ds(first, MOE_BLOCK)], sem)

        for j in range(N_EXPERTS):
            @pl.when(zf_ref[j] > 0)
            def _():
                zero_copy(j).start()
        for j in range(N_EXPERTS):
            @pl.when(zf_ref[j] > 0)
            def _():
                zero_copy(j).wait()

    @pl.when(f == 0)
    def _():
        for j in range(nu):
            @pl.when(uv_ref[slot[j]] > 0)
            def _():
                accs[j][...] = jnp.zeros_like(accs[j])
                xns[j][...] = _bf(_rms_rows(x_refs[j][...], g_ref[...]))

    for j in range(nu):
        @pl.when(uv_ref[slot[j]] > 0)
        def _():
            x = xns[j][...]
            if j == 0:
                w1, w3, w2 = _bf(w1_ref[0]), _bf(w3_ref[0]), _bf(w2_ref[0])
                if nu > 1:
                    wb1[...] = w1
                    wb3[...] = w3
                    wb2[...] = w2
            else:
                w1, w3, w2 = wb1[...], wb3[...], wb2[...]
            mid = _bf(_silu(_dot(x, w1)) * _dot(x, w3))
            accs[j][...] += _dot(mid, w2)

    @pl.when(f == nf - 1)
    def _():
        def out_copy(j):
            first = pl.multiple_of(ub_ref[slot[j]] * MOE_BLOCK, MOE_BLOCK)
            return pltpu.make_async_copy(accs[j], yb_ref.at[pl.ds(first, MOE_BLOCK)], sem)

        for j in range(nu):
            @pl.when(uv_ref[slot[j]] > 0)
            def _():
                out_copy(j).start()
        for j in range(nu):
            @pl.when(uv_ref[slot[j]] > 0)
            def _():
                out_copy(j).wait()


def _moe_ffn(xb, g, w1, w3, w2, unit_e, unit_b, unit_v, spare_start, spare_flag, *, tf):
    n_rows, d = xb.shape
    ff = w1.shape[-1]
    nf = ff // tf
    nu = MOE_UNIT_BLOCKS
    n_units = unit_e.shape[0]
    wcol = lambda u, f, ue, ub, uv, *_: (ue[u], 0, f * uv[u * nu])
    wrow = lambda u, f, ue, ub, uv, *_: (ue[u], f * uv[u * nu], 0)
    x_specs = [pl.BlockSpec((MOE_BLOCK, d), functools.partial(lambda u, f, ue, ub, *_, j: (ub[u * nu + j], 0), j=j),
                            pipeline_mode=pl.Buffered(1)) for j in range(nu)]
    return pl.pallas_call(
        functools.partial(_moe_ffn_kernel, nf=nf),
        grid_spec=pltpu.PrefetchScalarGridSpec(
            num_scalar_prefetch=5,
            grid=(n_units, nf),
            in_specs=x_specs + [pl.BlockSpec((1, d), lambda u, f, *_: (0, 0)),
                                pl.BlockSpec((1, d, tf), wcol),
                                pl.BlockSpec((1, d, tf), wcol),
                                pl.BlockSpec((1, tf, d), wrow)],
            out_specs=pl.BlockSpec(memory_space=pl.ANY),
            scratch_shapes=[pltpu.VMEM((MOE_BLOCK, d), BF16)] * nu + [pltpu.VMEM((MOE_BLOCK, d), F32)] * nu
            + [pltpu.VMEM((d, tf), BF16), pltpu.VMEM((d, tf), BF16), pltpu.VMEM((tf, d), BF16),
               pltpu.SemaphoreType.DMA(())]),
        out_shape=jax.ShapeDtypeStruct((n_rows, d), F32),
        compiler_params=_cparams(("arbitrary", "arbitrary")),
        name="moe_experts",
    )(unit_e, unit_b, unit_v, spare_start, spare_flag, *([xb] * nu), g.reshape(1, d), w1, w3, w2)


def _combine_kernel(d0_ref, d1_ref, yb_ref, h_ref, rf_ref, g_ref, o_ref, buf0, buf1, sem, *, final_norm):
    tm = h_ref.shape[0]
    base = pl.program_id(0) * tm

    def copies(rw):
        return (_row_copy(yb_ref, d0_ref[base + rw], buf0, rw, sem),
                _row_copy(yb_ref, d1_ref[base + rw], buf1, rw, sem))

    def start(rw, _):
        for cp in copies(rw):
            cp.start()
        return 0

    def wait(rw, _):
        for cp in copies(rw):
            cp.wait()
        return 0

    lax.fori_loop(0, tm, start, 0, unroll=DMA_ISSUE_UNROLL)
    lax.fori_loop(0, tm, wait, 0, unroll=DMA_ISSUE_UNROLL)
    w0 = rf_ref[:, ROUTE_W0:ROUTE_W0 + 1]
    w1 = rf_ref[:, ROUTE_W1:ROUTE_W1 + 1]
    out = h_ref[...] + (buf0[...] * w0 + buf1[...] * w1)
    o_ref[...] = _rms_rows(out, g_ref[...]) if final_norm else out


def _combine(yb, h, route_f, dest0, dest1, final_g, *, tm=256):
    t, d = h.shape
    g = jnp.ones((1, d), F32) if final_g is None else final_g.reshape(1, d)
    return pl.pallas_call(
        functools.partial(_combine_kernel, final_norm=final_g is not None),
        grid_spec=pltpu.PrefetchScalarGridSpec(
            num_scalar_prefetch=2,
            grid=(t // tm,),
            in_specs=[pl.BlockSpec(memory_space=pl.ANY),
                      pl.BlockSpec((tm, d), lambda i, d0, d1: (i, 0)),
                      pl.BlockSpec((tm, LANES), lambda i, d0, d1: (i, 0)),
                      pl.BlockSpec((1, d), lambda i, d0, d1: (0, 0))],
            out_specs=pl.BlockSpec((tm, d), lambda i, d0, d1: (i, 0)),
            scratch_shapes=[pltpu.VMEM((tm, d), F32), pltpu.VMEM((tm, d), F32),
                            pltpu.SemaphoreType.DMA(())]),
        out_shape=jax.ShapeDtypeStruct((t, d), F32),
        compiler_params=_cparams(("arbitrary",)),
        name="moe_combine",
    )(dest0, dest1, yb, h, route_f, g)


def _moe(h, g, router_w, w1, w3, w2, final_g=None):
    t, d = h.shape
    route_i, route_f, counts_f = _router(h, g, router_w)
    counts = counts_f[0, :N_EXPERTS].astype(I32)
    padded = (counts + MOE_BLOCK - 1) // MOE_BLOCK * MOE_BLOCK
    pad_end = jnp.cumsum(padded)
    pad_start = pad_end - padded
    n_rows = (-(-(t * 2) // MOE_BLOCK) + N_EXPERTS) * MOE_BLOCK
    n_blocks = n_rows // MOE_BLOCK
    nu = MOE_UNIT_BLOCKS
    blocks_e = padded // MOE_BLOCK
    units_e = (blocks_e + nu - 1) // nu
    unit_end = jnp.cumsum(units_e)
    n_units = -(-n_blocks // nu) + N_EXPERTS
    uidx = jnp.arange(n_units, dtype=I32)
    unit_e = jnp.minimum(jnp.sum((uidx[:, None] >= unit_end[None, :]).astype(I32), axis=1), N_EXPERTS - 1)
    within = uidx - (unit_end - units_e)[unit_e]
    left = jnp.where(uidx < unit_end[N_EXPERTS - 1], blocks_e[unit_e] - nu * within, 0)
    first_blk = pad_start[unit_e] // MOE_BLOCK + nu * within
    slot_j = jnp.arange(nu, dtype=I32)[None, :]
    unit_v = (slot_j < left[:, None]).astype(I32)
    unit_b = jnp.where(unit_v > 0, first_blk[:, None] + slot_j, jnp.where(left > 0, first_blk, 0)[:, None])
    unit_v = unit_v.reshape(-1)
    unit_b = unit_b.reshape(-1).astype(I32)
    dest0 = pad_start[route_i[:, ROUTE_E0]] + route_i[:, ROUTE_RANK0]
    dest1 = pad_start[route_i[:, ROUTE_E1]] + route_i[:, ROUTE_RANK1]
    spare = pad_end[N_EXPERTS - 1] // MOE_BLOCK + jnp.arange(N_EXPERTS, dtype=I32)
    zero_start = jnp.concatenate([pad_end - MOE_BLOCK, jnp.minimum(spare, n_blocks - 1) * MOE_BLOCK])
    zero_flag = jnp.concatenate([padded > 0, spare < n_blocks]).astype(I32)
    xb = _dispatch(h, dest0, dest1, zero_start.astype(I32), zero_flag, n_rows)
    yb = _moe_ffn(xb, g, w1, w3, w2, unit_e, unit_b, unit_v,
                  zero_start[N_EXPERTS:].astype(I32), zero_flag[N_EXPERTS:], tf=512)
    return _combine(yb, h, route_f, dest0, dest1, final_g)


def _pad_cols(w, width):
    return jnp.concatenate([w, jnp.zeros((w.shape[0], width - w.shape[1]), w.dtype)], axis=1)


def _split_w_in(w_in):
    bw = BRANCH_WIDTH
    d = w_in.shape[0]
    widths = (3 * bw, bw, GDN_HEADS, GDN_HEADS, 3 * bw + RWKV_DECAY_LORA + RWKV_ICLR_LORA + RWKV_GATE_LORA,
              bw, GLA_QW, GLA_QW, bw, bw, GLA_GATE_RANK, N_BRANCH * d)
    offs = [0]
    for w in widths:
        offs.append(offs[-1] + w)
    col = lambda i: w_in[:, offs[i]:offs[i + 1]]
    gdn = jnp.concatenate([col(0), col(1), _pad_cols(jnp.concatenate([col(2), col(3)], axis=1), LANES)], axis=1)
    rw = col(4)
    o = 3 * bw
    rwkv = jnp.concatenate([rw[:, :o],
                            _pad_cols(rw[:, o:o + RWKV_DECAY_LORA], LANES),
                            _pad_cols(rw[:, o + RWKV_DECAY_LORA:o + RWKV_DECAY_LORA + RWKV_ICLR_LORA], LANES),
                            rw[:, o + RWKV_DECAY_LORA + RWKV_ICLR_LORA:]], axis=1)
    pool = col(5)
    gla = jnp.concatenate([col(6), col(7), col(8), col(9), _pad_cols(col(10), LANES)], axis=1)
    gate = col(11)
    return tuple(_bf(w) for w in (gdn, rwkv, pool, gla, gate))


def _mixer(hn, h, layer, batch, seq, w_in, gdn_conv_w, gdn_a_log, gdn_dt_bias, gdn_norm_g,
           rwkv_mu, rwkv_w0, rwkv_w_up, rwkv_a0, rwkv_a_up, rwkv_g_up, rwkv_k_k, rwkv_k_a,
           rwkv_r_k, rwkv_ln_g, rwkv_ln_b, pool_w, pool_scale, gla_f_up, gla_f_bias, gla_norm_g,
           gate_bias, branch_proj, w_out):
    l = layer
    w_gdn, w_rwkv, w_pool, w_gla, w_gate = _split_w_in(w_in[l])
    tm = 512
    p_gdn = _matmul(hn, w_gdn, tm=tm, tn=GDN_WIDTH, name="proj_gdn")
    p_rwkv = _matmul(hn, w_rwkv, tm=tm, tn=RWKV_WIDTH, name="proj_rwkv")
    p_pool = _matmul(hn, w_pool, tm=tm, tn=BRANCH_WIDTH, name="proj_pool")
    p_gla = _matmul(hn, w_gla, tm=tm, tn=GLA_WIDTH, name="proj_gla")
    seq_tile = 256
    kw = dict(batch=batch, seq=seq, tile=seq_tile)
    y_gdn = _gdn_branch(p_gdn, gdn_conv_w[l], gdn_a_log[l], gdn_dt_bias[l], gdn_norm_g[l], **kw)
    y_rwkv = _rwkv_branch(p_rwkv, rwkv_mu[l], rwkv_w0[l], rwkv_w_up[l], rwkv_a0[l], rwkv_a_up[l],
                          rwkv_g_up[l], rwkv_k_k[l], rwkv_k_a[l], rwkv_r_k[l].reshape(-1),
                          rwkv_ln_g[l], rwkv_ln_b[l], **kw)
    y_pool = _pool_branch(p_pool, pool_w[l], pool_scale[l], **kw)
    y_gla = _gla_branch(p_gla, gla_f_up[l], gla_f_bias[l], gla_norm_g[l], **kw)
    mixed = _merge(hn, (y_gdn, y_rwkv, y_pool, y_gla), branch_proj[l], w_gate, gate_bias[l], tm=512, tn=512)
    return _matmul(mixed, _bf(w_out[l]), tm=512, tn=1024, residual=h, name="out_proj")


def kernel(x, norm1_g, w_in, gdn_conv_w, gdn_a_log, gdn_dt_bias, gdn_norm_g, rwkv_mu, rwkv_w0, rwkv_w_up, rwkv_a0, rwkv_a_up, rwkv_g_up, rwkv_k_k, rwkv_k_a, rwkv_r_k, rwkv_ln_g, rwkv_ln_b, pool_w, pool_scale, gla_f_up, gla_f_bias, gla_norm_g, gate_bias, branch_proj, w_out, norm2_g, ffn_w1, ffn_w3, ffn_w2, moe_router, moe_w1, moe_w3, moe_w2, final_norm_g):
    batch, seq, d = x.shape
    depth = norm1_g.shape[0]
    h = x.reshape(batch * seq, d)
    for layer in range(depth):
        hn = _rmsnorm(h, norm1_g[layer], BF16)
        h = _mixer(hn, h, layer, batch, seq, w_in, gdn_conv_w, gdn_a_log, gdn_dt_bias, gdn_norm_g,
                   rwkv_mu, rwkv_w0, rwkv_w_up, rwkv_a0, rwkv_a_up, rwkv_g_up, rwkv_k_k, rwkv_k_a,
                   rwkv_r_k, rwkv_ln_g, rwkv_ln_b, pool_w, pool_scale, gla_f_up, gla_f_bias,
                   gla_norm_g, gate_bias, branch_proj, w_out)
        i = layer // 2
        last = layer == depth - 1
        if layer % 2 == 0:
            h = _ffn(h, norm2_g[layer], _bf(ffn_w1[i]), _bf(ffn_w3[i]), _bf(ffn_w2[i]), tm=512, tf=512)
            if last:
                h = _rmsnorm(h, final_norm_g, F32)
        else:
            h = _moe(h, norm2_g[layer], moe_router[i], moe_w1[i], moe_w3[i], moe_w2[i],
                     final_g=final_norm_g if last else None)
    return h.reshape(batch, seq, d)
```

```python
import functools

import jax
import jax.numpy as jnp
from jax import lax
from jax.experimental import pallas as pl
from jax.experimental.pallas import tpu as pltpu

F32 = jnp.float32
BF16 = jnp.bfloat16
I32 = jnp.int32

NORM_EPS = 1e-6
CHUNK = 64
N_BRANCH = 4
BRANCH_WIDTH = 512
GDN_HEADS = 4
GDN_HEAD_DIM = 128
GDN_CONV = 4
RWKV_HEAD_DIM = 64
RWKV_DECAY_LORA = 64
RWKV_ICLR_LORA = 64
RWKV_GATE_LORA = 128
RWKV_DECAY_SCALE = 0.606531
RWKV_LN_EPS = 64e-5
POOL_WINDOWS = (2, 4, 8, 16)
POOL_MAX_WINDOW = 16
GLA_HEADS = 4
GLA_KEY_DIM = 64
GLA_VAL_DIM = 128
GLA_GATE_RANK = 16
GLA_GATE_NORM = 16.0
N_EXPERTS = 8
MOE_BLOCK = 512

LANES = 128
SUBLANES = 8
VMEM_LIMIT_BYTES = 56 * 1024 * 1024

INV_BLOCK = 16
INV_HI = False
SEQ_CHUNKS_PER_STEP = 2


def _cparams(sem):
    return pltpu.CompilerParams(dimension_semantics=sem, vmem_limit_bytes=VMEM_LIMIT_BYTES)


def _dot(a, b):
    return jnp.dot(a, b, preferred_element_type=F32)


def _dot_nt(a, b):
    return lax.dot_general(a, b, (((1,), (1,)), ((), ())), preferred_element_type=F32)


def _dot_tn(a, b):
    return lax.dot_general(a, b, (((0,), (0,)), ((), ())), preferred_element_type=F32)


def _bf(x):
    return x.astype(BF16)


def _split_terms(x, terms):
    out = []
    rem = x
    for i in range(terms):
        xi = _bf(rem)
        out.append(xi)
        if i + 1 < terms:
            rem = rem - xi.astype(F32)
    return out


def _dot_sel_r(x, sel, terms=3):
    parts = [_dot(t, sel) for t in _split_terms(x, terms)]
    return functools.reduce(lambda a, b: a + b, parts)


def _dot_sel_l(sel, x, terms=3):
    parts = [_dot(sel, t) for t in _split_terms(x, terms)]
    return functools.reduce(lambda a, b: a + b, parts)


def _stacked_sel_r(xs, sel, terms=2):
    n = len(xs)
    rows = xs[0].shape[0]
    parts = [_split_terms(x, terms) for x in xs]
    out = _dot(jnp.concatenate([p[t] for t in range(terms) for p in parts], axis=0), sel)
    res = []
    for i in range(n):
        acc = out[i * rows:(i + 1) * rows]
        for t in range(1, terms):
            acc = acc + out[(t * n + i) * rows:(t * n + i + 1) * rows]
        res.append(acc)
    return res


def _dot_hi(a, b):
    ah = _bf(a)
    al = _bf(a - ah.astype(F32))
    bh = _bf(b)
    bl = _bf(b - bh.astype(F32))
    return _dot(ah, bh) + _dot(ah, bl) + _dot(al, bh)


def _sigmoid(x):
    return jax.nn.sigmoid(x)


def _silu(x):
    return x * jax.nn.sigmoid(x)


def _softplus(x):
    return jnp.maximum(x, 0.0) + jnp.log1p(jnp.exp(-jnp.abs(x)))


def _iota2(n, m, dim):
    return lax.broadcasted_iota(I32, (n, m), dim)


def _tri_inclusive_bf16(n):
    r = _iota2(n, n, 0)
    c = _iota2(n, n, 1)
    return (c <= r).astype(BF16)


class _RowBlock:
    def __init__(self, nblk):
        self.nblk = nblk
        n = nblk * CHUNK
        self._shift = CHUNK.bit_length() - 1
        row = _iota2(CHUNK, n, 0)
        colw = _iota2(CHUNK, n, 1) & (CHUNK - 1)
        self.eye = (row == colw).astype(F32)
        self.strict = colw < row
        self.causal = colw <= row
        ish = INV_BLOCK.bit_length() - 1
        self.same_diag_block = (row >> ish) == (colw >> ish)
        self._bd = {}

    def bd_mask(self, w):
        if w not in self._bd:
            n = self.nblk * CHUNK
            rb = _iota2(n, self.nblk * w, 0) >> self._shift
            cb = _iota2(n, self.nblk * w, 1) >> (w.bit_length() - 1)
            self._bd[w] = (rb == cb).astype(BF16)
        return self._bd[w]

    def rhs(self, y, hi):
        mask = self.bd_mask(y.shape[1] // self.nblk)
        yh = _bf(y)
        bdh = jnp.concatenate([yh] * self.nblk, axis=0) * mask
        if not hi:
            return (bdh, None)
        yl = _bf(y - yh.astype(F32))
        return (bdh, jnp.concatenate([yl] * self.nblk, axis=0) * mask)


def _rb_mul(x, rhs):
    bdh, bdl = rhs
    xh = _bf(x)
    out = _dot(xh, bdh)
    if bdl is not None:
        xl = _bf(x - xh.astype(F32))
        out = out + _dot(xh, bdl) + _dot(xl, bdh)
    return out


def _rb_unit_lower_inverse(rb, lows, hi):
    a = [jnp.where(rb.same_diag_block, -l, 0.0) for l in lows]
    off = [jnp.where(rb.same_diag_block, 0.0, l) for l in lows]
    t = [rb.eye + x for x in a]
    p = [_rb_mul(x, rb.rhs(x, hi)) for x in a]
    k = 4
    while k <= INV_BLOCK:
        rp = [rb.rhs(x, hi) for x in p]
        if k < INV_BLOCK:
            both = [_rb_mul(jnp.concatenate([x, y], axis=0), r) for x, y, r in zip(t, p, rp)]
            t = [x + z[:CHUNK] for x, z in zip(t, both)]
            p = [z[CHUNK:] for z in both]
        else:
            t = [x + _rb_mul(x, r) for x, r in zip(t, rp)]
        k *= 2
    roff = [rb.rhs(x, hi) for x in off]
    nm = [_rb_mul(x, r) for x, r in zip(t, roff)]
    acc = [rb.eye - x for x in nm]
    rn = [rb.rhs(x, hi) for x in nm]
    pw = nm
    for j in range(2, CHUNK // INV_BLOCK):
        pw = [_rb_mul(x, r) for x, r in zip(pw, rn)]
        acc = [x + y if j % 2 == 0 else x - y for x, y in zip(acc, pw)]
    rt = [rb.rhs(x, hi) for x in t]
    return [_rb_mul(x, r) for x, r in zip(acc, rt)]


def _rmsnorm_kernel(x_ref, g_ref, o_ref):
    x = x_ref[...]
    ms = jnp.mean(x * x, axis=-1, keepdims=True)
    o_ref[...] = ((x * lax.rsqrt(ms + NORM_EPS)) * g_ref[...]).astype(o_ref.dtype)


def _rmsnorm(x, g, out_dtype, tm=512):
    t, d = x.shape
    return pl.pallas_call(
        _rmsnorm_kernel,
        grid=(t // tm,),
        in_specs=[pl.BlockSpec((tm, d), lambda i: (i, 0)),
                  pl.BlockSpec((1, d), lambda i: (0, 0))],
        out_specs=pl.BlockSpec((tm, d), lambda i: (i, 0)),
        out_shape=jax.ShapeDtypeStruct((t, d), out_dtype),
        compiler_params=_cparams(("parallel",)),
        name="rmsnorm",
    )(x, g.reshape(1, d))


def _mm_kernel(a_ref, b_ref, o_ref):
    o_ref[...] = _dot(a_ref[...], b_ref[...]).astype(o_ref.dtype)


def _mm_res_kernel(a_ref, b_ref, r_ref, o_ref):
    o_ref[...] = (r_ref[...] + _dot(a_ref[...], b_ref[...])).astype(o_ref.dtype)


def _matmul(a, b, *, tm, tn, out_dtype=F32, residual=None, name="matmul"):
    m, k = a.shape
    n = b.shape[1]
    grid = (n // tn, m // tm)
    in_specs = [pl.BlockSpec((tm, k), lambda j, i: (i, 0)),
                pl.BlockSpec((k, tn), lambda j, i: (0, j))]
    args = [a, b]
    kern = _mm_kernel
    if residual is not None:
        in_specs.append(pl.BlockSpec((tm, tn), lambda j, i: (i, j)))
        args.append(residual)
        kern = _mm_res_kernel
    return pl.pallas_call(
        kern,
        grid=grid,
        in_specs=in_specs,
        out_specs=pl.BlockSpec((tm, tn), lambda j, i: (i, j)),
        out_shape=jax.ShapeDtypeStruct((m, n), out_dtype),
        compiler_params=_cparams(("parallel", "parallel")),
        name=name,
    )(*args)


GDN_QKV = 3 * BRANCH_WIDTH
GDN_Z_OFF = GDN_QKV
GDN_BA_OFF = GDN_QKV + BRANCH_WIDTH
GDN_WIDTH = GDN_BA_OFF + LANES


def _gdn_kernel(p_ref, cw_ref, alog_ref, dtb_ref, ng_ref, o_ref,
                xbuf, carry, ybuf, bbuf, gbuf, state, *, tile, nb):
    s = pl.program_id(0)

    @pl.when(s == 0)
    def _():
        state[...] = jnp.zeros_like(state)
        carry[...] = jnp.zeros_like(carry)

    cw = cw_ref[...]
    for bi in range(nb):
        xbuf[bi, 0:SUBLANES, :] = carry[bi]
        xbuf[bi, SUBLANES:SUBLANES + tile, :] = p_ref[bi, :, 0:GDN_QKV]
        acc = xbuf[bi, SUBLANES:SUBLANES + tile, :] * cw[GDN_CONV - 1:GDN_CONV, :]
        for j in range(GDN_CONV - 1):
            off = SUBLANES - (GDN_CONV - 1) + j
            acc = acc + xbuf[bi, off:off + tile, :] * cw[j:j + 1, :]
        carry[bi] = xbuf[bi, tile:tile + SUBLANES, :]
        ybuf[bi] = _silu(acc)
        ba = p_ref[bi, :, GDN_BA_OFF:GDN_BA_OFF + LANES]
        bbuf[bi] = _sigmoid(ba)
        gbuf[bi] = -jnp.exp(alog_ref[...]) * _softplus(ba + dtb_ref[...])

    tri = _tri_inclusive_bf16(CHUNK)
    rb = _RowBlock(GDN_HEADS)
    lane_blk = _iota2(CHUNK, GDN_HEADS * CHUNK, 1) >> (CHUNK.bit_length() - 1)
    ng = ng_ref[...]
    hd = GDN_HEAD_DIM
    bw = BRANCH_WIDTH
    heads = range(GDN_HEADS)
    hsl = [slice(h * hd, (h + 1) * hd) for h in heads]

    def per_head_lanes(cols):
        return jnp.concatenate([jnp.broadcast_to(c, (c.shape[0], hd)) for c in cols], axis=1)

    def per_head_blocks(cols):
        out = cols[GDN_HEADS - 1]
        for h in range(GDN_HEADS - 2, -1, -1):
            out = jnp.where(lane_blk == h, cols[h], out)
        return out

    def l2n(x):
        return jnp.concatenate(
            [x[:, s_] * lax.rsqrt(jnp.sum(x[:, s_] * x[:, s_], axis=-1, keepdims=True) + 1e-6) for s_ in hsl],
            axis=1)

    def chunk_body(ci, _):
        base = pl.multiple_of(ci * (SEQ_CHUNKS_PER_STEP * CHUNK), SEQ_CHUNKS_PER_STEP * CHUNK)
        vrows = [(pl.ds(base + c * CHUNK, CHUNK), bi) for c in range(SEQ_CHUNKS_PER_STEP) for bi in range(nb)]
        yc = [ybuf[bi, r_, :] for r_, bi in vrows]
        bet = [bbuf[bi, r_, :] for r_, bi in vrows]
        gc = [_dot_sel_l(tri, gbuf[bi, r_, :]) for r_, bi in vrows]
        q = [l2n(x[:, 0:bw]) * (hd ** -0.5) for x in yc]
        k = [l2n(x[:, bw:2 * bw]) for x in yc]
        v = [x[:, 2 * bw:3 * bw] for x in yc]
        gcols = [[x[:, GDN_HEADS + h:GDN_HEADS + h + 1] for h in heads] for x in gc]
        glast = [[x[CHUNK - 1:CHUNK, GDN_HEADS + h:GDN_HEADS + h + 1] for h in heads] for x in gc]
        beta_l = [per_head_lanes([x[:, h:h + 1] for h in heads]) for x in bet]
        gcol_l = [per_head_lanes(c) for c in gcols]
        gcol_b = [per_head_blocks(c) for c in gcols]
        grow_b = [jnp.sum(jnp.where(rb.eye > 0, x, 0.0), axis=0, keepdims=True) for x in gcol_b]
        decay = [jnp.where(rb.causal, jnp.exp(jnp.where(rb.causal, x - y, 0.0)), 0.0)
                 for x, y in zip(gcol_b, grow_b)]
        eg = [jnp.exp(x) for x in gcol_l]
        kb = [x * y for x, y in zip(k, beta_l)]
        lhs = [_bf(jnp.concatenate([x, y], axis=0)) for x, y in zip(kb, q)]
        kr = [rb.rhs(x, False)[0] for x in k]
        pm = [_dot_nt(x, y) for x, y in zip(lhs, kr)]
        lower = [jnp.where(rb.strict, x[:CHUNK] * d, 0.0) for x, d in zip(pm, decay)]
        attn = [x[CHUNK:] * d for x, d in zip(pm, decay)]
        minv = _rb_unit_lower_inverse(rb, lower, INV_HI)
        u = [_rb_mul(m, rb.rhs(x * y, INV_HI)) for m, x, y in zip(minv, v, beta_l)]
        w = [_rb_mul(m, rb.rhs(x * y, INV_HI)) for m, x, y in zip(minv, kb, eg)]
        qd = [x * y for x, y in zip(q, eg)]
        k_tail = [[_bf(k[vi][:, hsl[h]] * jnp.exp(glast[vi][h] - gcols[vi][h])) for h in heads]
                  for vi in range(len(vrows))]
        for c in range(SEQ_CHUNKS_PER_STEP):
            it = [(c * nb + bi, bi * GDN_HEADS + h, h) for bi in range(nb) for h in heads]
            sts = [state[si] for _, si, _ in it]
            stb = [_bf(x) for x in sts]
            wq = [_dot(_bf(jnp.concatenate([w[vi][:, hsl[h]], qd[vi][:, hsl[h]]], axis=0)), sb)
                  for (vi, _, h), sb in zip(it, stb)]
            v_new = {vi: jnp.concatenate([u[vi][:, hsl[h]] - wq[bi * GDN_HEADS + h][:CHUNK] for h in heads], axis=1)
                     for bi, vi in enumerate(range(c * nb, (c + 1) * nb))}
            av = {vi: _rb_mul(attn[vi], rb.rhs(x, False)) for vi, x in v_new.items()}
            upd = [_dot_tn(k_tail[vi][h], _bf(v_new[vi][:, hsl[h]])) for vi, _, h in it]
            for i, (vi, si, h) in enumerate(it):
                r_, bi = vrows[vi]
                state[si] = sts[i] * jnp.exp(glast[vi][h]) + upd[i]
                o = wq[i][CHUNK:] + av[vi][:, hsl[h]]
                z = p_ref[bi, r_, GDN_Z_OFF + h * hd:GDN_Z_OFF + (h + 1) * hd]
                on = (o * lax.rsqrt(jnp.mean(o * o, axis=-1, keepdims=True) + NORM_EPS)) * ng
                o_ref[bi, r_, hsl[h]] = (on * _silu(z)).astype(o_ref.dtype)
        return 0

    lax.fori_loop(0, tile // (SEQ_CHUNKS_PER_STEP * CHUNK), chunk_body, 0)


def _gdn_branch(proj, conv_w, a_log, dt_bias, norm_g, *, batch, seq, tile):
    zeros = jnp.zeros((LANES,), F32)
    alog_p = zeros.at[GDN_HEADS:2 * GDN_HEADS].set(a_log).reshape(1, LANES)
    dtb_p = zeros.at[GDN_HEADS:2 * GDN_HEADS].set(dt_bias).reshape(1, LANES)
    const = lambda s: (0, 0)
    out = pl.pallas_call(
        functools.partial(_gdn_kernel, tile=tile, nb=batch),
        grid=(seq // tile,),
        in_specs=[pl.BlockSpec((batch, tile, GDN_WIDTH), lambda s: (0, s, 0)),
                  pl.BlockSpec((GDN_CONV, GDN_QKV), const),
                  pl.BlockSpec((1, LANES), const),
                  pl.BlockSpec((1, LANES), const),
                  pl.BlockSpec((1, GDN_HEAD_DIM), const)],
        out_specs=pl.BlockSpec((batch, tile, BRANCH_WIDTH), lambda s: (0, s, 0)),
        out_shape=jax.ShapeDtypeStruct((batch, seq, BRANCH_WIDTH), BF16),
        scratch_shapes=[pltpu.VMEM((batch, tile + SUBLANES, GDN_QKV), F32),
                        pltpu.VMEM((batch, SUBLANES, GDN_QKV), F32),
                        pltpu.VMEM((batch, tile, GDN_QKV), F32),
                        pltpu.VMEM((batch, tile, LANES), F32),
                        pltpu.VMEM((batch, tile, LANES), F32),
                        pltpu.VMEM((batch * GDN_HEADS, GDN_HEAD_DIM, GDN_HEAD_DIM), F32)],
        compiler_params=_cparams(("arbitrary",)),
        name="gdn_branch",
    )(proj.reshape(batch, seq, GDN_WIDTH), conv_w, alog_p, dtb_p, norm_g.reshape(1, GDN_HEAD_DIM))
    return out.reshape(batch * seq, BRANCH_WIDTH)


RWKV_R_OFF = 0
RWKV_K_OFF = BRANCH_WIDTH
RWKV_V_OFF = 2 * BRANCH_WIDTH
RWKV_WD_OFF = 3 * BRANCH_WIDTH
RWKV_AD_OFF = RWKV_WD_OFF + LANES
RWKV_GD_OFF = RWKV_AD_OFF + LANES
RWKV_WIDTH = RWKV_GD_OFF + LANES
RWKV_GROUP_HEADS = 4
RWKV_GROUP_W = RWKV_GROUP_HEADS * RWKV_HEAD_DIM


def _rwkv_kernel(p_ref, mu_ref, w0_ref, wup_ref, a0_ref, aup_ref, gup_ref, kk_ref, ka_ref,
                 rk_ref, lng_ref, lnb_ref, o_ref, xbuf, carry, hsbuf, state, *, tile, nb):
    s = pl.program_id(0)

    @pl.when(s == 0)
    def _():
        state[...] = jnp.zeros_like(state)
        carry[...] = jnp.zeros_like(carry)

    for bi in range(nb):
        xbuf[bi, 0:SUBLANES, :] = carry[bi]
        xbuf[bi, SUBLANES:SUBLANES + tile, :] = p_ref[bi]
        hr = xbuf[bi, SUBLANES:SUBLANES + tile, :]
        prev = xbuf[bi, SUBLANES - 1:SUBLANES - 1 + tile, :]
        carry[bi] = xbuf[bi, tile:tile + SUBLANES, :]
        hsbuf[bi] = hr + (prev - hr) * mu_ref[...]

    tri = _tri_inclusive_bf16(CHUNK)
    rb = _RowBlock(RWKV_GROUP_HEADS)
    gw = RWKV_GROUP_W
    seg = rb.bd_mask(RWKV_HEAD_DIM)
    segf = seg.astype(F32)
    inv_hd = 1.0 / RWKV_HEAD_DIM
    ng = BRANCH_WIDTH // gw
    items = [(bi, gi) for bi in range(nb) for gi in range(ng)]
    sl = [slice(gi * gw, (gi + 1) * gw) for _, gi in items]
    bidx = [bi for bi, _ in items]

    def chunk_body(ci, _):
        r0 = pl.multiple_of(ci * CHUNK, CHUNK)
        rows = pl.ds(r0, CHUNK)
        hs = [hsbuf[bi, rows, :] for bi in range(nb)]
        rv = [x[:, RWKV_R_OFF:RWKV_R_OFF + BRANCH_WIDTH] for x in hs]
        kv = [x[:, RWKV_K_OFF:RWKV_K_OFF + BRANCH_WIDTH] for x in hs]
        vv = [x[:, RWKV_V_OFF:RWKV_V_OFF + BRANCH_WIDTH] for x in hs]
        lora_in = jnp.concatenate([x[:, RWKV_WD_OFF:RWKV_WIDTH] for x in hs], axis=0)
        dw = _dot(_bf(jnp.tanh(lora_in[:, 0:LANES])), wup_ref[...])
        da = _dot(_bf(lora_in[:, LANES:2 * LANES]), aup_ref[...])
        gate_all = _dot(_bf(_sigmoid(lora_in[:, 2 * LANES:3 * LANES])), gup_ref[...])
        rowsl = [slice(bi * CHUNK, (bi + 1) * CHUNK) for bi in range(nb)]
        log_w = [-RWKV_DECAY_SCALE * _sigmoid(w0_ref[...] + dw[r_]) for r_ in rowsl]
        a_lr = [_sigmoid(a0_ref[...] + da[r_]) for r_ in rowsl]
        gate = [gate_all[r_] for r_ in rowsl]
        kkr = [x * kk_ref[...] for x in kv]
        kmod = [x * (1.0 + (a - 1.0) * ka_ref[...]) for x, a in zip(kv, a_lr)]
        rk = [x * y * rk_ref[...] for x, y in zip(rv, kmod)]
        g = [_dot_sel_l(tri, x) for x in log_w]
        egn = [jnp.exp(-x) for x in g]
        glast = [x[CHUNK - 1:CHUNK, :] for x in g]
        etail = [jnp.exp(gl - x) for gl, x in zip(glast, g)]
        dec = [jnp.exp(x) for x in glast]
        r_t = [x * jnp.exp(y) for x, y in zip(rv, g)]
        g_prev = [x - y for x, y in zip(g, log_w)]
        kkss = _stacked_sel_r([kkr[bi][:, s_] * kkr[bi][:, s_] for bi, s_ in zip(bidx, sl)], seg)
        bon = _stacked_sel_r([rk[bi][:, s_] for bi, s_ in zip(bidx, sl)], seg)
        kk = [kkr[bi][:, s_] * lax.rsqrt(x + 1e-6) for bi, s_, x in zip(bidx, sl, kkss)]
        b = [x * a_lr[bi][:, s_] for bi, s_, x in zip(bidx, sl, kk)]
        a_t = [-x * jnp.exp(g_prev[bi][:, s_]) for bi, s_, x in zip(bidx, sl, kk)]
        vs = [vv[bi][:, s_] for bi, s_ in zip(bidx, sl)]
        km = [kmod[bi][:, s_] for bi, s_ in zip(bidx, sl)]
        lhs = [_bf(jnp.concatenate([x, r_t[bi][:, s_]], axis=0)) for bi, s_, x in zip(bidx, sl, a_t)]
        rbt = [rb.rhs(x * egn[bi][:, s_], False)[0] for bi, s_, x in zip(bidx, sl, b)]
        rkt = [rb.rhs(x * egn[bi][:, s_], False)[0] for bi, s_, x in zip(bidx, sl, km)]
        pb = [_dot_nt(x, y) for x, y in zip(lhs, rbt)]
        pk = [_dot_nt(x, y) for x, y in zip(lhs, rkt)]
        minv = _rb_unit_lower_inverse(rb, [jnp.where(rb.strict, -x[:CHUNK], 0.0) for x in pb], INV_HI)
        rv_rhs = [rb.rhs(x, False) for x in vs]
        a_k = [jnp.concatenate([jnp.where(rb.strict, x[:CHUNK], 0.0), jnp.where(rb.causal, x[CHUNK:], 0.0)],
                               axis=0) for x in pk]
        a_kv = [_rb_mul(x, r) for x, r in zip(a_k, rv_rhs)]
        akv = [x[:CHUNK] for x in a_kv]
        arkv = [x[CHUNK:] for x in a_kv]
        a_rb = [jnp.where(rb.causal, x[CHUNK:], 0.0) for x in pb]
        sts = [state[i] for i in range(len(items))]
        init = [_dot_nt(x, _bf(st)) for x, st in zip(lhs, sts)]
        u = [_rb_mul(m, rb.rhs(x[:CHUNK] + y, INV_HI)) for m, x, y in zip(minv, init, akv)]
        y = [x[CHUNK:] + _rb_mul(p, rb.rhs(q, False)) + w for x, p, q, w in zip(init, a_rb, u, arkv)]
        tails = [_bf(jnp.concatenate([x * etail[bi][:, s_], k_ * etail[bi][:, s_]], axis=0))
                 for bi, s_, x, k_ in zip(bidx, sl, b, km)]
        upd = [_dot_tn(_bf(jnp.concatenate([x, v_], axis=0)), t) for x, v_, t in zip(u, vs, tails)]
        for i, (bi, s_) in enumerate(zip(bidx, sl)):
            state[i] = sts[i] * dec[bi][:, s_] + upd[i] * segf
        mean = [x * inv_hd for x in _stacked_sel_r(y, seg)]
        yc = [x - m for x, m in zip(y, mean)]
        var = [x * inv_hd for x in _stacked_sel_r([x * x for x in yc], seg)]
        for i, (bi, s_) in enumerate(zip(bidx, sl)):
            yn = yc[i] * lax.rsqrt(var[i] + RWKV_LN_EPS) * lng_ref[:, s_] + lnb_ref[:, s_]
            o_ref[bi, rows, s_] = ((yn + bon[i] * vs[i]) * gate[bi][:, s_]).astype(o_ref.dtype)
        return 0

    lax.fori_loop(0, tile // CHUNK, chunk_body, 0)


def _pad_rows(w, rows):
    return jnp.zeros((rows,) + w.shape[1:], w.dtype).at[:w.shape[0]].set(w)


def _rwkv_mu_layout(mu):
    z = jnp.zeros((LANES - RWKV_DECAY_LORA,), mu.dtype)
    o = 3 * BRANCH_WIDTH
    return jnp.concatenate([mu[:o], mu[o:o + RWKV_DECAY_LORA], z,
                            mu[o + RWKV_DECAY_LORA:o + RWKV_DECAY_LORA + RWKV_ICLR_LORA], z,
                            mu[o + RWKV_DECAY_LORA + RWKV_ICLR_LORA:]])


def _rwkv_branch(proj, mu, w0, w_up, a0, a_up, g_up, k_k, k_a, r_k, ln_g, ln_b, *, batch, seq, tile):
    bw = BRANCH_WIDTH
    row = lambda x: x.reshape(1, -1).astype(F32)
    const = lambda s: (0, 0)
    vec = pl.BlockSpec((1, bw), const)
    n_state = batch * (bw // RWKV_GROUP_W)
    out = pl.pallas_call(
        functools.partial(_rwkv_kernel, tile=tile, nb=batch),
        grid=(seq // tile,),
        in_specs=[pl.BlockSpec((batch, tile, RWKV_WIDTH), lambda s: (0, s, 0)),
                  pl.BlockSpec((1, RWKV_WIDTH), const),
                  vec, pl.BlockSpec((LANES, bw), const),
                  vec, pl.BlockSpec((LANES, bw), const),
                  pl.BlockSpec((LANES, bw), const),
                  vec, vec, vec, vec, vec],
        out_specs=pl.BlockSpec((batch, tile, bw), lambda s: (0, s, 0)),
        out_shape=jax.ShapeDtypeStruct((batch, seq, bw), BF16),
        scratch_shapes=[pltpu.VMEM((batch, tile + SUBLANES, RWKV_WIDTH), F32),
                        pltpu.VMEM((batch, SUBLANES, RWKV_WIDTH), F32),
                        pltpu.VMEM((batch, tile, RWKV_WIDTH), F32),
                        pltpu.VMEM((n_state, RWKV_GROUP_W, RWKV_GROUP_W), F32)],
        compiler_params=_cparams(("arbitrary",)),
        name="rwkv_branch",
    )(proj.reshape(batch, seq, RWKV_WIDTH), row(_rwkv_mu_layout(mu)), row(w0), _bf(_pad_rows(w_up, LANES)),
      row(a0), _bf(_pad_rows(a_up, LANES)), _bf(g_up), row(k_k), row(k_a), row(r_k), row(ln_g), row(ln_b))
    return out.reshape(batch * seq, bw)


def _pool_kernel(u_ref, pw_ref, ps_ref, o_ref, xbuf, sbuf, carry, *, tile):
    s = pl.program_id(1)

    @pl.when(s == 0)
    def _():
        carry[...] = jnp.zeros_like(carry)

    hist = POOL_MAX_WINDOW
    n = tile + hist
    xbuf[0:hist, :] = carry[...]
    xbuf[hist:n, :] = u_ref[...]
    carry[...] = xbuf[tile:n, :]
    pos = s * tile + _iota2(tile, LANES, 0)
    gw = LANES
    for gi, win in enumerate(POOL_WINDOWS):
        sl = slice(gi * gw, (gi + 1) * gw)
        sbuf[...] = xbuf[:, sl]
        span = 1
        while span < win:
            sbuf[span:n, :] = sbuf[span:n, :] + sbuf[0:n - span, :]
            span *= 2
        x = xbuf[hist:n, sl]
        count = jnp.minimum(pos + 1, win).astype(F32)
        pooled = sbuf[hist:n, :] / count - x
        y = _dot(_bf(pooled), pw_ref[gi])
        o_ref[:, sl] = (y * ps_ref[:, sl]).astype(o_ref.dtype)


def _pool_branch(u, pool_w, pool_scale, *, batch, seq, tile):
    ns = seq // tile
    bw = BRANCH_WIDTH
    return pl.pallas_call(
        functools.partial(_pool_kernel, tile=tile),
        grid=(batch, ns),
        in_specs=[pl.BlockSpec((tile, bw), lambda b, s: (b * ns + s, 0)),
                  pl.BlockSpec((len(POOL_WINDOWS), LANES, LANES), lambda b, s: (0, 0, 0)),
                  pl.BlockSpec((1, bw), lambda b, s: (0, 0))],
        out_specs=pl.BlockSpec((tile, bw), lambda b, s: (b * ns + s, 0)),
        out_shape=jax.ShapeDtypeStruct((batch * seq, bw), BF16),
        scratch_shapes=[pltpu.VMEM((tile + POOL_MAX_WINDOW, bw), F32),
                        pltpu.VMEM((tile + POOL_MAX_WINDOW, LANES), F32),
                        pltpu.VMEM((POOL_MAX_WINDOW, bw), F32)],
        compiler_params=_cparams(("parallel", "arbitrary")),
        name="pool_branch",
    )(u, _bf(pool_w), pool_scale.reshape(1, bw))


GLA_QW = GLA_HEADS * GLA_KEY_DIM
GLA_Q_OFF = 0
GLA_K_OFF = GLA_QW
GLA_V_OFF = 2 * GLA_QW
GLA_G_OFF = GLA_V_OFF + BRANCH_WIDTH
GLA_F_OFF = GLA_G_OFF + BRANCH_WIDTH
GLA_WIDTH = GLA_F_OFF + LANES


def _gla_kernel(p_ref, fup_ref, fb_ref, ng_ref, o_ref, state, *, tile, nb):
    s = pl.program_id(0)

    @pl.when(s == 0)
    def _():
        state[...] = jnp.zeros_like(state)

    tri = _tri_inclusive_bf16(CHUNK)
    rb = _RowBlock(GLA_HEADS)
    vh = _iota2(BRANCH_WIDTH, GLA_QW, 0) >> (GLA_VAL_DIM.bit_length() - 1)
    kh = _iota2(BRANCH_WIDTH, GLA_QW, 1) >> (GLA_KEY_DIM.bit_length() - 1)
    same_head = (vh == kh).astype(F32)
    ng = ng_ref[...]
    dv = GLA_VAL_DIM
    bs = range(nb)

    def chunk_body(ci, _):
        r0 = pl.multiple_of(ci * CHUNK, CHUNK)
        rows = pl.ds(r0, CHUNK)
        fl = jnp.concatenate([p_ref[bi, rows, GLA_F_OFF:GLA_F_OFF + LANES] for bi in bs], axis=0)
        logits = _dot(_bf(fl), fup_ref[...]) + fb_ref[...]
        log_f = -_softplus(-logits) / GLA_GATE_NORM
        gc = [_dot_sel_l(tri, log_f[bi * CHUNK:(bi + 1) * CHUNK]) for bi in bs]
        q = [p_ref[bi, rows, GLA_Q_OFF:GLA_Q_OFF + GLA_QW] * (GLA_KEY_DIM ** -0.5) for bi in bs]
        k = [p_ref[bi, rows, GLA_K_OFF:GLA_K_OFF + GLA_QW] for bi in bs]
        v = [p_ref[bi, rows, GLA_V_OFF:GLA_V_OFF + BRANCH_WIDTH] for bi in bs]
        q_dec = [_bf(x * jnp.exp(g)) for x, g in zip(q, gc)]
        k_dec = [x * jnp.exp(-g) for x, g in zip(k, gc)]
        glast = [g[CHUNK - 1:CHUNK, :] for g in gc]
        k_tail = [_bf(x * jnp.exp(gl - g)) for x, gl, g in zip(k, glast, gc)]
        attn = [_dot_nt(x, rb.rhs(y, False)[0]) for x, y in zip(q_dec, k_dec)]
        intra = [_rb_mul(jnp.where(rb.causal, a, 0.0), rb.rhs(x, False)) for a, x in zip(attn, v)]
        sts = [state[bi] for bi in bs]
        inter = [_dot_nt(x, _bf(st)) for x, st in zip(q_dec, sts)]
        upd = [_dot_tn(_bf(x), y) for x, y in zip(v, k_tail)]
        for bi in bs:
            state[bi] = sts[bi] * jnp.exp(glast[bi]) + upd[bi] * same_head
            o = intra[bi] + inter[bi]
            for h in range(GLA_HEADS):
                oh = o[:, h * dv:(h + 1) * dv]
                gate = p_ref[bi, rows, GLA_G_OFF + h * dv:GLA_G_OFF + (h + 1) * dv]
                on = (oh * lax.rsqrt(jnp.mean(oh * oh, axis=-1, keepdims=True) + NORM_EPS)) * ng
                o_ref[bi, rows, h * dv:(h + 1) * dv] = (on * _silu(gate)).astype(o_ref.dtype)
        return 0

    lax.fori_loop(0, tile // CHUNK, chunk_body, 0)


def _gla_branch(proj, f_up, f_bias, norm_g, *, batch, seq, tile):
    const = lambda s: (0, 0)
    out = pl.pallas_call(
        functools.partial(_gla_kernel, tile=tile, nb=batch),
        grid=(seq // tile,),
        in_specs=[pl.BlockSpec((batch, tile, GLA_WIDTH), lambda s: (0, s, 0)),
                  pl.BlockSpec((LANES, GLA_QW), const),
                  pl.BlockSpec((1, GLA_QW), const),
                  pl.BlockSpec((1, GLA_VAL_DIM), const)],
        out_specs=pl.BlockSpec((batch, tile, BRANCH_WIDTH), lambda s: (0, s, 0)),
        out_shape=jax.ShapeDtypeStruct((batch, seq, BRANCH_WIDTH), BF16),
        scratch_shapes=[pltpu.VMEM((batch, BRANCH_WIDTH, GLA_QW), F32)],
        compiler_params=_cparams(("arbitrary",)),
        name="gla_branch",
    )(proj.reshape(batch, seq, GLA_WIDTH), _bf(_pad_rows(f_up, LANES)), f_bias.reshape(1, GLA_QW),
      norm_g.reshape(1, GLA_VAL_DIM))
    return out.reshape(batch * seq, BRANCH_WIDTH)


def _merge_kernel(hn_ref, y0, y1, y2, y3, bp_ref, wg0, wg1, wg2, wg3, gb_ref, o_ref):
    ys = (y0, y1, y2, y3)
    wgs = (wg0, wg1, wg2, wg3)
    hn = hn_ref[...]
    acc = None
    for i in range(N_BRANCH):
        logit = _dot(hn, wgs[i][...]) + gb_ref[i]
        gate = 0.5 * (jnp.tanh(0.5 * logit) + 1.0)
        term = gate * _dot(ys[i][...], bp_ref[i])
        acc = term if acc is None else acc + term
    o_ref[...] = acc.astype(o_ref.dtype)


def _merge(hn, ys, branch_proj, w_gate, gate_bias, *, tm, tn):
    t, dm = hn.shape
    d = branch_proj.shape[-1]
    nj = d // tn
    y_spec = pl.BlockSpec((tm, BRANCH_WIDTH), lambda j, i: (i, 0))
    wg_specs = [pl.BlockSpec((dm, tn), functools.partial(lambda j, i, b: (0, b * nj + j), b=b))
                for b in range(N_BRANCH)]
    return pl.pallas_call(
        _merge_kernel,
        grid=(nj, t // tm),
        in_specs=[pl.BlockSpec((tm, dm), lambda j, i: (i, 0))]
        + [y_spec] * N_BRANCH
        + [pl.BlockSpec((N_BRANCH, BRANCH_WIDTH, tn), lambda j, i: (0, 0, j))]
        + wg_specs
        + [pl.BlockSpec((N_BRANCH, 1, tn), lambda j, i: (0, 0, j))],
        out_specs=pl.BlockSpec((tm, tn), lambda j, i: (i, j)),
        out_shape=jax.ShapeDtypeStruct((t, d), BF16),
        compiler_params=_cparams(("parallel", "parallel")),
        name="merge",
    )(hn, *ys, _bf(branch_proj), w_gate, w_gate, w_gate, w_gate, gate_bias.reshape(N_BRANCH, 1, d))


def _rms_rows(x, g):
    return (x * lax.rsqrt(jnp.mean(x * x, axis=-1, keepdims=True) + NORM_EPS)) * g


def _ffn_kernel(h_ref, g_ref, w1_ref, w3_ref, w2_ref, o_ref, xn_ref, acc_ref, *, nf):
    f = pl.program_id(1)

    @pl.when(f == 0)
    def _():
        acc_ref[...] = jnp.zeros_like(acc_ref)
        xn_ref[...] = _bf(_rms_rows(h_ref[...], g_ref[...]))

    x = xn_ref[...]
    mid = _bf(_silu(_dot(x, w1_ref[...])) * _dot(x, w3_ref[...]))
    acc_ref[...] += _dot(mid, w2_ref[...])

    @pl.when(f == nf - 1)
    def _():
        o_ref[...] = h_ref[...] + acc_ref[...]


def _ffn(h, g, w1, w3, w2, *, tm, tf):
    t, d = h.shape
    ff = w1.shape[1]
    nf = ff // tf
    return pl.pallas_call(
        functools.partial(_ffn_kernel, nf=nf),
        grid=(t // tm, nf),
        in_specs=[pl.BlockSpec((tm, d), lambda i, f: (i, 0)),
                  pl.BlockSpec((1, d), lambda i, f: (0, 0)),
                  pl.BlockSpec((d, tf), lambda i, f: (0, f)),
                  pl.BlockSpec((d, tf), lambda i, f: (0, f)),
                  pl.BlockSpec((tf, d), lambda i, f: (f, 0))],
        out_specs=pl.BlockSpec((tm, d), lambda i, f: (i, 0)),
        out_shape=jax.ShapeDtypeStruct((t, d), F32),
        scratch_shapes=[pltpu.VMEM((tm, d), BF16), pltpu.VMEM((tm, d), F32)],
        compiler_params=_cparams(("parallel", "arbitrary")),
        name="ffn",
    )(h, g.reshape(1, d), w1, w3, w2)


ROUTE_E0, ROUTE_E1, ROUTE_RANK0, ROUTE_RANK1 = 0, 1, 2, 3
ROUTE_W0, ROUTE_W1 = 0, 1


def _router_kernel(h_ref, g_ref, rw_ref, ri_ref, rf_ref, cnt_ref, run):
    i = pl.program_id(0)

    @pl.when(i == 0)
    def _():
        run[...] = jnp.zeros_like(run)

    tm = h_ref.shape[0]
    hn = _bf(_rms_rows(h_ref[...], g_ref[...]))
    logits = _dot(hn, rw_ref[...])
    lane = _iota2(tm, LANES, 1)
    neg = jnp.float32(-jnp.inf)
    lg = jnp.where(lane < N_EXPERTS, logits, neg)
    m1 = jnp.max(lg, axis=-1, keepdims=True)
    e0 = jnp.min(jnp.where(lg == m1, lane, LANES), axis=-1, keepdims=True)
    lg2 = jnp.where(lane == e0, neg, lg)
    m2 = jnp.max(lg2, axis=-1, keepdims=True)
    e1 = jnp.min(jnp.where(lg2 == m2, lane, LANES), axis=-1, keepdims=True)
    ex = jnp.exp(m2 - m1)
    den = 1.0 + ex
    w0 = 1.0 / den
    w1 = ex / den
    hit0 = lane == e0
    hit1 = lane == e1
    onehot = (hit0 | hit1).astype(F32)
    rr = _iota2(tm, tm, 0)
    cc = _iota2(tm, tm, 1)
    before = _dot((cc < rr).astype(BF16), _bf(onehot)) + run[...]
    rank0 = jnp.sum(jnp.where(hit0, before, 0.0), axis=-1, keepdims=True).astype(I32)
    rank1 = jnp.sum(jnp.where(hit1, before, 0.0), axis=-1, keepdims=True).astype(I32)
    run[...] += jnp.sum(onehot, axis=0, keepdims=True)
    cnt_ref[...] = run[...]
    ri_ref[...] = jnp.where(lane == ROUTE_E0, e0,
                            jnp.where(lane == ROUTE_E1, e1,
                                      jnp.where(lane == ROUTE_RANK0, rank0,
                                                jnp.where(lane == ROUTE_RANK1, rank1, 0))))
    rf_ref[...] = jnp.where(lane == ROUTE_W0, w0, jnp.where(lane == ROUTE_W1, w1, 0.0))


def _router(h, g, router_w, *, tm=512):
    t, d = h.shape
    rw = _bf(jnp.zeros((d, LANES), F32).at[:, :N_EXPERTS].set(router_w))
    return pl.pallas_call(
        _router_kernel,
        grid=(t // tm,),
        in_specs=[pl.BlockSpec((tm, d), lambda i: (i, 0)),
                  pl.BlockSpec((1, d), lambda i: (0, 0)),
                  pl.BlockSpec((d, LANES), lambda i: (0, 0))],
        out_specs=[pl.BlockSpec((tm, LANES), lambda i: (i, 0)),
                   pl.BlockSpec((tm, LANES), lambda i: (i, 0)),
                   pl.BlockSpec((1, LANES), lambda i: (0, 0))],
        out_shape=[jax.ShapeDtypeStruct((t, LANES), I32),
                   jax.ShapeDtypeStruct((t, LANES), F32),
                   jax.ShapeDtypeStruct((1, LANES), F32)],
        scratch_shapes=[pltpu.VMEM((1, LANES), F32)],
        compiler_params=_cparams(("arbitrary",)),
        name="moe_router",
    )(h, g.reshape(1, d), rw)


def _row_copy(src_ref, src_row, dst_ref, dst_row, sem):
    return pltpu.make_async_copy(src_ref.at[pl.ds(src_row, 1)], dst_ref.at[pl.ds(dst_row, 1)], sem)


DMA_ISSUE_UNROLL = 8


def _dispatch_kernel(d0_ref, d1_ref, zs_ref, zf_ref, x_ref, xb_ref, zbuf, sem, zsem):
    tm = x_ref.shape[0]
    step = pl.program_id(0)
    base = step * tm

    @pl.when(step == 0)
    def _():
        zbuf[...] = jnp.zeros_like(zbuf)

        def zero_copy(j):
            first = pl.multiple_of(zs_ref[j], MOE_BLOCK)
            return pltpu.make_async_copy(zbuf, xb_ref.at[pl.ds(first, MOE_BLOCK)], zsem)

        for j in range(2 * N_EXPERTS):
            @pl.when(zf_ref[j] > 0)
            def _():
                zero_copy(j).start()
        for j in range(2 * N_EXPERTS):
            @pl.when(zf_ref[j] > 0)
            def _():
                zero_copy(j).wait()

    def copies(rw):
        return (_row_copy(x_ref, rw, xb_ref, d0_ref[base + rw], sem),
                _row_copy(x_ref, rw, xb_ref, d1_ref[base + rw], sem))

    def start(rw, _):
        for cp in copies(rw):
            cp.start()
        return 0

    def wait(rw, _):
        for cp in copies(rw):
            cp.wait()
        return 0

    lax.fori_loop(0, tm, start, 0, unroll=DMA_ISSUE_UNROLL)
    lax.fori_loop(0, tm, wait, 0, unroll=DMA_ISSUE_UNROLL)


def _dispatch(x, dest0, dest1, zero_start, zero_flag, n_rows, *, tm=512):
    t, d = x.shape
    return pl.pallas_call(
        _dispatch_kernel,
        grid_spec=pltpu.PrefetchScalarGridSpec(
            num_scalar_prefetch=4,
            grid=(t // tm,),
            in_specs=[pl.BlockSpec((tm, d), lambda i, *_: (i, 0))],
            out_specs=pl.BlockSpec(memory_space=pl.ANY),
            scratch_shapes=[pltpu.VMEM((MOE_BLOCK, d), F32),
                            pltpu.SemaphoreType.DMA(()), pltpu.SemaphoreType.DMA(())]),
        out_shape=jax.ShapeDtypeStruct((n_rows, d), F32),
        compiler_params=_cparams(("arbitrary",)),
        name="moe_dispatch",
    )(dest0, dest1, zero_start, zero_flag, x)


MOE_UNIT_BLOCKS = 2


def _moe_ffn_kernel(ue_ref, ub_ref, uv_ref, zs_ref, zf_ref, *refs, nf):
    del ue_ref
    nu = MOE_UNIT_BLOCKS
    x_refs = refs[:nu]
    g_ref, w1_ref, w3_ref, w2_ref, yb_ref = refs[nu:nu + 5]
    xns = refs[nu + 5:2 * nu + 5]
    accs = refs[2 * nu + 5:3 * nu + 5]
    wb1, wb3, wb2, sem = refs[3 * nu + 5:3 * nu + 9]
    u = pl.program_id(0)
    f = pl.program_id(1)
    slot = [u * nu + j for j in range(nu)]

    @pl.when((u == 0) & (f == 0))
    def _():
        zsrc = accs[nu - 1]
        zsrc[...] = jnp.zeros_like(zsrc)

        def zero_copy(j):
            first = pl.multiple_of(zs_ref[j], MOE_BLOCK)
            return pltpu.make_async_copy(zsrc, yb_ref.at[pl.ds(first, MOE_BLOCK)], sem)

        for j in range(N_EXPERTS):
            @pl.when(zf_ref[j] > 0)
            def _():
                zero_copy(j).start()
        for j in range(N_EXPERTS):
            @pl.when(zf_ref[j] > 0)
            def _():
                zero_copy(j).wait()

    @pl.when(f == 0)
    def _():
        for j in range(nu):
            @pl.when(uv_ref[slot[j]] > 0)
            def _():
                accs[j][...] = jnp.zeros_like(accs[j])
                xns[j][...] = _bf(_rms_rows(x_refs[j][...], g_ref[...]))

    for j in range(nu):
        @pl.when(uv_ref[slot[j]] > 0)
        def _():
            x = xns[j][...]
            if j == 0:
                w1, w3, w2 = _bf(w1_ref[0]), _bf(w3_ref[0]), _bf(w2_ref[0])
                if nu > 1:
                    wb1[...] = w1
                    wb3[...] = w3
                    wb2[...] = w2
            else:
                w1, w3, w2 = wb1[...], wb3[...], wb2[...]
            mid = _bf(_silu(_dot(x, w1)) * _dot(x, w3))
            accs[j][...] += _dot(mid, w2)

    @pl.when(f == nf - 1)
    def _():
        def out_copy(j):
            first = pl.multiple_of(ub_ref[slot[j]] * MOE_BLOCK, MOE_BLOCK)
            return pltpu.make_async_copy(accs[j], yb_ref.at[pl.ds(first, MOE_BLOCK)], sem)

        for j in range(nu):
            @pl.when(uv_ref[slot[j]] > 0)
            def _():
                out_copy(j).start()
        for j in range(nu):
            @pl.when(uv_ref[slot[j]] > 0)
            def _():
                out_copy(j).wait()


def _moe_ffn(xb, g, w1, w3, w2, unit_e, unit_b, unit_v, spare_start, spare_flag, *, tf):
    n_rows, d = xb.shape
    ff = w1.shape[-1]
    nf = ff // tf
    nu = MOE_UNIT_BLOCKS
    n_units = unit_e.shape[0]
    wcol = lambda u, f, ue, ub, uv, *_: (ue[u], 0, f * uv[u * nu])
    wrow = lambda u, f, ue, ub, uv, *_: (ue[u], f * uv[u * nu], 0)
    x_specs = [pl.BlockSpec((MOE_BLOCK, d), functools.partial(lambda u, f, ue, ub, *_, j: (ub[u * nu + j], 0), j=j),
                            pipeline_mode=pl.Buffered(1)) for j in range(nu)]
    return pl.pallas_call(
        functools.partial(_moe_ffn_kernel, nf=nf),
        grid_spec=pltpu.PrefetchScalarGridSpec(
            num_scalar_prefetch=5,
            grid=(n_units, nf),
            in_specs=x_specs + [pl.BlockSpec((1, d), lambda u, f, *_: (0, 0)),
                                pl.BlockSpec((1, d, tf), wcol),
                                pl.BlockSpec((1, d, tf), wcol),
                                pl.BlockSpec((1, tf, d), wrow)],
            out_specs=pl.BlockSpec(memory_space=pl.ANY),
            scratch_shapes=[pltpu.VMEM((MOE_BLOCK, d), BF16)] * nu + [pltpu.VMEM((MOE_BLOCK, d), F32)] * nu
            + [pltpu.VMEM((d, tf), BF16), pltpu.VMEM((d, tf), BF16), pltpu.VMEM((tf, d), BF16),
               pltpu.SemaphoreType.DMA(())]),
        out_shape=jax.ShapeDtypeStruct((n_rows, d), F32),
        compiler_params=_cparams(("arbitrary", "arbitrary")),
        name="moe_experts",
    )(unit_e, unit_b, unit_v, spare_start, spare_flag, *([xb] * nu), g.reshape(1, d), w1, w3, w2)


COMBINE_SLOTS = 2


def _combine_kernel(d0_ref, d1_ref, yb_ref, h_ref, rf_ref, g_ref, o_ref, buf, sems, *, final_norm):
    tm = h_ref.shape[0]
    i = pl.program_id(0)
    n = pl.num_programs(0)
    slot = lax.rem(i, COMBINE_SLOTS)

    def copies(step, sl, rw):
        base = step * tm
        return (_row_copy(yb_ref, d0_ref[base + rw], buf.at[sl, 0], rw, sems.at[sl]),
                _row_copy(yb_ref, d1_ref[base + rw], buf.at[sl, 1], rw, sems.at[sl]))

    def start_all(step, sl):
        def body(rw, _):
            for cp in copies(step, sl, rw):
                cp.start()
            return 0
        lax.fori_loop(0, tm, body, 0, unroll=DMA_ISSUE_UNROLL)

    def wait_all(step, sl):
        def body(rw, _):
            for cp in copies(step, sl, rw):
                cp.wait()
            return 0
        lax.fori_loop(0, tm, body, 0, unroll=DMA_ISSUE_UNROLL)

    @pl.when(i == 0)
    def _():
        start_all(i, slot)

    @pl.when(i + 1 < n)
    def _():
        start_all(i + 1, 1 - slot)

    wait_all(i, slot)
    w0 = rf_ref[:, ROUTE_W0:ROUTE_W0 + 1]
    w1 = rf_ref[:, ROUTE_W1:ROUTE_W1 + 1]
    out = h_ref[...] + (buf[slot, 0] * w0 + buf[slot, 1] * w1)
    o_ref[...] = _rms_rows(out, g_ref[...]) if final_norm else out


def _combine(yb, h, route_f, dest0, dest1, final_g, *, tm=256):
    t, d = h.shape
    g = jnp.ones((1, d), F32) if final_g is None else final_g.reshape(1, d)
    return pl.pallas_call(
        functools.partial(_combine_kernel, final_norm=final_g is not None),
        grid_spec=pltpu.PrefetchScalarGridSpec(
            num_scalar_prefetch=2,
            grid=(t // tm,),
            in_specs=[pl.BlockSpec(memory_space=pl.ANY),
                      pl.BlockSpec((tm, d), lambda i, d0, d1: (i, 0)),
                      pl.BlockSpec((tm, LANES), lambda i, d0, d1: (i, 0)),
                      pl.BlockSpec((1, d), lambda i, d0, d1: (0, 0))],
            out_specs=pl.BlockSpec((tm, d), lambda i, d0, d1: (i, 0)),
            scratch_shapes=[pltpu.VMEM((COMBINE_SLOTS, 2, tm, d), F32),
                            pltpu.SemaphoreType.DMA((COMBINE_SLOTS,))]),
        out_shape=jax.ShapeDtypeStruct((t, d), F32),
        compiler_params=_cparams(("arbitrary",)),
        name="moe_combine",
    )(dest0, dest1, yb, h, route_f, g)


def _moe(h, g, router_w, w1, w3, w2, final_g=None):
    t, d = h.shape
    route_i, route_f, counts_f = _router(h, g, router_w)
    counts = counts_f[0, :N_EXPERTS].astype(I32)
    padded = (counts + MOE_BLOCK - 1) // MOE_BLOCK * MOE_BLOCK
    pad_end = jnp.cumsum(padded)
    pad_start = pad_end - padded
    n_rows = (-(-(t * 2) // MOE_BLOCK) + N_EXPERTS) * MOE_BLOCK
    n_blocks = n_rows // MOE_BLOCK
    nu = MOE_UNIT_BLOCKS
    blocks_e = padded // MOE_BLOCK
    units_e = (blocks_e + nu - 1) // nu
    unit_end = jnp.cumsum(units_e)
    n_units = -(-n_blocks // nu) + N_EXPERTS
    uidx = jnp.arange(n_units, dtype=I32)
    unit_e = jnp.minimum(jnp.sum((uidx[:, None] >= unit_end[None, :]).astype(I32), axis=1), N_EXPERTS - 1)
    within = uidx - (unit_end - units_e)[unit_e]
    left = jnp.where(uidx < unit_end[N_EXPERTS - 1], blocks_e[unit_e] - nu * within, 0)
    first_blk = pad_start[unit_e] // MOE_BLOCK + nu * within
    slot_j = jnp.arange(nu, dtype=I32)[None, :]
    unit_v = (slot_j < left[:, None]).astype(I32)
    unit_b = jnp.where(unit_v > 0, first_blk[:, None] + slot_j, jnp.where(left > 0, first_blk, 0)[:, None])
    unit_v = unit_v.reshape(-1)
    unit_b = unit_b.reshape(-1).astype(I32)
    dest0 = pad_start[route_i[:, ROUTE_E0]] + route_i[:, ROUTE_RANK0]
    dest1 = pad_start[route_i[:, ROUTE_E1]] + route_i[:, ROUTE_RANK1]
    spare = pad_end[N_EXPERTS - 1] // MOE_BLOCK + jnp.arange(N_EXPERTS, dtype=I32)
    zero_start = jnp.concatenate([pad_end - MOE_BLOCK, jnp.minimum(spare, n_blocks - 1) * MOE_BLOCK])
    zero_flag = jnp.concatenate([padded > 0, spare < n_blocks]).astype(I32)
    xb = _dispatch(h, dest0, dest1, zero_start.astype(I32), zero_flag, n_rows)
    yb = _moe_ffn(xb, g, w1, w3, w2, unit_e, unit_b, unit_v,
                  zero_start[N_EXPERTS:].astype(I32), zero_flag[N_EXPERTS:], tf=512)
    return _combine(yb, h, route_f, dest0, dest1, final_g)


def _pad_cols(w, width):
    return jnp.concatenate([w, jnp.zeros((w.shape[0], width - w.shape[1]), w.dtype)], axis=1)


def _split_w_in(w_in):
    bw = BRANCH_WIDTH
    d = w_in.shape[0]
    widths = (3 * bw, bw, GDN_HEADS, GDN_HEADS, 3 * bw + RWKV_DECAY_LORA + RWKV_ICLR_LORA + RWKV_GATE_LORA,
              bw, GLA_QW, GLA_QW, bw, bw, GLA_GATE_RANK, N_BRANCH * d)
    offs = [0]
    for w in widths:
        offs.append(offs[-1] + w)
    col = lambda i: w_in[:, offs[i]:offs[i + 1]]
    gdn = jnp.concatenate([col(0), col(1), _pad_cols(jnp.concatenate([col(2), col(3)], axis=1), LANES)], axis=1)
    rw = col(4)
    o = 3 * bw
    rwkv = jnp.concatenate([rw[:, :o],
                            _pad_cols(rw[:, o:o + RWKV_DECAY_LORA], LANES),
                            _pad_cols(rw[:, o + RWKV_DECAY_LORA:o + RWKV_DECAY_LORA + RWKV_ICLR_LORA], LANES),
                            rw[:, o + RWKV_DECAY_LORA + RWKV_ICLR_LORA:]], axis=1)
    pool = col(5)
    gla = jnp.concatenate([col(6), col(7), col(8), col(9), _pad_cols(col(10), LANES)], axis=1)
    gate = col(11)
    return tuple(_bf(w) for w in (gdn, rwkv, pool, gla, gate))


def _mixer(hn, h, layer, batch, seq, w_in, gdn_conv_w, gdn_a_log, gdn_dt_bias, gdn_norm_g,
           rwkv_mu, rwkv_w0, rwkv_w_up, rwkv_a0, rwkv_a_up, rwkv_g_up, rwkv_k_k, rwkv_k_a,
           rwkv_r_k, rwkv_ln_g, rwkv_ln_b, pool_w, pool_scale, gla_f_up, gla_f_bias, gla_norm_g,
           gate_bias, branch_proj, w_out):
    l = layer
    w_gdn, w_rwkv, w_pool, w_gla, w_gate = _split_w_in(w_in[l])
    tm = 512
    p_gdn = _matmul(hn, w_gdn, tm=tm, tn=GDN_WIDTH, name="proj_gdn")
    p_rwkv = _matmul(hn, w_rwkv, tm=tm, tn=RWKV_WIDTH, name="proj_rwkv")
    p_pool = _matmul(hn, w_pool, tm=tm, tn=BRANCH_WIDTH, name="proj_pool")
    p_gla = _matmul(hn, w_gla, tm=tm, tn=GLA_WIDTH, name="proj_gla")
    seq_tile = 256
    kw = dict(batch=batch, seq=seq, tile=seq_tile)
    y_gdn = _gdn_branch(p_gdn, gdn_conv_w[l], gdn_a_log[l], gdn_dt_bias[l], gdn_norm_g[l], **kw)
    y_rwkv = _rwkv_branch(p_rwkv, rwkv_mu[l], rwkv_w0[l], rwkv_w_up[l], rwkv_a0[l], rwkv_a_up[l],
                          rwkv_g_up[l], rwkv_k_k[l], rwkv_k_a[l], rwkv_r_k[l].reshape(-1),
                          rwkv_ln_g[l], rwkv_ln_b[l], **kw)
    y_pool = _pool_branch(p_pool, pool_w[l], pool_scale[l], **kw)
    y_gla = _gla_branch(p_gla, gla_f_up[l], gla_f_bias[l], gla_norm_g[l], **kw)
    mixed = _merge(hn, (y_gdn, y_rwkv, y_pool, y_gla), branch_proj[l], w_gate, gate_bias[l], tm=512, tn=512)
    return _matmul(mixed, _bf(w_out[l]), tm=512, tn=1024, residual=h, name="out_proj")


def kernel(x, norm1_g, w_in, gdn_conv_w, gdn_a_log, gdn_dt_bias, gdn_norm_g, rwkv_mu, rwkv_w0, rwkv_w_up, rwkv_a0, rwkv_a_up, rwkv_g_up, rwkv_k_k, rwkv_k_a, rwkv_r_k, rwkv_ln_g, rwkv_ln_b, pool_w, pool_scale, gla_f_up, gla_f_bias, gla_norm_g, gate_bias, branch_proj, w_out, norm2_g, ffn_w1, ffn_w3, ffn_w2, moe_router, moe_w1, moe_w3, moe_w2, final_norm_g):
    batch, seq, d = x.shape
    depth = norm1_g.shape[0]
    h = x.reshape(batch * seq, d)
    for layer in range(depth):
        hn = _rmsnorm(h, norm1_g[layer], BF16)
        h = _mixer(hn, h, layer, batch, seq, w_in, gdn_conv_w, gdn_a_log, gdn_dt_bias, gdn_norm_g,
                   rwkv_mu, rwkv_w0, rwkv_w_up, rwkv_a0, rwkv_a_up, rwkv_g_up, rwkv_k_k, rwkv_k_a,
                   rwkv_r_k, rwkv_ln_g, rwkv_ln_b, pool_w, pool_scale, gla_f_up, gla_f_bias,
                   gla_norm_g, gate_bias, branch_proj, w_out)
        i = layer // 2
        last = layer == depth - 1
        if layer % 2 == 0:
            h = _ffn(h, norm2_g[layer], _bf(ffn_w1[i]), _bf(ffn_w3[i]), _bf(ffn_w2[i]), tm=512, tf=512)
            if last:
                h = _rmsnorm(h, final_norm_g, F32)
        else:
            h = _moe(h, norm2_g[layer], moe_router[i], moe_w1[i], moe_w3[i], moe_w2[i],
                     final_g=final_norm_g if last else None)
    return h.reshape(batch, seq, d)
```

```python
import functools

import jax
import jax.numpy as jnp
from jax import lax
from jax.experimental import pallas as pl
from jax.experimental.pallas import tpu as pltpu

F32 = jnp.float32
BF16 = jnp.bfloat16
I32 = jnp.int32

NORM_EPS = 1e-6
CHUNK = 64
N_BRANCH = 4
BRANCH_WIDTH = 512
GDN_HEADS = 4
GDN_HEAD_DIM = 128
GDN_CONV = 4
RWKV_HEAD_DIM = 64
RWKV_DECAY_LORA = 64
RWKV_ICLR_LORA = 64
RWKV_GATE_LORA = 128
RWKV_DECAY_SCALE = 0.606531
RWKV_LN_EPS = 64e-5
POOL_WINDOWS = (2, 4, 8, 16)
POOL_MAX_WINDOW = 16
GLA_HEADS = 4
GLA_KEY_DIM = 64
GLA_VAL_DIM = 128
GLA_GATE_RANK = 16
GLA_GATE_NORM = 16.0
N_EXPERTS = 8
MOE_BLOCK = 512

LANES = 128
SUBLANES = 8
VMEM_LIMIT_BYTES = 56 * 1024 * 1024

INV_BLOCK = 16
INV_HI = False
SEQ_CHUNKS_PER_STEP = 2

TOKEN_TILE = 512
SEQ_TILE = 256
MERGE_COL_TILE = 512
OUT_PROJ_COL_TILE = 1024
FFN_HIDDEN_TILE = 512
MOE_HIDDEN_TILE = 512
COMBINE_TILE = 256


def _cparams(sem):
    return pltpu.CompilerParams(dimension_semantics=sem, vmem_limit_bytes=VMEM_LIMIT_BYTES)


def _dot(a, b):
    return jnp.dot(a, b, preferred_element_type=F32)


def _dot_nt(a, b):
    return lax.dot_general(a, b, (((1,), (1,)), ((), ())), preferred_element_type=F32)


def _dot_tn(a, b):
    return lax.dot_general(a, b, (((0,), (0,)), ((), ())), preferred_element_type=F32)


def _bf(x):
    return x.astype(BF16)


def _split_terms(x, terms):
    out = []
    rem = x
    for i in range(terms):
        xi = _bf(rem)
        out.append(xi)
        if i + 1 < terms:
            rem = rem - xi.astype(F32)
    return out


def _dot_sel_l(sel, x, terms=3):
    parts = [_dot(sel, t) for t in _split_terms(x, terms)]
    return functools.reduce(lambda a, b: a + b, parts)


def _stacked_sel_r(xs, sel, terms=2):
    n = len(xs)
    rows = xs[0].shape[0]
    parts = [_split_terms(x, terms) for x in xs]
    out = _dot(jnp.concatenate([p[t] for t in range(terms) for p in parts], axis=0), sel)
    res = []
    for i in range(n):
        acc = out[i * rows:(i + 1) * rows]
        for t in range(1, terms):
            acc = acc + out[(t * n + i) * rows:(t * n + i + 1) * rows]
        res.append(acc)
    return res


def _sigmoid(x):
    return jax.nn.sigmoid(x)


def _silu(x):
    return x * jax.nn.sigmoid(x)


def _softplus(x):
    return jnp.maximum(x, 0.0) + jnp.log1p(jnp.exp(-jnp.abs(x)))


def _iota2(n, m, dim):
    return lax.broadcasted_iota(I32, (n, m), dim)


def _tri_inclusive_bf16(n):
    r = _iota2(n, n, 0)
    c = _iota2(n, n, 1)
    return (c <= r).astype(BF16)


class _RowBlock:
    def __init__(self, nblk):
        self.nblk = nblk
        n = nblk * CHUNK
        self._shift = CHUNK.bit_length() - 1
        row = _iota2(CHUNK, n, 0)
        colw = _iota2(CHUNK, n, 1) & (CHUNK - 1)
        self.eye = (row == colw).astype(F32)
        self.strict = colw < row
        self.causal = colw <= row
        ish = INV_BLOCK.bit_length() - 1
        self.same_diag_block = (row >> ish) == (colw >> ish)
        self._bd = {}

    def bd_mask(self, w):
        if w not in self._bd:
            n = self.nblk * CHUNK
            rb = _iota2(n, self.nblk * w, 0) >> self._shift
            cb = _iota2(n, self.nblk * w, 1) >> (w.bit_length() - 1)
            self._bd[w] = (rb == cb).astype(BF16)
        return self._bd[w]

    def rhs(self, y, hi):
        mask = self.bd_mask(y.shape[1] // self.nblk)
        yh = _bf(y)
        bdh = jnp.concatenate([yh] * self.nblk, axis=0) * mask
        if not hi:
            return (bdh, None)
        yl = _bf(y - yh.astype(F32))
        return (bdh, jnp.concatenate([yl] * self.nblk, axis=0) * mask)


def _rb_mul(x, rhs):
    bdh, bdl = rhs
    xh = _bf(x)
    out = _dot(xh, bdh)
    if bdl is not None:
        xl = _bf(x - xh.astype(F32))
        out = out + _dot(xh, bdl) + _dot(xl, bdh)
    return out


def _rb_unit_lower_inverse(rb, lows, hi):
    a = [jnp.where(rb.same_diag_block, -l, 0.0) for l in lows]
    off = [jnp.where(rb.same_diag_block, 0.0, l) for l in lows]
    t = [rb.eye + x for x in a]
    p = [_rb_mul(x, rb.rhs(x, hi)) for x in a]
    k = 4
    while k <= INV_BLOCK:
        rp = [rb.rhs(x, hi) for x in p]
        if k < INV_BLOCK:
            both = [_rb_mul(jnp.concatenate([x, y], axis=0), r) for x, y, r in zip(t, p, rp)]
            t = [x + z[:CHUNK] for x, z in zip(t, both)]
            p = [z[CHUNK:] for z in both]
        else:
            t = [x + _rb_mul(x, r) for x, r in zip(t, rp)]
        k *= 2
    roff = [rb.rhs(x, hi) for x in off]
    nm = [_rb_mul(x, r) for x, r in zip(t, roff)]
    acc = [rb.eye - x for x in nm]
    rn = [rb.rhs(x, hi) for x in nm]
    pw = nm
    for j in range(2, CHUNK // INV_BLOCK):
        pw = [_rb_mul(x, r) for x, r in zip(pw, rn)]
        acc = [x + y if j % 2 == 0 else x - y for x, y in zip(acc, pw)]
    rt = [rb.rhs(x, hi) for x in t]
    return [_rb_mul(x, r) for x, r in zip(acc, rt)]


def _rmsnorm_kernel(x_ref, g_ref, o_ref):
    x = x_ref[...]
    ms = jnp.mean(x * x, axis=-1, keepdims=True)
    o_ref[...] = ((x * lax.rsqrt(ms + NORM_EPS)) * g_ref[...]).astype(o_ref.dtype)


def _rmsnorm(x, g, out_dtype, tm=TOKEN_TILE):
    t, d = x.shape
    return pl.pallas_call(
        _rmsnorm_kernel,
        grid=(t // tm,),
        in_specs=[pl.BlockSpec((tm, d), lambda i: (i, 0)),
                  pl.BlockSpec((1, d), lambda i: (0, 0))],
        out_specs=pl.BlockSpec((tm, d), lambda i: (i, 0)),
        out_shape=jax.ShapeDtypeStruct((t, d), out_dtype),
        compiler_params=_cparams(("parallel",)),
        name="rmsnorm",
    )(x, g.reshape(1, d))


def _mm_kernel(a_ref, b_ref, o_ref):
    o_ref[...] = _dot(a_ref[...], b_ref[...]).astype(o_ref.dtype)


def _mm_res_kernel(a_ref, b_ref, r_ref, o_ref):
    o_ref[...] = (r_ref[...] + _dot(a_ref[...], b_ref[...])).astype(o_ref.dtype)


def _matmul(a, b, *, tm, tn, out_dtype=F32, residual=None, name="matmul"):
    m, k = a.shape
    n = b.shape[1]
    grid = (n // tn, m // tm)
    in_specs = [pl.BlockSpec((tm, k), lambda j, i: (i, 0)),
                pl.BlockSpec((k, tn), lambda j, i: (0, j))]
    args = [a, b]
    kern = _mm_kernel
    if residual is not None:
        in_specs.append(pl.BlockSpec((tm, tn), lambda j, i: (i, j)))
        args.append(residual)
        kern = _mm_res_kernel
    return pl.pallas_call(
        kern,
        grid=grid,
        in_specs=in_specs,
        out_specs=pl.BlockSpec((tm, tn), lambda j, i: (i, j)),
        out_shape=jax.ShapeDtypeStruct((m, n), out_dtype),
        compiler_params=_cparams(("parallel", "parallel")),
        name=name,
    )(*args)


GDN_QKV = 3 * BRANCH_WIDTH
GDN_Z_OFF = GDN_QKV
GDN_BA_OFF = GDN_QKV + BRANCH_WIDTH
GDN_WIDTH = GDN_BA_OFF + LANES


def _gdn_kernel(p_ref, cw_ref, alog_ref, dtb_ref, ng_ref, o_ref,
                xbuf, carry, ybuf, bbuf, gbuf, state, *, tile, nb):
    s = pl.program_id(0)

    @pl.when(s == 0)
    def _():
        state[...] = jnp.zeros_like(state)
        carry[...] = jnp.zeros_like(carry)

    cw = cw_ref[...]
    for bi in range(nb):
        xbuf[bi, 0:SUBLANES, :] = carry[bi]
        xbuf[bi, SUBLANES:SUBLANES + tile, :] = p_ref[bi, :, 0:GDN_QKV]
        acc = xbuf[bi, SUBLANES:SUBLANES + tile, :] * cw[GDN_CONV - 1:GDN_CONV, :]
        for j in range(GDN_CONV - 1):
            off = SUBLANES - (GDN_CONV - 1) + j
            acc = acc + xbuf[bi, off:off + tile, :] * cw[j:j + 1, :]
        carry[bi] = xbuf[bi, tile:tile + SUBLANES, :]
        ybuf[bi] = _silu(acc)
        ba = p_ref[bi, :, GDN_BA_OFF:GDN_BA_OFF + LANES]
        bbuf[bi] = _sigmoid(ba)
        gbuf[bi] = -jnp.exp(alog_ref[...]) * _softplus(ba + dtb_ref[...])

    tri = _tri_inclusive_bf16(CHUNK)
    rb = _RowBlock(GDN_HEADS)
    lane_blk = _iota2(CHUNK, GDN_HEADS * CHUNK, 1) >> (CHUNK.bit_length() - 1)
    ng = ng_ref[...]
    hd = GDN_HEAD_DIM
    bw = BRANCH_WIDTH
    heads = range(GDN_HEADS)
    hsl = [slice(h * hd, (h + 1) * hd) for h in heads]

    def per_head_lanes(cols):
        return jnp.concatenate([jnp.broadcast_to(c, (c.shape[0], hd)) for c in cols], axis=1)

    def per_head_blocks(cols):
        out = cols[GDN_HEADS - 1]
        for h in range(GDN_HEADS - 2, -1, -1):
            out = jnp.where(lane_blk == h, cols[h], out)
        return out

    def l2n(x):
        return jnp.concatenate(
            [x[:, s_] * lax.rsqrt(jnp.sum(x[:, s_] * x[:, s_], axis=-1, keepdims=True) + 1e-6) for s_ in hsl],
            axis=1)

    def chunk_body(ci, _):
        base = pl.multiple_of(ci * (SEQ_CHUNKS_PER_STEP * CHUNK), SEQ_CHUNKS_PER_STEP * CHUNK)
        vrows = [(pl.ds(base + c * CHUNK, CHUNK), bi) for c in range(SEQ_CHUNKS_PER_STEP) for bi in range(nb)]
        yc = [ybuf[bi, r_, :] for r_, bi in vrows]
        bet = [bbuf[bi, r_, :] for r_, bi in vrows]
        gc = [_dot_sel_l(tri, gbuf[bi, r_, :]) for r_, bi in vrows]
        q = [l2n(x[:, 0:bw]) * (hd ** -0.5) for x in yc]
        k = [l2n(x[:, bw:2 * bw]) for x in yc]
        v = [x[:, 2 * bw:3 * bw] for x in yc]
        gcols = [[x[:, GDN_HEADS + h:GDN_HEADS + h + 1] for h in heads] for x in gc]
        glast = [[x[CHUNK - 1:CHUNK, GDN_HEADS + h:GDN_HEADS + h + 1] for h in heads] for x in gc]
        beta_l = [per_head_lanes([x[:, h:h + 1] for h in heads]) for x in bet]
        gcol_l = [per_head_lanes(c) for c in gcols]
        gcol_b = [per_head_blocks(c) for c in gcols]
        grow_b = [jnp.sum(jnp.where(rb.eye > 0, x, 0.0), axis=0, keepdims=True) for x in gcol_b]
        decay = [jnp.where(rb.causal, jnp.exp(jnp.where(rb.causal, x - y, 0.0)), 0.0)
                 for x, y in zip(gcol_b, grow_b)]
        eg = [jnp.exp(x) for x in gcol_l]
        kb = [x * y for x, y in zip(k, beta_l)]
        lhs = [_bf(jnp.concatenate([x, y], axis=0)) for x, y in zip(kb, q)]
        kr = [rb.rhs(x, False)[0] for x in k]
        pm = [_dot_nt(x, y) for x, y in zip(lhs, kr)]
        lower = [jnp.where(rb.strict, x[:CHUNK] * d, 0.0) for x, d in zip(pm, decay)]
        attn = [x[CHUNK:] * d for x, d in zip(pm, decay)]
        minv = _rb_unit_lower_inverse(rb, lower, INV_HI)
        u = [_rb_mul(m, rb.rhs(x * y, INV_HI)) for m, x, y in zip(minv, v, beta_l)]
        w = [_rb_mul(m, rb.rhs(x * y, INV_HI)) for m, x, y in zip(minv, kb, eg)]
        qd = [x * y for x, y in zip(q, eg)]
        k_tail = [[_bf(k[vi][:, hsl[h]] * jnp.exp(glast[vi][h] - gcols[vi][h])) for h in heads]
                  for vi in range(len(vrows))]
        for c in range(SEQ_CHUNKS_PER_STEP):
            it = [(c * nb + bi, bi * GDN_HEADS + h, h) for bi in range(nb) for h in heads]
            sts = [state[si] for _, si, _ in it]
            stb = [_bf(x) for x in sts]
            wq = [_dot(_bf(jnp.concatenate([w[vi][:, hsl[h]], qd[vi][:, hsl[h]]], axis=0)), sb)
                  for (vi, _, h), sb in zip(it, stb)]
            v_new = {vi: jnp.concatenate([u[vi][:, hsl[h]] - wq[bi * GDN_HEADS + h][:CHUNK] for h in heads], axis=1)
                     for bi, vi in enumerate(range(c * nb, (c + 1) * nb))}
            av = {vi: _rb_mul(attn[vi], rb.rhs(x, False)) for vi, x in v_new.items()}
            upd = [_dot_tn(k_tail[vi][h], _bf(v_new[vi][:, hsl[h]])) for vi, _, h in it]
            for i, (vi, si, h) in enumerate(it):
                r_, bi = vrows[vi]
                state[si] = sts[i] * jnp.exp(glast[vi][h]) + upd[i]
                o = wq[i][CHUNK:] + av[vi][:, hsl[h]]
                z = p_ref[bi, r_, GDN_Z_OFF + h * hd:GDN_Z_OFF + (h + 1) * hd]
                on = (o * lax.rsqrt(jnp.mean(o * o, axis=-1, keepdims=True) + NORM_EPS)) * ng
                o_ref[bi, r_, hsl[h]] = (on * _silu(z)).astype(o_ref.dtype)
        return 0

    lax.fori_loop(0, tile // (SEQ_CHUNKS_PER_STEP * CHUNK), chunk_body, 0)


def _gdn_branch(proj, conv_w, a_log, dt_bias, norm_g, *, batch, seq, tile):
    zeros = jnp.zeros((LANES,), F32)
    alog_p = zeros.at[GDN_HEADS:2 * GDN_HEADS].set(a_log).reshape(1, LANES)
    dtb_p = zeros.at[GDN_HEADS:2 * GDN_HEADS].set(dt_bias).reshape(1, LANES)
    const = lambda s: (0, 0)
    out = pl.pallas_call(
        functools.partial(_gdn_kernel, tile=tile, nb=batch),
        grid=(seq // tile,),
        in_specs=[pl.BlockSpec((batch, tile, GDN_WIDTH), lambda s: (0, s, 0)),
                  pl.BlockSpec((GDN_CONV, GDN_QKV), const),
                  pl.BlockSpec((1, LANES), const),
                  pl.BlockSpec((1, LANES), const),
                  pl.BlockSpec((1, GDN_HEAD_DIM), const)],
        out_specs=pl.BlockSpec((batch, tile, BRANCH_WIDTH), lambda s: (0, s, 0)),
        out_shape=jax.ShapeDtypeStruct((batch, seq, BRANCH_WIDTH), BF16),
        scratch_shapes=[pltpu.VMEM((batch, tile + SUBLANES, GDN_QKV), F32),
                        pltpu.VMEM((batch, SUBLANES, GDN_QKV), F32),
                        pltpu.VMEM((batch, tile, GDN_QKV), F32),
                        pltpu.VMEM((batch, tile, LANES), F32),
                        pltpu.VMEM((batch, tile, LANES), F32),
                        pltpu.VMEM((batch * GDN_HEADS, GDN_HEAD_DIM, GDN_HEAD_DIM), F32)],
        compiler_params=_cparams(("arbitrary",)),
        name="gdn_branch",
    )(proj.reshape(batch, seq, GDN_WIDTH), conv_w, alog_p, dtb_p, norm_g.reshape(1, GDN_HEAD_DIM))
    return out.reshape(batch * seq, BRANCH_WIDTH)


RWKV_R_OFF = 0
RWKV_K_OFF = BRANCH_WIDTH
RWKV_V_OFF = 2 * BRANCH_WIDTH
RWKV_WD_OFF = 3 * BRANCH_WIDTH
RWKV_AD_OFF = RWKV_WD_OFF + LANES
RWKV_GD_OFF = RWKV_AD_OFF + LANES
RWKV_WIDTH = RWKV_GD_OFF + LANES
RWKV_GROUP_HEADS = 4
RWKV_GROUP_W = RWKV_GROUP_HEADS * RWKV_HEAD_DIM


def _rwkv_kernel(p_ref, mu_ref, w0_ref, wup_ref, a0_ref, aup_ref, gup_ref, kk_ref, ka_ref,
                 rk_ref, lng_ref, lnb_ref, o_ref, xbuf, carry, hsbuf, state, *, tile, nb):
    s = pl.program_id(0)

    @pl.when(s == 0)
    def _():
        state[...] = jnp.zeros_like(state)
        carry[...] = jnp.zeros_like(carry)

    for bi in range(nb):
        xbuf[bi, 0:SUBLANES, :] = carry[bi]
        xbuf[bi, SUBLANES:SUBLANES + tile, :] = p_ref[bi]
        hr = xbuf[bi, SUBLANES:SUBLANES + tile, :]
        prev = xbuf[bi, SUBLANES - 1:SUBLANES - 1 + tile, :]
        carry[bi] = xbuf[bi, tile:tile + SUBLANES, :]
        hsbuf[bi] = hr + (prev - hr) * mu_ref[...]

    tri = _tri_inclusive_bf16(CHUNK)
    rb = _RowBlock(RWKV_GROUP_HEADS)
    gw = RWKV_GROUP_W
    seg = rb.bd_mask(RWKV_HEAD_DIM)
    segf = seg.astype(F32)
    inv_hd = 1.0 / RWKV_HEAD_DIM
    ng = BRANCH_WIDTH // gw
    items = [(bi, gi) for bi in range(nb) for gi in range(ng)]
    sl = [slice(gi * gw, (gi + 1) * gw) for _, gi in items]
    bidx = [bi for bi, _ in items]

    def chunk_body(ci, _):
        r0 = pl.multiple_of(ci * CHUNK, CHUNK)
        rows = pl.ds(r0, CHUNK)
        hs = [hsbuf[bi, rows, :] for bi in range(nb)]
        rv = [x[:, RWKV_R_OFF:RWKV_R_OFF + BRANCH_WIDTH] for x in hs]
        kv = [x[:, RWKV_K_OFF:RWKV_K_OFF + BRANCH_WIDTH] for x in hs]
        vv = [x[:, RWKV_V_OFF:RWKV_V_OFF + BRANCH_WIDTH] for x in hs]
        lora_in = jnp.concatenate([x[:, RWKV_WD_OFF:RWKV_WIDTH] for x in hs], axis=0)
        dw = _dot(_bf(jnp.tanh(lora_in[:, 0:LANES])), wup_ref[...])
        da = _dot(_bf(lora_in[:, LANES:2 * LANES]), aup_ref[...])
        gate_all = _dot(_bf(_sigmoid(lora_in[:, 2 * LANES:3 * LANES])), gup_ref[...])
        rowsl = [slice(bi * CHUNK, (bi + 1) * CHUNK) for bi in range(nb)]
        log_w = [-RWKV_DECAY_SCALE * _sigmoid(w0_ref[...] + dw[r_]) for r_ in rowsl]
        a_lr = [_sigmoid(a0_ref[...] + da[r_]) for r_ in rowsl]
        gate = [gate_all[r_] for r_ in rowsl]
        kkr = [x * kk_ref[...] for x in kv]
        kmod = [x * (1.0 + (a - 1.0) * ka_ref[...]) for x, a in zip(kv, a_lr)]
        rk = [x * y * rk_ref[...] for x, y in zip(rv, kmod)]
        g = [_dot_sel_l(tri, x) for x in log_w]
        egn = [jnp.exp(-x) for x in g]
        glast = [x[CHUNK - 1:CHUNK, :] for x in g]
        etail = [jnp.exp(gl - x) for gl, x in zip(glast, g)]
        dec = [jnp.exp(x) for x in glast]
        r_t = [x * jnp.exp(y) for x, y in zip(rv, g)]
        g_prev = [x - y for x, y in zip(g, log_w)]
        kkss = _stacked_sel_r([kkr[bi][:, s_] * kkr[bi][:, s_] for bi, s_ in zip(bidx, sl)], seg)
        bon = _stacked_sel_r([rk[bi][:, s_] for bi, s_ in zip(bidx, sl)], seg)
        kk = [kkr[bi][:, s_] * lax.rsqrt(x + 1e-6) for bi, s_, x in zip(bidx, sl, kkss)]
        b = [x * a_lr[bi][:, s_] for bi, s_, x in zip(bidx, sl, kk)]
        a_t = [-x * jnp.exp(g_prev[bi][:, s_]) for bi, s_, x in zip(bidx, sl, kk)]
        vs = [vv[bi][:, s_] for bi, s_ in zip(bidx, sl)]
        km = [kmod[bi][:, s_] for bi, s_ in zip(bidx, sl)]
        lhs = [_bf(jnp.concatenate([x, r_t[bi][:, s_]], axis=0)) for bi, s_, x in zip(bidx, sl, a_t)]
        rbt = [rb.rhs(x * egn[bi][:, s_], False)[0] for bi, s_, x in zip(bidx, sl, b)]
        rkt = [rb.rhs(x * egn[bi][:, s_], False)[0] for bi, s_, x in zip(bidx, sl, km)]
        pb = [_dot_nt(x, y) for x, y in zip(lhs, rbt)]
        pk = [_dot_nt(x, y) for x, y in zip(lhs, rkt)]
        minv = _rb_unit_lower_inverse(rb, [jnp.where(rb.strict, -x[:CHUNK], 0.0) for x in pb], INV_HI)
        rv_rhs = [rb.rhs(x, False) for x in vs]
        a_k = [jnp.concatenate([jnp.where(rb.strict, x[:CHUNK], 0.0), jnp.where(rb.causal, x[CHUNK:], 0.0)],
                               axis=0) for x in pk]
        a_kv = [_rb_mul(x, r) for x, r in zip(a_k, rv_rhs)]
        akv = [x[:CHUNK] for x in a_kv]
        arkv = [x[CHUNK:] for x in a_kv]
        a_rb = [jnp.where(rb.causal, x[CHUNK:], 0.0) for x in pb]
        sts = [state[i] for i in range(len(items))]
        init = [_dot_nt(x, _bf(st)) for x, st in zip(lhs, sts)]
        u = [_rb_mul(m, rb.rhs(x[:CHUNK] + y, INV_HI)) for m, x, y in zip(minv, init, akv)]
        y = [x[CHUNK:] + _rb_mul(p, rb.rhs(q, False)) + w for x, p, q, w in zip(init, a_rb, u, arkv)]
        tails = [_bf(jnp.concatenate([x * etail[bi][:, s_], k_ * etail[bi][:, s_]], axis=0))
                 for bi, s_, x, k_ in zip(bidx, sl, b, km)]
        upd = [_dot_tn(_bf(jnp.concatenate([x, v_], axis=0)), t) for x, v_, t in zip(u, vs, tails)]
        for i, (bi, s_) in enumerate(zip(bidx, sl)):
            state[i] = sts[i] * dec[bi][:, s_] + upd[i] * segf
        mean = [x * inv_hd for x in _stacked_sel_r(y, seg)]
        yc = [x - m for x, m in zip(y, mean)]
        var = [x * inv_hd for x in _stacked_sel_r([x * x for x in yc], seg)]
        for i, (bi, s_) in enumerate(zip(bidx, sl)):
            yn = yc[i] * lax.rsqrt(var[i] + RWKV_LN_EPS) * lng_ref[:, s_] + lnb_ref[:, s_]
            o_ref[bi, rows, s_] = ((yn + bon[i] * vs[i]) * gate[bi][:, s_]).astype(o_ref.dtype)
        return 0

    lax.fori_loop(0, tile // CHUNK, chunk_body, 0)


def _pad_rows(w, rows):
    return jnp.zeros((rows,) + w.shape[1:], w.dtype).at[:w.shape[0]].set(w)


def _rwkv_mu_layout(mu):
    z = jnp.zeros((LANES - RWKV_DECAY_LORA,), mu.dtype)
    o = 3 * BRANCH_WIDTH
    return jnp.concatenate([mu[:o], mu[o:o + RWKV_DECAY_LORA], z,
                            mu[o + RWKV_DECAY_LORA:o + RWKV_DECAY_LORA + RWKV_ICLR_LORA], z,
                            mu[o + RWKV_DECAY_LORA + RWKV_ICLR_LORA:]])


def _rwkv_branch(proj, mu, w0, w_up, a0, a_up, g_up, k_k, k_a, r_k, ln_g, ln_b, *, batch, seq, tile):
    bw = BRANCH_WIDTH
    row = lambda x: x.reshape(1, -1).astype(F32)
    const = lambda s: (0, 0)
    vec = pl.BlockSpec((1, bw), const)
    n_state = batch * (bw // RWKV_GROUP_W)
    out = pl.pallas_call(
        functools.partial(_rwkv_kernel, tile=tile, nb=batch),
        grid=(seq // tile,),
        in_specs=[pl.BlockSpec((batch, tile, RWKV_WIDTH), lambda s: (0, s, 0)),
                  pl.BlockSpec((1, RWKV_WIDTH), const),
                  vec, pl.BlockSpec((LANES, bw), const),
                  vec, pl.BlockSpec((LANES, bw), const),
                  pl.BlockSpec((LANES, bw), const),
                  vec, vec, vec, vec, vec],
        out_specs=pl.BlockSpec((batch, tile, bw), lambda s: (0, s, 0)),
        out_shape=jax.ShapeDtypeStruct((batch, seq, bw), BF16),
        scratch_shapes=[pltpu.VMEM((batch, tile + SUBLANES, RWKV_WIDTH), F32),
                        pltpu.VMEM((batch, SUBLANES, RWKV_WIDTH), F32),
                        pltpu.VMEM((batch, tile, RWKV_WIDTH), F32),
                        pltpu.VMEM((n_state, RWKV_GROUP_W, RWKV_GROUP_W), F32)],
        compiler_params=_cparams(("arbitrary",)),
        name="rwkv_branch",
    )(proj.reshape(batch, seq, RWKV_WIDTH), row(_rwkv_mu_layout(mu)), row(w0), _bf(_pad_rows(w_up, LANES)),
      row(a0), _bf(_pad_rows(a_up, LANES)), _bf(g_up), row(k_k), row(k_a), row(r_k), row(ln_g), row(ln_b))
    return out.reshape(batch * seq, bw)


def _pool_kernel(u_ref, pw_ref, ps_ref, o_ref, xbuf, sbuf, carry, *, tile):
    s = pl.program_id(1)

    @pl.when(s == 0)
    def _():
        carry[...] = jnp.zeros_like(carry)

    hist = POOL_MAX_WINDOW
    n = tile + hist
    xbuf[0:hist, :] = carry[...]
    xbuf[hist:n, :] = u_ref[...]
    carry[...] = xbuf[tile:n, :]
    pos = s * tile + _iota2(tile, LANES, 0)
    gw = LANES
    for gi, win in enumerate(POOL_WINDOWS):
        sl = slice(gi * gw, (gi + 1) * gw)
        sbuf[...] = xbuf[:, sl]
        span = 1
        while span < win:
            sbuf[span:n, :] = sbuf[span:n, :] + sbuf[0:n - span, :]
            span *= 2
        x = xbuf[hist:n, sl]
        count = jnp.minimum(pos + 1, win).astype(F32)
        pooled = sbuf[hist:n, :] / count - x
        y = _dot(_bf(pooled), pw_ref[gi])
        o_ref[:, sl] = (y * ps_ref[:, sl]).astype(o_ref.dtype)


def _pool_branch(u, pool_w, pool_scale, *, batch, seq, tile):
    ns = seq // tile
    bw = BRANCH_WIDTH
    return pl.pallas_call(
        functools.partial(_pool_kernel, tile=tile),
        grid=(batch, ns),
        in_specs=[pl.BlockSpec((tile, bw), lambda b, s: (b * ns + s, 0)),
                  pl.BlockSpec((len(POOL_WINDOWS), LANES, LANES), lambda b, s: (0, 0, 0)),
                  pl.BlockSpec((1, bw), lambda b, s: (0, 0))],
        out_specs=pl.BlockSpec((tile, bw), lambda b, s: (b * ns + s, 0)),
        out_shape=jax.ShapeDtypeStruct((batch * seq, bw), BF16),
        scratch_shapes=[pltpu.VMEM((tile + POOL_MAX_WINDOW, bw), F32),
                        pltpu.VMEM((tile + POOL_MAX_WINDOW, LANES), F32),
                        pltpu.VMEM((POOL_MAX_WINDOW, bw), F32)],
        compiler_params=_cparams(("parallel", "arbitrary")),
        name="pool_branch",
    )(u, _bf(pool_w), pool_scale.reshape(1, bw))


GLA_QW = GLA_HEADS * GLA_KEY_DIM
GLA_Q_OFF = 0
GLA_K_OFF = GLA_QW
GLA_V_OFF = 2 * GLA_QW
GLA_G_OFF = GLA_V_OFF + BRANCH_WIDTH
GLA_F_OFF = GLA_G_OFF + BRANCH_WIDTH
GLA_WIDTH = GLA_F_OFF + LANES


def _gla_kernel(p_ref, fup_ref, fb_ref, ng_ref, o_ref, state, *, tile, nb):
    s = pl.program_id(0)

    @pl.when(s == 0)
    def _():
        state[...] = jnp.zeros_like(state)

    tri = _tri_inclusive_bf16(CHUNK)
    rb = _RowBlock(GLA_HEADS)
    vh = _iota2(BRANCH_WIDTH, GLA_QW, 0) >> (GLA_VAL_DIM.bit_length() - 1)
    kh = _iota2(BRANCH_WIDTH, GLA_QW, 1) >> (GLA_KEY_DIM.bit_length() - 1)
    same_head = (vh == kh).astype(F32)
    ng = ng_ref[...]
    dv = GLA_VAL_DIM
    bs = range(nb)

    def chunk_body(ci, _):
        r0 = pl.multiple_of(ci * CHUNK, CHUNK)
        rows = pl.ds(r0, CHUNK)
        fl = jnp.concatenate([p_ref[bi, rows, GLA_F_OFF:GLA_F_OFF + LANES] for bi in bs], axis=0)
        logits = _dot(_bf(fl), fup_ref[...]) + fb_ref[...]
        log_f = -_softplus(-logits) / GLA_GATE_NORM
        gc = [_dot_sel_l(tri, log_f[bi * CHUNK:(bi + 1) * CHUNK]) for bi in bs]
        q = [p_ref[bi, rows, GLA_Q_OFF:GLA_Q_OFF + GLA_QW] * (GLA_KEY_DIM ** -0.5) for bi in bs]
        k = [p_ref[bi, rows, GLA_K_OFF:GLA_K_OFF + GLA_QW] for bi in bs]
        v = [p_ref[bi, rows, GLA_V_OFF:GLA_V_OFF + BRANCH_WIDTH] for bi in bs]
        q_dec = [_bf(x * jnp.exp(g)) for x, g in zip(q, gc)]
        k_dec = [x * jnp.exp(-g) for x, g in zip(k, gc)]
        glast = [g[CHUNK - 1:CHUNK, :] for g in gc]
        k_tail = [_bf(x * jnp.exp(gl - g)) for x, gl, g in zip(k, glast, gc)]
        attn = [_dot_nt(x, rb.rhs(y, False)[0]) for x, y in zip(q_dec, k_dec)]
        intra = [_rb_mul(jnp.where(rb.causal, a, 0.0), rb.rhs(x, False)) for a, x in zip(attn, v)]
        sts = [state[bi] for bi in bs]
        inter = [_dot_nt(x, _bf(st)) for x, st in zip(q_dec, sts)]
        upd = [_dot_tn(_bf(x), y) for x, y in zip(v, k_tail)]
        for bi in bs:
            state[bi] = sts[bi] * jnp.exp(glast[bi]) + upd[bi] * same_head
            o = intra[bi] + inter[bi]
            for h in range(GLA_HEADS):
                oh = o[:, h * dv:(h + 1) * dv]
                gate = p_ref[bi, rows, GLA_G_OFF + h * dv:GLA_G_OFF + (h + 1) * dv]
                on = (oh * lax.rsqrt(jnp.mean(oh * oh, axis=-1, keepdims=True) + NORM_EPS)) * ng
                o_ref[bi, rows, h * dv:(h + 1) * dv] = (on * _silu(gate)).astype(o_ref.dtype)
        return 0

    lax.fori_loop(0, tile // CHUNK, chunk_body, 0)


def _gla_branch(proj, f_up, f_bias, norm_g, *, batch, seq, tile):
    const = lambda s: (0, 0)
    out = pl.pallas_call(
        functools.partial(_gla_kernel, tile=tile, nb=batch),
        grid=(seq // tile,),
        in_specs=[pl.BlockSpec((batch, tile, GLA_WIDTH), lambda s: (0, s, 0)),
                  pl.BlockSpec((LANES, GLA_QW), const),
                  pl.BlockSpec((1, GLA_QW), const),
                  pl.BlockSpec((1, GLA_VAL_DIM), const)],
        out_specs=pl.BlockSpec((batch, tile, BRANCH_WIDTH), lambda s: (0, s, 0)),
        out_shape=jax.ShapeDtypeStruct((batch, seq, BRANCH_WIDTH), BF16),
        scratch_shapes=[pltpu.VMEM((batch, BRANCH_WIDTH, GLA_QW), F32)],
        compiler_params=_cparams(("arbitrary",)),
        name="gla_branch",
    )(proj.reshape(batch, seq, GLA_WIDTH), _bf(_pad_rows(f_up, LANES)), f_bias.reshape(1, GLA_QW),
      norm_g.reshape(1, GLA_VAL_DIM))
    return out.reshape(batch * seq, BRANCH_WIDTH)


def _merge_kernel(hn_ref, y0, y1, y2, y3, bp_ref, wg0, wg1, wg2, wg3, gb_ref, o_ref):
    ys = (y0, y1, y2, y3)
    wgs = (wg0, wg1, wg2, wg3)
    hn = hn_ref[...]
    acc = None
    for i in range(N_BRANCH):
        logit = _dot(hn, wgs[i][...]) + gb_ref[i]
        gate = 0.5 * (jnp.tanh(0.5 * logit) + 1.0)
        term = gate * _dot(ys[i][...], bp_ref[i])
        acc = term if acc is None else acc + term
    o_ref[...] = acc.astype(o_ref.dtype)


def _merge(hn, ys, branch_proj, w_gate, gate_bias, *, tm, tn):
    t, dm = hn.shape
    d = branch_proj.shape[-1]
    nj = d // tn
    y_spec = pl.BlockSpec((tm, BRANCH_WIDTH), lambda j, i: (i, 0))
    wg_specs = [pl.BlockSpec((dm, tn), functools.partial(lambda j, i, b: (0, b * nj + j), b=b))
                for b in range(N_BRANCH)]
    return pl.pallas_call(
        _merge_kernel,
        grid=(nj, t // tm),
        in_specs=[pl.BlockSpec((tm, dm), lambda j, i: (i, 0))]
        + [y_spec] * N_BRANCH
        + [pl.BlockSpec((N_BRANCH, BRANCH_WIDTH, tn), lambda j, i: (0, 0, j))]
        + wg_specs
        + [pl.BlockSpec((N_BRANCH, 1, tn), lambda j, i: (0, 0, j))],
        out_specs=pl.BlockSpec((tm, tn), lambda j, i: (i, j)),
        out_shape=jax.ShapeDtypeStruct((t, d), BF16),
        compiler_params=_cparams(("parallel", "parallel")),
        name="merge",
    )(hn, *ys, _bf(branch_proj), w_gate, w_gate, w_gate, w_gate, gate_bias.reshape(N_BRANCH, 1, d))


def _rms_rows(x, g):
    return (x * lax.rsqrt(jnp.mean(x * x, axis=-1, keepdims=True) + NORM_EPS)) * g


def _ffn_kernel(h_ref, g_ref, gn_ref, w1_ref, w3_ref, w2_ref, o_ref, on_ref, xn_ref, acc_ref, *, nf):
    f = pl.program_id(1)

    @pl.when(f == 0)
    def _():
        acc_ref[...] = jnp.zeros_like(acc_ref)
        xn_ref[...] = _bf(_rms_rows(h_ref[...], g_ref[...]))

    x = xn_ref[...]
    mid = _bf(_silu(_dot(x, w1_ref[...])) * _dot(x, w3_ref[...]))
    acc_ref[...] += _dot(mid, w2_ref[...])

    @pl.when(f == nf - 1)
    def _():
        out = h_ref[...] + acc_ref[...]
        o_ref[...] = out
        on_ref[...] = _rms_rows(out, gn_ref[...]).astype(on_ref.dtype)


def _ffn(h, g, next_g, next_dtype, w1, w3, w2, *, tm, tf):
    t, d = h.shape
    ff = w1.shape[1]
    nf = ff // tf
    row = pl.BlockSpec((tm, d), lambda i, f: (i, 0))
    vec = pl.BlockSpec((1, d), lambda i, f: (0, 0))
    return pl.pallas_call(
        functools.partial(_ffn_kernel, nf=nf),
        grid=(t // tm, nf),
        in_specs=[row, vec, vec,
                  pl.BlockSpec((d, tf), lambda i, f: (0, f)),
                  pl.BlockSpec((d, tf), lambda i, f: (0, f)),
                  pl.BlockSpec((tf, d), lambda i, f: (f, 0))],
        out_specs=[row, row],
        out_shape=[jax.ShapeDtypeStruct((t, d), F32), jax.ShapeDtypeStruct((t, d), next_dtype)],
        scratch_shapes=[pltpu.VMEM((tm, d), BF16), pltpu.VMEM((tm, d), F32)],
        compiler_params=_cparams(("parallel", "arbitrary")),
        name="ffn",
    )(h, g.reshape(1, d), next_g.reshape(1, d), w1, w3, w2)


ROUTE_E0, ROUTE_E1, ROUTE_RANK0, ROUTE_RANK1 = 0, 1, 2, 3
ROUTE_W0, ROUTE_W1 = 0, 1


def _router_kernel(h_ref, g_ref, rw_ref, ri_ref, rf_ref, cnt_ref, run):
    i = pl.program_id(0)

    @pl.when(i == 0)
    def _():
        run[...] = jnp.zeros_like(run)

    tm = h_ref.shape[0]
    hn = _bf(_rms_rows(h_ref[...], g_ref[...]))
    logits = _dot(hn, rw_ref[...])
    lane = _iota2(tm, LANES, 1)
    neg = jnp.float32(-jnp.inf)
    lg = jnp.where(lane < N_EXPERTS, logits, neg)
    m1 = jnp.max(lg, axis=-1, keepdims=True)
    e0 = jnp.min(jnp.where(lg == m1, lane, LANES), axis=-1, keepdims=True)
    lg2 = jnp.where(lane == e0, neg, lg)
    m2 = jnp.max(lg2, axis=-1, keepdims=True)
    e1 = jnp.min(jnp.where(lg2 == m2, lane, LANES), axis=-1, keepdims=True)
    ex = jnp.exp(m2 - m1)
    den = 1.0 + ex
    w0 = 1.0 / den
    w1 = ex / den
    hit0 = lane == e0
    hit1 = lane == e1
    onehot = (hit0 | hit1).astype(F32)
    rr = _iota2(tm, tm, 0)
    cc = _iota2(tm, tm, 1)
    before = _dot((cc < rr).astype(BF16), _bf(onehot)) + run[...]
    rank0 = jnp.sum(jnp.where(hit0, before, 0.0), axis=-1, keepdims=True).astype(I32)
    rank1 = jnp.sum(jnp.where(hit1, before, 0.0), axis=-1, keepdims=True).astype(I32)
    run[...] += jnp.sum(onehot, axis=0, keepdims=True)
    cnt_ref[...] = run[...]
    ri_ref[...] = jnp.where(lane == ROUTE_E0, e0,
                            jnp.where(lane == ROUTE_E1, e1,
                                      jnp.where(lane == ROUTE_RANK0, rank0,
                                                jnp.where(lane == ROUTE_RANK1, rank1, 0))))
    rf_ref[...] = jnp.where(lane == ROUTE_W0, w0, jnp.where(lane == ROUTE_W1, w1, 0.0))


def _router(h, g, router_w, *, tm=TOKEN_TILE):
    t, d = h.shape
    rw = _bf(jnp.zeros((d, LANES), F32).at[:, :N_EXPERTS].set(router_w))
    return pl.pallas_call(
        _router_kernel,
        grid=(t // tm,),
        in_specs=[pl.BlockSpec((tm, d), lambda i: (i, 0)),
                  pl.BlockSpec((1, d), lambda i: (0, 0)),
                  pl.BlockSpec((d, LANES), lambda i: (0, 0))],
        out_specs=[pl.BlockSpec((tm, LANES), lambda i: (i, 0)),
                   pl.BlockSpec((tm, LANES), lambda i: (i, 0)),
                   pl.BlockSpec((1, LANES), lambda i: (0, 0))],
        out_shape=[jax.ShapeDtypeStruct((t, LANES), I32),
                   jax.ShapeDtypeStruct((t, LANES), F32),
                   jax.ShapeDtypeStruct((1, LANES), F32)],
        scratch_shapes=[pltpu.VMEM((1, LANES), F32)],
        compiler_params=_cparams(("arbitrary",)),
        name="moe_router",
    )(h, g.reshape(1, d), rw)


def _row_copy(src_ref, src_row, dst_ref, dst_row, sem):
    return pltpu.make_async_copy(src_ref.at[pl.ds(src_row, 1)], dst_ref.at[pl.ds(dst_row, 1)], sem)


DMA_ISSUE_UNROLL = 8


def _dispatch_kernel(d0_ref, d1_ref, zs_ref, zf_ref, x_ref, xb_ref, zbuf, sem, zsem):
    tm = x_ref.shape[0]
    step = pl.program_id(0)
    base = step * tm

    @pl.when(step == 0)
    def _():
        zbuf[...] = jnp.zeros_like(zbuf)

        def zero_copy(j):
            first = pl.multiple_of(zs_ref[j], MOE_BLOCK)
            return pltpu.make_async_copy(zbuf, xb_ref.at[pl.ds(first, MOE_BLOCK)], zsem)

        for j in range(2 * N_EXPERTS):
            @pl.when(zf_ref[j] > 0)
            def _():
                zero_copy(j).start()
        for j in range(2 * N_EXPERTS):
            @pl.when(zf_ref[j] > 0)
            def _():
                zero_copy(j).wait()

    def copies(rw):
        return (_row_copy(x_ref, rw, xb_ref, d0_ref[base + rw], sem),
                _row_copy(x_ref, rw, xb_ref, d1_ref[base + rw], sem))

    def start(rw, _):
        for cp in copies(rw):
            cp.start()
        return 0

    def wait(rw, _):
        for cp in copies(rw):
            cp.wait()
        return 0

    lax.fori_loop(0, tm, start, 0, unroll=DMA_ISSUE_UNROLL)
    lax.fori_loop(0, tm, wait, 0, unroll=DMA_ISSUE_UNROLL)


def _dispatch(x, dest0, dest1, zero_start, zero_flag, n_rows, *, tm=TOKEN_TILE):
    t, d = x.shape
    return pl.pallas_call(
        _dispatch_kernel,
        grid_spec=pltpu.PrefetchScalarGridSpec(
            num_scalar_prefetch=4,
            grid=(t // tm,),
            in_specs=[pl.BlockSpec((tm, d), lambda i, *_: (i, 0))],
            out_specs=pl.BlockSpec(memory_space=pl.ANY),
            scratch_shapes=[pltpu.VMEM((MOE_BLOCK, d), F32),
                            pltpu.SemaphoreType.DMA(()), pltpu.SemaphoreType.DMA(())]),
        out_shape=jax.ShapeDtypeStruct((n_rows, d), F32),
        compiler_params=_cparams(("arbitrary",)),
        name="moe_dispatch",
    )(dest0, dest1, zero_start, zero_flag, x)


MOE_UNIT_BLOCKS = 2


def _moe_ffn_kernel(ue_ref, ub_ref, uv_ref, zs_ref, zf_ref, *refs, nf):
    del ue_ref
    nu = MOE_UNIT_BLOCKS
    x_refs = refs[:nu]
    g_ref, w1_ref, w3_ref, w2_ref, yb_ref = refs[nu:nu + 5]
    xns = refs[nu + 5:2 * nu + 5]
    accs = refs[2 * nu + 5:3 * nu + 5]
    wb1, wb3, wb2, sem = refs[3 * nu + 5:3 * nu + 9]
    u = pl.program_id(0)
    f = pl.program_id(1)
    slot = [u * nu + j for j in range(nu)]

    @pl.when((u == 0) & (f == 0))
    def _():
        zsrc = accs[nu - 1]
        zsrc[...] = jnp.zeros_like(zsrc)

        def zero_copy(j):
            first = pl.multiple_of(zs_ref[j], MOE_BLOCK)
            return pltpu.make_async_copy(zsrc, yb_ref.at[pl.ds(first, MOE_BLOCK)], sem)

        for j in range(N_EXPERTS):
            @pl.when(zf_ref[j] > 0)
            def _():
                zero_copy(j).start()
        for j in range(N_EXPERTS):
            @pl.when(zf_ref[j] > 0)
            def _():
                zero_copy(j).wait()

    @pl.when(f == 0)
    def _():
        for j in range(nu):
            @pl.when(uv_ref[slot[j]] > 0)
            def _():
                accs[j][...] = jnp.zeros_like(accs[j])
                xns[j][...] = _bf(_rms_rows(x_refs[j][...], g_ref[...]))

    for j in range(nu):
        @pl.when(uv_ref[slot[j]] > 0)
        def _():
            x = xns[j][...]
            if j == 0:
                w1, w3, w2 = _bf(w1_ref[0]), _bf(w3_ref[0]), _bf(w2_ref[0])
                if nu > 1:
                    wb1[...] = w1
                    wb3[...] = w3
                    wb2[...] = w2
            else:
                w1, w3, w2 = wb1[...], wb3[...], wb2[...]
            mid = _bf(_silu(_dot(x, w1)) * _dot(x, w3))
            accs[j][...] += _dot(mid, w2)

    @pl.when(f == nf - 1)
    def _():
        def out_copy(j):
            first = pl.multiple_of(ub_ref[slot[j]] * MOE_BLOCK, MOE_BLOCK)
            return pltpu.make_async_copy(accs[j], yb_ref.at[pl.ds(first, MOE_BLOCK)], sem)

        for j in range(nu):
            @pl.when(uv_ref[slot[j]] > 0)
            def _():
                out_copy(j).start()
        for j in range(nu):
            @pl.when(uv_ref[slot[j]] > 0)
            def _():
                out_copy(j).wait()


def _moe_ffn(xb, g, w1, w3, w2, unit_e, unit_b, unit_v, spare_start, spare_flag, *, tf):
    n_rows, d = xb.shape
    ff = w1.shape[-1]
    nf = ff // tf
    nu = MOE_UNIT_BLOCKS
    n_units = unit_e.shape[0]
    wcol = lambda u, f, ue, ub, uv, *_: (ue[u], 0, f * uv[u * nu])
    wrow = lambda u, f, ue, ub, uv, *_: (ue[u], f * uv[u * nu], 0)
    x_specs = [pl.BlockSpec((MOE_BLOCK, d), functools.partial(lambda u, f, ue, ub, *_, j: (ub[u * nu + j], 0), j=j),
                            pipeline_mode=pl.Buffered(1)) for j in range(nu)]
    return pl.pallas_call(
        functools.partial(_moe_ffn_kernel, nf=nf),
        grid_spec=pltpu.PrefetchScalarGridSpec(
            num_scalar_prefetch=5,
            grid=(n_units, nf),
            in_specs=x_specs + [pl.BlockSpec((1, d), lambda u, f, *_: (0, 0)),
                                pl.BlockSpec((1, d, tf), wcol),
                                pl.BlockSpec((1, d, tf), wcol),
                                pl.BlockSpec((1, tf, d), wrow)],
            out_specs=pl.BlockSpec(memory_space=pl.ANY),
            scratch_shapes=[pltpu.VMEM((MOE_BLOCK, d), BF16)] * nu + [pltpu.VMEM((MOE_BLOCK, d), F32)] * nu
            + [pltpu.VMEM((d, tf), BF16), pltpu.VMEM((d, tf), BF16), pltpu.VMEM((tf, d), BF16),
               pltpu.SemaphoreType.DMA(())]),
        out_shape=jax.ShapeDtypeStruct((n_rows, d), F32),
        compiler_params=_cparams(("arbitrary", "arbitrary")),
        name="moe_experts",
    )(unit_e, unit_b, unit_v, spare_start, spare_flag, *([xb] * nu), g.reshape(1, d), w1, w3, w2)


COMBINE_SLOTS = 2


def _combine_kernel(d0_ref, d1_ref, yb_ref, h_ref, rf_ref, g_ref, o_ref, buf, sems, *, final_norm):
    tm = h_ref.shape[0]
    i = pl.program_id(0)
    n = pl.num_programs(0)
    slot = lax.rem(i, COMBINE_SLOTS)

    def copies(step, sl, rw):
        base = step * tm
        return (_row_copy(yb_ref, d0_ref[base + rw], buf.at[sl, 0], rw, sems.at[sl]),
                _row_copy(yb_ref, d1_ref[base + rw], buf.at[sl, 1], rw, sems.at[sl]))

    def start_all(step, sl):
        def body(rw, _):
            for cp in copies(step, sl, rw):
                cp.start()
            return 0
        lax.fori_loop(0, tm, body, 0, unroll=DMA_ISSUE_UNROLL)

    def wait_all(step, sl):
        def body(rw, _):
            for cp in copies(step, sl, rw):
                cp.wait()
            return 0
        lax.fori_loop(0, tm, body, 0, unroll=DMA_ISSUE_UNROLL)

    @pl.when(i == 0)
    def _():
        start_all(i, slot)

    @pl.when(i + 1 < n)
    def _():
        start_all(i + 1, 1 - slot)

    wait_all(i, slot)
    w0 = rf_ref[:, ROUTE_W0:ROUTE_W0 + 1]
    w1 = rf_ref[:, ROUTE_W1:ROUTE_W1 + 1]
    out = h_ref[...] + (buf[slot, 0] * w0 + buf[slot, 1] * w1)
    o_ref[...] = _rms_rows(out, g_ref[...]) if final_norm else out


def _combine(yb, h, route_f, dest0, dest1, final_g, *, tm=COMBINE_TILE):
    t, d = h.shape
    g = jnp.ones((1, d), F32) if final_g is None else final_g.reshape(1, d)
    return pl.pallas_call(
        functools.partial(_combine_kernel, final_norm=final_g is not None),
        grid_spec=pltpu.PrefetchScalarGridSpec(
            num_scalar_prefetch=2,
            grid=(t // tm,),
            in_specs=[pl.BlockSpec(memory_space=pl.ANY),
                      pl.BlockSpec((tm, d), lambda i, d0, d1: (i, 0)),
                      pl.BlockSpec((tm, LANES), lambda i, d0, d1: (i, 0)),
                      pl.BlockSpec((1, d), lambda i, d0, d1: (0, 0))],
            out_specs=pl.BlockSpec((tm, d), lambda i, d0, d1: (i, 0)),
            scratch_shapes=[pltpu.VMEM((COMBINE_SLOTS, 2, tm, d), F32),
                            pltpu.SemaphoreType.DMA((COMBINE_SLOTS,))]),
        out_shape=jax.ShapeDtypeStruct((t, d), F32),
        compiler_params=_cparams(("arbitrary",)),
        name="moe_combine",
    )(dest0, dest1, yb, h, route_f, g)


def _moe(h, g, router_w, w1, w3, w2, final_g=None):
    t, d = h.shape
    route_i, route_f, counts_f = _router(h, g, router_w)
    counts = counts_f[0, :N_EXPERTS].astype(I32)
    padded = (counts + MOE_BLOCK - 1) // MOE_BLOCK * MOE_BLOCK
    pad_end = jnp.cumsum(padded)
    pad_start = pad_end - padded
    n_rows = (-(-(t * 2) // MOE_BLOCK) + N_EXPERTS) * MOE_BLOCK
    n_blocks = n_rows // MOE_BLOCK
    nu = MOE_UNIT_BLOCKS
    blocks_e = padded // MOE_BLOCK
    units_e = (blocks_e + nu - 1) // nu
    unit_end = jnp.cumsum(units_e)
    n_units = -(-n_blocks // nu) + N_EXPERTS
    uidx = jnp.arange(n_units, dtype=I32)
    unit_e = jnp.minimum(jnp.sum((uidx[:, None] >= unit_end[None, :]).astype(I32), axis=1), N_EXPERTS - 1)
    within = uidx - (unit_end - units_e)[unit_e]
    left = jnp.where(uidx < unit_end[N_EXPERTS - 1], blocks_e[unit_e] - nu * within, 0)
    first_blk = pad_start[unit_e] // MOE_BLOCK + nu * within
    slot_j = jnp.arange(nu, dtype=I32)[None, :]
    unit_v = (slot_j < left[:, None]).astype(I32)
    unit_b = jnp.where(unit_v > 0, first_blk[:, None] + slot_j, jnp.where(left > 0, first_blk, 0)[:, None])
    unit_v = unit_v.reshape(-1)
    unit_b = unit_b.reshape(-1).astype(I32)
    dest0 = pad_start[route_i[:, ROUTE_E0]] + route_i[:, ROUTE_RANK0]
    dest1 = pad_start[route_i[:, ROUTE_E1]] + route_i[:, ROUTE_RANK1]
    spare = pad_end[N_EXPERTS - 1] // MOE_BLOCK + jnp.arange(N_EXPERTS, dtype=I32)
    zero_start = jnp.concatenate([pad_end - MOE_BLOCK, jnp.minimum(spare, n_blocks - 1) * MOE_BLOCK])
    zero_flag = jnp.concatenate([padded > 0, spare < n_blocks]).astype(I32)
    xb = _dispatch(h, dest0, dest1, zero_start.astype(I32), zero_flag, n_rows)
    yb = _moe_ffn(xb, g, w1, w3, w2, unit_e, unit_b, unit_v,
                  zero_start[N_EXPERTS:].astype(I32), zero_flag[N_EXPERTS:], tf=MOE_HIDDEN_TILE)
    return _combine(yb, h, route_f, dest0, dest1, final_g)


def _pad_cols(w, width):
    return jnp.concatenate([w, jnp.zeros((w.shape[0], width - w.shape[1]), w.dtype)], axis=1)


def _split_w_in(w_in):
    bw = BRANCH_WIDTH
    d = w_in.shape[0]
    widths = (3 * bw, bw, GDN_HEADS, GDN_HEADS, 3 * bw + RWKV_DECAY_LORA + RWKV_ICLR_LORA + RWKV_GATE_LORA,
              bw, GLA_QW, GLA_QW, bw, bw, GLA_GATE_RANK, N_BRANCH * d)
    offs = [0]
    for w in widths:
        offs.append(offs[-1] + w)
    col = lambda i: w_in[:, offs[i]:offs[i + 1]]
    gdn = jnp.concatenate([col(0), col(1), _pad_cols(jnp.concatenate([col(2), col(3)], axis=1), LANES)], axis=1)
    rw = col(4)
    o = 3 * bw
    rwkv = jnp.concatenate([rw[:, :o],
                            _pad_cols(rw[:, o:o + RWKV_DECAY_LORA], LANES),
                            _pad_cols(rw[:, o + RWKV_DECAY_LORA:o + RWKV_DECAY_LORA + RWKV_ICLR_LORA], LANES),
                            rw[:, o + RWKV_DECAY_LORA + RWKV_ICLR_LORA:]], axis=1)
    pool = col(5)
    gla = jnp.concatenate([col(6), col(7), col(8), col(9), _pad_cols(col(10), LANES)], axis=1)
    gate = col(11)
    return tuple(_bf(w) for w in (gdn, rwkv, pool, gla, gate))


def _mixer(hn, h, layer, batch, seq, w_in, gdn_conv_w, gdn_a_log, gdn_dt_bias, gdn_norm_g,
           rwkv_mu, rwkv_w0, rwkv_w_up, rwkv_a0, rwkv_a_up, rwkv_g_up, rwkv_k_k, rwkv_k_a,
           rwkv_r_k, rwkv_ln_g, rwkv_ln_b, pool_w, pool_scale, gla_f_up, gla_f_bias, gla_norm_g,
           gate_bias, branch_proj, w_out):
    l = layer
    w_gdn, w_rwkv, w_pool, w_gla, w_gate = _split_w_in(w_in[l])
    tm = TOKEN_TILE
    p_gdn = _matmul(hn, w_gdn, tm=tm, tn=GDN_WIDTH, name="proj_gdn")
    p_rwkv = _matmul(hn, w_rwkv, tm=tm, tn=RWKV_WIDTH, name="proj_rwkv")
    p_pool = _matmul(hn, w_pool, tm=tm, tn=BRANCH_WIDTH, name="proj_pool")
    p_gla = _matmul(hn, w_gla, tm=tm, tn=GLA_WIDTH, name="proj_gla")
    kw = dict(batch=batch, seq=seq, tile=SEQ_TILE)
    y_gdn = _gdn_branch(p_gdn, gdn_conv_w[l], gdn_a_log[l], gdn_dt_bias[l], gdn_norm_g[l], **kw)
    y_rwkv = _rwkv_branch(p_rwkv, rwkv_mu[l], rwkv_w0[l], rwkv_w_up[l], rwkv_a0[l], rwkv_a_up[l],
                          rwkv_g_up[l], rwkv_k_k[l], rwkv_k_a[l], rwkv_r_k[l].reshape(-1),
                          rwkv_ln_g[l], rwkv_ln_b[l], **kw)
    y_pool = _pool_branch(p_pool, pool_w[l], pool_scale[l], **kw)
    y_gla = _gla_branch(p_gla, gla_f_up[l], gla_f_bias[l], gla_norm_g[l], **kw)
    mixed = _merge(hn, (y_gdn, y_rwkv, y_pool, y_gla), branch_proj[l], w_gate, gate_bias[l],
                   tm=TOKEN_TILE, tn=MERGE_COL_TILE)
    return _matmul(mixed, _bf(w_out[l]), tm=TOKEN_TILE, tn=OUT_PROJ_COL_TILE, residual=h, name="out_proj")


def kernel(x, norm1_g, w_in, gdn_conv_w, gdn_a_log, gdn_dt_bias, gdn_norm_g, rwkv_mu, rwkv_w0, rwkv_w_up, rwkv_a0, rwkv_a_up, rwkv_g_up, rwkv_k_k, rwkv_k_a, rwkv_r_k, rwkv_ln_g, rwkv_ln_b, pool_w, pool_scale, gla_f_up, gla_f_bias, gla_norm_g, gate_bias, branch_proj, w_out, norm2_g, ffn_w1, ffn_w3, ffn_w2, moe_router, moe_w1, moe_w3, moe_w2, final_norm_g):
    batch, seq, d = x.shape
    depth = norm1_g.shape[0]
    h = x.reshape(batch * seq, d)
    hn = None
    for layer in range(depth):
        if hn is None:
            hn = _rmsnorm(h, norm1_g[layer], BF16)
        h = _mixer(hn, h, layer, batch, seq, w_in, gdn_conv_w, gdn_a_log, gdn_dt_bias, gdn_norm_g,
                   rwkv_mu, rwkv_w0, rwkv_w_up, rwkv_a0, rwkv_a_up, rwkv_g_up, rwkv_k_k, rwkv_k_a,
                   rwkv_r_k, rwkv_ln_g, rwkv_ln_b, pool_w, pool_scale, gla_f_up, gla_f_bias,
                   gla_norm_g, gate_bias, branch_proj, w_out)
        hn = None
        i = layer // 2
        last = layer == depth - 1
        if layer % 2 == 0:
            next_g, next_dtype = (final_norm_g, F32) if last else (norm1_g[layer + 1], BF16)
            h, nxt = _ffn(h, norm2_g[layer], next_g, next_dtype, _bf(ffn_w1[i]), _bf(ffn_w3[i]), _bf(ffn_w2[i]),
                          tm=TOKEN_TILE, tf=FFN_HIDDEN_TILE)
            if last:
                h = nxt
            else:
                hn = nxt
        else:
            h = _moe(h, norm2_g[layer], moe_router[i], moe_w1[i], moe_w3[i], moe_w2[i],
                     final_g=final_norm_g if last else None)
    return h.reshape(batch, seq, d)
```

```python
import functools

import jax
import jax.numpy as jnp
from jax import lax
from jax.experimental import pallas as pl
from jax.experimental.pallas import tpu as pltpu

F32 = jnp.float32
BF16 = jnp.bfloat16
I32 = jnp.int32

NORM_EPS = 1e-6
CHUNK = 64
N_BRANCH = 4
BRANCH_WIDTH = 512
GDN_HEADS = 4
GDN_HEAD_DIM = 128
GDN_CONV = 4
RWKV_HEAD_DIM = 64
RWKV_DECAY_LORA = 64
RWKV_ICLR_LORA = 64
RWKV_GATE_LORA = 128
RWKV_DECAY_SCALE = 0.606531
RWKV_LN_EPS = 64e-5
POOL_WINDOWS = (2, 4, 8, 16)
POOL_MAX_WINDOW = 16
GLA_HEADS = 4
GLA_KEY_DIM = 64
GLA_VAL_DIM = 128
GLA_GATE_RANK = 16
GLA_GATE_NORM = 16.0
N_EXPERTS = 8
MOE_BLOCK = 512

LANES = 128
SUBLANES = 8
VMEM_LIMIT_BYTES = 56 * 1024 * 1024

INV_BLOCK = 16
INV_HI = False
SEQ_CHUNKS_PER_STEP = 2

TOKEN_TILE = 512
SEQ_TILE = 256
MERGE_COL_TILE = 512
OUT_PROJ_COL_TILE = 1024
FFN_HIDDEN_TILE = 512
MOE_HIDDEN_TILE = 512
COMBINE_TILE = 256


def _cparams(sem):
    return pltpu.CompilerParams(dimension_semantics=sem, vmem_limit_bytes=VMEM_LIMIT_BYTES)


def _dot(a, b):
    return jnp.dot(a, b, preferred_element_type=F32)


def _dot_nt(a, b):
    return lax.dot_general(a, b, (((1,), (1,)), ((), ())), preferred_element_type=F32)


def _dot_tn(a, b):
    return lax.dot_general(a, b, (((0,), (0,)), ((), ())), preferred_element_type=F32)


def _bf(x):
    return x.astype(BF16)


def _split_terms(x, terms):
    out = []
    rem = x
    for i in range(terms):
        xi = _bf(rem)
        out.append(xi)
        if i + 1 < terms:
            rem = rem - xi.astype(F32)
    return out


def _dot_sel_l(sel, x, terms=3):
    parts = [_dot(sel, t) for t in _split_terms(x, terms)]
    return functools.reduce(lambda a, b: a + b, parts)


def _stacked_sel_r(xs, sel, terms=2):
    n = len(xs)
    rows = xs[0].shape[0]
    parts = [_split_terms(x, terms) for x in xs]
    out = _dot(jnp.concatenate([p[t] for t in range(terms) for p in parts], axis=0), sel)
    res = []
    for i in range(n):
        acc = out[i * rows:(i + 1) * rows]
        for t in range(1, terms):
            acc = acc + out[(t * n + i) * rows:(t * n + i + 1) * rows]
        res.append(acc)
    return res


def _sigmoid(x):
    return jax.nn.sigmoid(x)


def _silu(x):
    return x * jax.nn.sigmoid(x)


def _softplus(x):
    return jnp.maximum(x, 0.0) + jnp.log1p(jnp.exp(-jnp.abs(x)))


def _iota2(n, m, dim):
    return lax.broadcasted_iota(I32, (n, m), dim)


def _tri_inclusive_bf16(n):
    r = _iota2(n, n, 0)
    c = _iota2(n, n, 1)
    return (c <= r).astype(BF16)


class _RowBlock:
    def __init__(self, nblk):
        self.nblk = nblk
        n = nblk * CHUNK
        self._shift = CHUNK.bit_length() - 1
        row = _iota2(CHUNK, n, 0)
        colw = _iota2(CHUNK, n, 1) & (CHUNK - 1)
        self.eye = (row == colw).astype(F32)
        self.strict = colw < row
        self.causal = colw <= row
        ish = INV_BLOCK.bit_length() - 1
        self.same_diag_block = (row >> ish) == (colw >> ish)
        self._bd = {}

    def bd_mask(self, w):
        if w not in self._bd:
            n = self.nblk * CHUNK
            rb = _iota2(n, self.nblk * w, 0) >> self._shift
            cb = _iota2(n, self.nblk * w, 1) >> (w.bit_length() - 1)
            self._bd[w] = (rb == cb).astype(BF16)
        return self._bd[w]

    def rhs(self, y, hi):
        mask = self.bd_mask(y.shape[1] // self.nblk)
        yh = _bf(y)
        bdh = jnp.concatenate([yh] * self.nblk, axis=0) * mask
        if not hi:
            return (bdh, None)
        yl = _bf(y - yh.astype(F32))
        return (bdh, jnp.concatenate([yl] * self.nblk, axis=0) * mask)


def _rb_mul(x, rhs):
    bdh, bdl = rhs
    xh = _bf(x)
    out = _dot(xh, bdh)
    if bdl is not None:
        xl = _bf(x - xh.astype(F32))
        out = out + _dot(xh, bdl) + _dot(xl, bdh)
    return out


def _rb_unit_lower_inverse(rb, lows, hi):
    a = [jnp.where(rb.same_diag_block, -l, 0.0) for l in lows]
    off = [jnp.where(rb.same_diag_block, 0.0, l) for l in lows]
    t = [rb.eye + x for x in a]
    p = [_rb_mul(x, rb.rhs(x, hi)) for x in a]
    k = 4
    while k <= INV_BLOCK:
        rp = [rb.rhs(x, hi) for x in p]
        if k < INV_BLOCK:
            both = [_rb_mul(jnp.concatenate([x, y], axis=0), r) for x, y, r in zip(t, p, rp)]
            t = [x + z[:CHUNK] for x, z in zip(t, both)]
            p = [z[CHUNK:] for z in both]
        else:
            t = [x + _rb_mul(x, r) for x, r in zip(t, rp)]
        k *= 2
    roff = [rb.rhs(x, hi) for x in off]
    nm = [_rb_mul(x, r) for x, r in zip(t, roff)]
    acc = [rb.eye - x for x in nm]
    rn = [rb.rhs(x, hi) for x in nm]
    pw = nm
    for j in range(2, CHUNK // INV_BLOCK):
        pw = [_rb_mul(x, r) for x, r in zip(pw, rn)]
        acc = [x + y if j % 2 == 0 else x - y for x, y in zip(acc, pw)]
    rt = [rb.rhs(x, hi) for x in t]
    return [_rb_mul(x, r) for x, r in zip(acc, rt)]


def _rmsnorm_kernel(x_ref, g_ref, o_ref):
    x = x_ref[...]
    ms = jnp.mean(x * x, axis=-1, keepdims=True)
    o_ref[...] = ((x * lax.rsqrt(ms + NORM_EPS)) * g_ref[...]).astype(o_ref.dtype)


def _rmsnorm(x, g, out_dtype, tm=TOKEN_TILE):
    t, d = x.shape
    return pl.pallas_call(
        _rmsnorm_kernel,
        grid=(t // tm,),
        in_specs=[pl.BlockSpec((tm, d), lambda i: (i, 0)),
                  pl.BlockSpec((1, d), lambda i: (0, 0))],
        out_specs=pl.BlockSpec((tm, d), lambda i: (i, 0)),
        out_shape=jax.ShapeDtypeStruct((t, d), out_dtype),
        compiler_params=_cparams(("parallel",)),
        name="rmsnorm",
    )(x, g.reshape(1, d))


def _mm_kernel(a_ref, b_ref, o_ref):
    o_ref[...] = _dot(a_ref[...], b_ref[...]).astype(o_ref.dtype)


def _mm_res_kernel(a_ref, b_ref, r_ref, o_ref):
    o_ref[...] = (r_ref[...] + _dot(a_ref[...], b_ref[...])).astype(o_ref.dtype)


def _matmul(a, b, *, tm, tn, out_dtype=F32, residual=None, name="matmul"):
    m, k = a.shape
    n = b.shape[1]
    grid = (n // tn, m // tm)
    in_specs = [pl.BlockSpec((tm, k), lambda j, i: (i, 0)),
                pl.BlockSpec((k, tn), lambda j, i: (0, j))]
    args = [a, b]
    kern = _mm_kernel
    if residual is not None:
        in_specs.append(pl.BlockSpec((tm, tn), lambda j, i: (i, j)))
        args.append(residual)
        kern = _mm_res_kernel
    return pl.pallas_call(
        kern,
        grid=grid,
        in_specs=in_specs,
        out_specs=pl.BlockSpec((tm, tn), lambda j, i: (i, j)),
        out_shape=jax.ShapeDtypeStruct((m, n), out_dtype),
        compiler_params=_cparams(("parallel", "parallel")),
        name=name,
    )(*args)


GDN_QKV = 3 * BRANCH_WIDTH
GDN_Z_OFF = GDN_QKV
GDN_BA_OFF = GDN_QKV + BRANCH_WIDTH
GDN_WIDTH = GDN_BA_OFF + LANES


def _gdn_kernel(p_ref, cw_ref, alog_ref, dtb_ref, ng_ref, o_ref,
                xbuf, carry, ybuf, bbuf, gbuf, state, *, tile, nb):
    s = pl.program_id(0)

    @pl.when(s == 0)
    def _():
        state[...] = jnp.zeros_like(state)
        carry[...] = jnp.zeros_like(carry)

    cw = cw_ref[...]
    for bi in range(nb):
        xbuf[bi, 0:SUBLANES, :] = carry[bi]
        xbuf[bi, SUBLANES:SUBLANES + tile, :] = p_ref[bi, :, 0:GDN_QKV]
        acc = xbuf[bi, SUBLANES:SUBLANES + tile, :] * cw[GDN_CONV - 1:GDN_CONV, :]
        for j in range(GDN_CONV - 1):
            off = SUBLANES - (GDN_CONV - 1) + j
            acc = acc + xbuf[bi, off:off + tile, :] * cw[j:j + 1, :]
        carry[bi] = xbuf[bi, tile:tile + SUBLANES, :]
        ybuf[bi] = _silu(acc)
        ba = p_ref[bi, :, GDN_BA_OFF:GDN_BA_OFF + LANES]
        bbuf[bi] = _sigmoid(ba)
        gbuf[bi] = -jnp.exp(alog_ref[...]) * _softplus(ba + dtb_ref[...])

    tri = _tri_inclusive_bf16(CHUNK)
    rb = _RowBlock(GDN_HEADS)
    lane_blk = _iota2(CHUNK, GDN_HEADS * CHUNK, 1) >> (CHUNK.bit_length() - 1)
    ng = ng_ref[...]
    hd = GDN_HEAD_DIM
    bw = BRANCH_WIDTH
    heads = range(GDN_HEADS)
    hsl = [slice(h * hd, (h + 1) * hd) for h in heads]

    def per_head_lanes(cols):
        return jnp.concatenate([jnp.broadcast_to(c, (c.shape[0], hd)) for c in cols], axis=1)

    def per_head_blocks(cols):
        out = cols[GDN_HEADS - 1]
        for h in range(GDN_HEADS - 2, -1, -1):
            out = jnp.where(lane_blk == h, cols[h], out)
        return out

    def l2n(x):
        return jnp.concatenate(
            [x[:, s_] * lax.rsqrt(jnp.sum(x[:, s_] * x[:, s_], axis=-1, keepdims=True) + 1e-6) for s_ in hsl],
            axis=1)

    def chunk_body(ci, _):
        base = pl.multiple_of(ci * (SEQ_CHUNKS_PER_STEP * CHUNK), SEQ_CHUNKS_PER_STEP * CHUNK)
        vrows = [(pl.ds(base + c * CHUNK, CHUNK), bi) for c in range(SEQ_CHUNKS_PER_STEP) for bi in range(nb)]
        yc = [ybuf[bi, r_, :] for r_, bi in vrows]
        bet = [bbuf[bi, r_, :] for r_, bi in vrows]
        gc = [_dot_sel_l(tri, gbuf[bi, r_, :]) for r_, bi in vrows]
        q = [l2n(x[:, 0:bw]) * (hd ** -0.5) for x in yc]
        k = [l2n(x[:, bw:2 * bw]) for x in yc]
        v = [x[:, 2 * bw:3 * bw] for x in yc]
        gcols = [[x[:, GDN_HEADS + h:GDN_HEADS + h + 1] for h in heads] for x in gc]
        glast = [[x[CHUNK - 1:CHUNK, GDN_HEADS + h:GDN_HEADS + h + 1] for h in heads] for x in gc]
        beta_l = [per_head_lanes([x[:, h:h + 1] for h in heads]) for x in bet]
        gcol_l = [per_head_lanes(c) for c in gcols]
        gcol_b = [per_head_blocks(c) for c in gcols]
        grow_b = [jnp.sum(jnp.where(rb.eye > 0, x, 0.0), axis=0, keepdims=True) for x in gcol_b]
        decay = [jnp.where(rb.causal, jnp.exp(jnp.where(rb.causal, x - y, 0.0)), 0.0)
                 for x, y in zip(gcol_b, grow_b)]
        eg = [jnp.exp(x) for x in gcol_l]
        kb = [x * y for x, y in zip(k, beta_l)]
        lhs = [_bf(jnp.concatenate([x, y], axis=0)) for x, y in zip(kb, q)]
        kr = [rb.rhs(x, False)[0] for x in k]
        pm = [_dot_nt(x, y) for x, y in zip(lhs, kr)]
        lower = [jnp.where(rb.strict, x[:CHUNK] * d, 0.0) for x, d in zip(pm, decay)]
        attn = [x[CHUNK:] * d for x, d in zip(pm, decay)]
        minv = _rb_unit_lower_inverse(rb, lower, INV_HI)
        u = [_rb_mul(m, rb.rhs(x * y, INV_HI)) for m, x, y in zip(minv, v, beta_l)]
        w = [_rb_mul(m, rb.rhs(x * y, INV_HI)) for m, x, y in zip(minv, kb, eg)]
        qd = [x * y for x, y in zip(q, eg)]
        k_tail = [[_bf(k[vi][:, hsl[h]] * jnp.exp(glast[vi][h] - gcols[vi][h])) for h in heads]
                  for vi in range(len(vrows))]
        for c in range(SEQ_CHUNKS_PER_STEP):
            it = [(c * nb + bi, bi * GDN_HEADS + h, h) for bi in range(nb) for h in heads]
            sts = [state[si] for _, si, _ in it]
            stb = [_bf(x) for x in sts]
            wq = [_dot(_bf(jnp.concatenate([w[vi][:, hsl[h]], qd[vi][:, hsl[h]]], axis=0)), sb)
                  for (vi, _, h), sb in zip(it, stb)]
            v_new = {vi: jnp.concatenate([u[vi][:, hsl[h]] - wq[bi * GDN_HEADS + h][:CHUNK] for h in heads], axis=1)
                     for bi, vi in enumerate(range(c * nb, (c + 1) * nb))}
            av = {vi: _rb_mul(attn[vi], rb.rhs(x, False)) for vi, x in v_new.items()}
            upd = [_dot_tn(k_tail[vi][h], _bf(v_new[vi][:, hsl[h]])) for vi, _, h in it]
            for i, (vi, si, h) in enumerate(it):
                r_, bi = vrows[vi]
                state[si] = sts[i] * jnp.exp(glast[vi][h]) + upd[i]
                o = wq[i][CHUNK:] + av[vi][:, hsl[h]]
                z = p_ref[bi, r_, GDN_Z_OFF + h * hd:GDN_Z_OFF + (h + 1) * hd]
                on = (o * lax.rsqrt(jnp.mean(o * o, axis=-1, keepdims=True) + NORM_EPS)) * ng
                o_ref[bi, r_, hsl[h]] = (on * _silu(z)).astype(o_ref.dtype)
        return 0

    lax.fori_loop(0, tile // (SEQ_CHUNKS_PER_STEP * CHUNK), chunk_body, 0)


def _gdn_branch(proj, conv_w, a_log, dt_bias, norm_g, *, batch, seq, tile):
    zeros = jnp.zeros((LANES,), F32)
    alog_p = zeros.at[GDN_HEADS:2 * GDN_HEADS].set(a_log).reshape(1, LANES)
    dtb_p = zeros.at[GDN_HEADS:2 * GDN_HEADS].set(dt_bias).reshape(1, LANES)
    const = lambda s: (0, 0)
    out = pl.pallas_call(
        functools.partial(_gdn_kernel, tile=tile, nb=batch),
        grid=(seq // tile,),
        in_specs=[pl.BlockSpec((batch, tile, GDN_WIDTH), lambda s: (0, s, 0)),
                  pl.BlockSpec((GDN_CONV, GDN_QKV), const),
                  pl.BlockSpec((1, LANES), const),
                  pl.BlockSpec((1, LANES), const),
                  pl.BlockSpec((1, GDN_HEAD_DIM), const)],
        out_specs=pl.BlockSpec((batch, tile, BRANCH_WIDTH), lambda s: (0, s, 0)),
        out_shape=jax.ShapeDtypeStruct((batch, seq, BRANCH_WIDTH), BF16),
        scratch_shapes=[pltpu.VMEM((batch, tile + SUBLANES, GDN_QKV), F32),
                        pltpu.VMEM((batch, SUBLANES, GDN_QKV), F32),
                        pltpu.VMEM((batch, tile, GDN_QKV), F32),
                        pltpu.VMEM((batch, tile, LANES), F32),
                        pltpu.VMEM((batch, tile, LANES), F32),
                        pltpu.VMEM((batch * GDN_HEADS, GDN_HEAD_DIM, GDN_HEAD_DIM), F32)],
        compiler_params=_cparams(("arbitrary",)),
        name="gdn_branch",
    )(proj.reshape(batch, seq, GDN_WIDTH), conv_w, alog_p, dtb_p, norm_g.reshape(1, GDN_HEAD_DIM))
    return out.reshape(batch * seq, BRANCH_WIDTH)


RWKV_R_OFF = 0
RWKV_K_OFF = BRANCH_WIDTH
RWKV_V_OFF = 2 * BRANCH_WIDTH
RWKV_WD_OFF = 3 * BRANCH_WIDTH
RWKV_AD_OFF = RWKV_WD_OFF + LANES
RWKV_GD_OFF = RWKV_AD_OFF + LANES
RWKV_WIDTH = RWKV_GD_OFF + LANES
RWKV_GROUP_HEADS = 4
RWKV_GROUP_W = RWKV_GROUP_HEADS * RWKV_HEAD_DIM


def _rwkv_kernel(p_ref, mu_ref, w0_ref, wup_ref, a0_ref, aup_ref, gup_ref, kk_ref, ka_ref,
                 rk_ref, lng_ref, lnb_ref, o_ref, xbuf, carry, hsbuf, state, *, tile, nb):
    s = pl.program_id(0)

    @pl.when(s == 0)
    def _():
        state[...] = jnp.zeros_like(state)
        carry[...] = jnp.zeros_like(carry)

    for bi in range(nb):
        xbuf[bi, 0:SUBLANES, :] = carry[bi]
        xbuf[bi, SUBLANES:SUBLANES + tile, :] = p_ref[bi]
        hr = xbuf[bi, SUBLANES:SUBLANES + tile, :]
        prev = xbuf[bi, SUBLANES - 1:SUBLANES - 1 + tile, :]
        carry[bi] = xbuf[bi, tile:tile + SUBLANES, :]
        hsbuf[bi] = hr + (prev - hr) * mu_ref[...]

    tri = _tri_inclusive_bf16(CHUNK)
    rb = _RowBlock(RWKV_GROUP_HEADS)
    gw = RWKV_GROUP_W
    seg = rb.bd_mask(RWKV_HEAD_DIM)
    segf = seg.astype(F32)
    inv_hd = 1.0 / RWKV_HEAD_DIM
    ng = BRANCH_WIDTH // gw
    items = [(bi, gi) for bi in range(nb) for gi in range(ng)]
    sl = [slice(gi * gw, (gi + 1) * gw) for _, gi in items]
    bidx = [bi for bi, _ in items]

    def chunk_body(ci, _):
        r0 = pl.multiple_of(ci * CHUNK, CHUNK)
        rows = pl.ds(r0, CHUNK)
        hs = [hsbuf[bi, rows, :] for bi in range(nb)]
        rv = [x[:, RWKV_R_OFF:RWKV_R_OFF + BRANCH_WIDTH] for x in hs]
        kv = [x[:, RWKV_K_OFF:RWKV_K_OFF + BRANCH_WIDTH] for x in hs]
        vv = [x[:, RWKV_V_OFF:RWKV_V_OFF + BRANCH_WIDTH] for x in hs]
        lora_in = jnp.concatenate([x[:, RWKV_WD_OFF:RWKV_WIDTH] for x in hs], axis=0)
        dw = _dot(_bf(jnp.tanh(lora_in[:, 0:LANES])), wup_ref[...])
        da = _dot(_bf(lora_in[:, LANES:2 * LANES]), aup_ref[...])
        gate_all = _dot(_bf(_sigmoid(lora_in[:, 2 * LANES:3 * LANES])), gup_ref[...])
        rowsl = [slice(bi * CHUNK, (bi + 1) * CHUNK) for bi in range(nb)]
        log_w = [-RWKV_DECAY_SCALE * _sigmoid(w0_ref[...] + dw[r_]) for r_ in rowsl]
        a_lr = [_sigmoid(a0_ref[...] + da[r_]) for r_ in rowsl]
        gate = [gate_all[r_] for r_ in rowsl]
        kkr = [x * kk_ref[...] for x in kv]
        kmod = [x * (1.0 + (a - 1.0) * ka_ref[...]) for x, a in zip(kv, a_lr)]
        rk = [x * y * rk_ref[...] for x, y in zip(rv, kmod)]
        g = [_dot_sel_l(tri, x) for x in log_w]
        egn = [jnp.exp(-x) for x in g]
        glast = [x[CHUNK - 1:CHUNK, :] for x in g]
        etail = [jnp.exp(gl - x) for gl, x in zip(glast, g)]
        dec = [jnp.exp(x) for x in glast]
        r_t = [x * jnp.exp(y) for x, y in zip(rv, g)]
        g_prev = [x - y for x, y in zip(g, log_w)]
        kkss = _stacked_sel_r([kkr[bi][:, s_] * kkr[bi][:, s_] for bi, s_ in zip(bidx, sl)], seg)
        bon = _stacked_sel_r([rk[bi][:, s_] for bi, s_ in zip(bidx, sl)], seg)
        kk = [kkr[bi][:, s_] * lax.rsqrt(x + 1e-6) for bi, s_, x in zip(bidx, sl, kkss)]
        b = [x * a_lr[bi][:, s_] for bi, s_, x in zip(bidx, sl, kk)]
        a_t = [-x * jnp.exp(g_prev[bi][:, s_]) for bi, s_, x in zip(bidx, sl, kk)]
        vs = [vv[bi][:, s_] for bi, s_ in zip(bidx, sl)]
        km = [kmod[bi][:, s_] for bi, s_ in zip(bidx, sl)]
        lhs = [_bf(jnp.concatenate([x, r_t[bi][:, s_]], axis=0)) for bi, s_, x in zip(bidx, sl, a_t)]
        rbt = [rb.rhs(x * egn[bi][:, s_], False)[0] for bi, s_, x in zip(bidx, sl, b)]
        rkt = [rb.rhs(x * egn[bi][:, s_], False)[0] for bi, s_, x in zip(bidx, sl, km)]
        pb = [_dot_nt(x, y) for x, y in zip(lhs, rbt)]
        pk = [_dot_nt(x, y) for x, y in zip(lhs, rkt)]
        minv = _rb_unit_lower_inverse(rb, [jnp.where(rb.strict, -x[:CHUNK], 0.0) for x in pb], INV_HI)
        rv_rhs = [rb.rhs(x, False) for x in vs]
        a_k = [jnp.concatenate([jnp.where(rb.strict, x[:CHUNK], 0.0), jnp.where(rb.causal, x[CHUNK:], 0.0)],
                               axis=0) for x in pk]
        a_kv = [_rb_mul(x, r) for x, r in zip(a_k, rv_rhs)]
        akv = [x[:CHUNK] for x in a_kv]
        arkv = [x[CHUNK:] for x in a_kv]
        a_rb = [jnp.where(rb.causal, x[CHUNK:], 0.0) for x in pb]
        sts = [state[i] for i in range(len(items))]
        init = [_dot_nt(x, _bf(st)) for x, st in zip(lhs, sts)]
        u = [_rb_mul(m, rb.rhs(x[:CHUNK] + y, INV_HI)) for m, x, y in zip(minv, init, akv)]
        y = [x[CHUNK:] + _rb_mul(p, rb.rhs(q, False)) + w for x, p, q, w in zip(init, a_rb, u, arkv)]
        tails = [_bf(jnp.concatenate([x * etail[bi][:, s_], k_ * etail[bi][:, s_]], axis=0))
                 for bi, s_, x, k_ in zip(bidx, sl, b, km)]
        upd = [_dot_tn(_bf(jnp.concatenate([x, v_], axis=0)), t) for x, v_, t in zip(u, vs, tails)]
        for i, (bi, s_) in enumerate(zip(bidx, sl)):
            state[i] = sts[i] * dec[bi][:, s_] + upd[i] * segf
        mean = [x * inv_hd for x in _stacked_sel_r(y, seg)]
        yc = [x - m for x, m in zip(y, mean)]
        var = [x * inv_hd for x in _stacked_sel_r([x * x for x in yc], seg)]
        for i, (bi, s_) in enumerate(zip(bidx, sl)):
            yn = yc[i] * lax.rsqrt(var[i] + RWKV_LN_EPS) * lng_ref[:, s_] + lnb_ref[:, s_]
            o_ref[bi, rows, s_] = ((yn + bon[i] * vs[i]) * gate[bi][:, s_]).astype(o_ref.dtype)
        return 0

    lax.fori_loop(0, tile // CHUNK, chunk_body, 0)


def _pad_rows(w, rows):
    return jnp.zeros((rows,) + w.shape[1:], w.dtype).at[:w.shape[0]].set(w)


def _rwkv_mu_layout(mu):
    z = jnp.zeros((LANES - RWKV_DECAY_LORA,), mu.dtype)
    o = 3 * BRANCH_WIDTH
    return jnp.concatenate([mu[:o], mu[o:o + RWKV_DECAY_LORA], z,
                            mu[o + RWKV_DECAY_LORA:o + RWKV_DECAY_LORA + RWKV_ICLR_LORA], z,
                            mu[o + RWKV_DECAY_LORA + RWKV_ICLR_LORA:]])


def _rwkv_branch(proj, mu, w0, w_up, a0, a_up, g_up, k_k, k_a, r_k, ln_g, ln_b, *, batch, seq, tile):
    bw = BRANCH_WIDTH
    row = lambda x: x.reshape(1, -1).astype(F32)
    const = lambda s: (0, 0)
    vec = pl.BlockSpec((1, bw), const)
    n_state = batch * (bw // RWKV_GROUP_W)
    out = pl.pallas_call(
        functools.partial(_rwkv_kernel, tile=tile, nb=batch),
        grid=(seq // tile,),
        in_specs=[pl.BlockSpec((batch, tile, RWKV_WIDTH), lambda s: (0, s, 0)),
                  pl.BlockSpec((1, RWKV_WIDTH), const),
                  vec, pl.BlockSpec((LANES, bw), const),
                  vec, pl.BlockSpec((LANES, bw), const),
                  pl.BlockSpec((LANES, bw), const),
                  vec, vec, vec, vec, vec],
        out_specs=pl.BlockSpec((batch, tile, bw), lambda s: (0, s, 0)),
        out_shape=jax.ShapeDtypeStruct((batch, seq, bw), BF16),
        scratch_shapes=[pltpu.VMEM((batch, tile + SUBLANES, RWKV_WIDTH), F32),
                        pltpu.VMEM((batch, SUBLANES, RWKV_WIDTH), F32),
                        pltpu.VMEM((batch, tile, RWKV_WIDTH), F32),
                        pltpu.VMEM((n_state, RWKV_GROUP_W, RWKV_GROUP_W), F32)],
        compiler_params=_cparams(("arbitrary",)),
        name="rwkv_branch",
    )(proj.reshape(batch, seq, RWKV_WIDTH), row(_rwkv_mu_layout(mu)), row(w0), _bf(_pad_rows(w_up, LANES)),
      row(a0), _bf(_pad_rows(a_up, LANES)), _bf(g_up), row(k_k), row(k_a), row(r_k), row(ln_g), row(ln_b))
    return out.reshape(batch * seq, bw)


def _pool_kernel(u_ref, pw_ref, ps_ref, o_ref, xbuf, sbuf, carry, *, tile):
    s = pl.program_id(1)

    @pl.when(s == 0)
    def _():
        carry[...] = jnp.zeros_like(carry)

    hist = POOL_MAX_WINDOW
    n = tile + hist
    xbuf[0:hist, :] = carry[...]
    xbuf[hist:n, :] = u_ref[...]
    carry[...] = xbuf[tile:n, :]
    pos = s * tile + _iota2(tile, LANES, 0)
    gw = LANES
    for gi, win in enumerate(POOL_WINDOWS):
        sl = slice(gi * gw, (gi + 1) * gw)
        sbuf[...] = xbuf[:, sl]
        span = 1
        while span < win:
            sbuf[span:n, :] = sbuf[span:n, :] + sbuf[0:n - span, :]
            span *= 2
        x = xbuf[hist:n, sl]
        count = jnp.minimum(pos + 1, win).astype(F32)
        pooled = sbuf[hist:n, :] / count - x
        y = _dot(_bf(pooled), pw_ref[gi])
        o_ref[:, sl] = (y * ps_ref[:, sl]).astype(o_ref.dtype)


def _pool_branch(u, pool_w, pool_scale, *, batch, seq, tile):
    ns = seq // tile
    bw = BRANCH_WIDTH
    return pl.pallas_call(
        functools.partial(_pool_kernel, tile=tile),
        grid=(batch, ns),
        in_specs=[pl.BlockSpec((tile, bw), lambda b, s: (b * ns + s, 0)),
                  pl.BlockSpec((len(POOL_WINDOWS), LANES, LANES), lambda b, s: (0, 0, 0)),
                  pl.BlockSpec((1, bw), lambda b, s: (0, 0))],
        out_specs=pl.BlockSpec((tile, bw), lambda b, s: (b * ns + s, 0)),
        out_shape=jax.ShapeDtypeStruct((batch * seq, bw), BF16),
        scratch_shapes=[pltpu.VMEM((tile + POOL_MAX_WINDOW, bw), F32),
                        pltpu.VMEM((tile + POOL_MAX_WINDOW, LANES), F32),
                        pltpu.VMEM((POOL_MAX_WINDOW, bw), F32)],
        compiler_params=_cparams(("parallel", "arbitrary")),
        name="pool_branch",
    )(u, _bf(pool_w), pool_scale.reshape(1, bw))


GLA_QW = GLA_HEADS * GLA_KEY_DIM
GLA_Q_OFF = 0
GLA_K_OFF = GLA_QW
GLA_V_OFF = 2 * GLA_QW
GLA_G_OFF = GLA_V_OFF + BRANCH_WIDTH
GLA_F_OFF = GLA_G_OFF + BRANCH_WIDTH
GLA_WIDTH = GLA_F_OFF + LANES


def _gla_kernel(p_ref, fup_ref, fb_ref, ng_ref, o_ref, state, *, tile, nb):
    s = pl.program_id(0)

    @pl.when(s == 0)
    def _():
        state[...] = jnp.zeros_like(state)

    tri = _tri_inclusive_bf16(CHUNK)
    rb = _RowBlock(GLA_HEADS)
    vh = _iota2(BRANCH_WIDTH, GLA_QW, 0) >> (GLA_VAL_DIM.bit_length() - 1)
    kh = _iota2(BRANCH_WIDTH, GLA_QW, 1) >> (GLA_KEY_DIM.bit_length() - 1)
    same_head = (vh == kh).astype(F32)
    ng = ng_ref[...]
    dv = GLA_VAL_DIM
    bs = range(nb)

    def chunk_body(ci, _):
        r0 = pl.multiple_of(ci * CHUNK, CHUNK)
        rows = pl.ds(r0, CHUNK)
        fl = jnp.concatenate([p_ref[bi, rows, GLA_F_OFF:GLA_F_OFF + LANES] for bi in bs], axis=0)
        logits = _dot(_bf(fl), fup_ref[...]) + fb_ref[...]
        log_f = -_softplus(-logits) / GLA_GATE_NORM
        gc = [_dot_sel_l(tri, log_f[bi * CHUNK:(bi + 1) * CHUNK]) for bi in bs]
        q = [p_ref[bi, rows, GLA_Q_OFF:GLA_Q_OFF + GLA_QW] * (GLA_KEY_DIM ** -0.5) for bi in bs]
        k = [p_ref[bi, rows, GLA_K_OFF:GLA_K_OFF + GLA_QW] for bi in bs]
        v = [p_ref[bi, rows, GLA_V_OFF:GLA_V_OFF + BRANCH_WIDTH] for bi in bs]
        q_dec = [_bf(x * jnp.exp(g)) for x, g in zip(q, gc)]
        k_dec = [x * jnp.exp(-g) for x, g in zip(k, gc)]
        glast = [g[CHUNK - 1:CHUNK, :] for g in gc]
        k_tail = [_bf(x * jnp.exp(gl - g)) for x, gl, g in zip(k, glast, gc)]
        attn = [_dot_nt(x, rb.rhs(y, False)[0]) for x, y in zip(q_dec, k_dec)]
        intra = [_rb_mul(jnp.where(rb.causal, a, 0.0), rb.rhs(x, False)) for a, x in zip(attn, v)]
        sts = [state[bi] for bi in bs]
        inter = [_dot_nt(x, _bf(st)) for x, st in zip(q_dec, sts)]
        upd = [_dot_tn(_bf(x), y) for x, y in zip(v, k_tail)]
        for bi in bs:
            state[bi] = sts[bi] * jnp.exp(glast[bi]) + upd[bi] * same_head
            o = intra[bi] + inter[bi]
            for h in range(GLA_HEADS):
                oh = o[:, h * dv:(h + 1) * dv]
                gate = p_ref[bi, rows, GLA_G_OFF + h * dv:GLA_G_OFF + (h + 1) * dv]
                on = (oh * lax.rsqrt(jnp.mean(oh * oh, axis=-1, keepdims=True) + NORM_EPS)) * ng
                o_ref[bi, rows, h * dv:(h + 1) * dv] = (on * _silu(gate)).astype(o_ref.dtype)
        return 0

    lax.fori_loop(0, tile // CHUNK, chunk_body, 0)


def _gla_branch(proj, f_up, f_bias, norm_g, *, batch, seq, tile):
    const = lambda s: (0, 0)
    out = pl.pallas_call(
        functools.partial(_gla_kernel, tile=tile, nb=batch),
        grid=(seq // tile,),
        in_specs=[pl.BlockSpec((batch, tile, GLA_WIDTH), lambda s: (0, s, 0)),
                  pl.BlockSpec((LANES, GLA_QW), const),
                  pl.BlockSpec((1, GLA_QW), const),
                  pl.BlockSpec((1, GLA_VAL_DIM), const)],
        out_specs=pl.BlockSpec((batch, tile, BRANCH_WIDTH), lambda s: (0, s, 0)),
        out_shape=jax.ShapeDtypeStruct((batch, seq, BRANCH_WIDTH), BF16),
        scratch_shapes=[pltpu.VMEM((batch, BRANCH_WIDTH, GLA_QW), F32)],
        compiler_params=_cparams(("arbitrary",)),
        name="gla_branch",
    )(proj.reshape(batch, seq, GLA_WIDTH), _bf(_pad_rows(f_up, LANES)), f_bias.reshape(1, GLA_QW),
      norm_g.reshape(1, GLA_VAL_DIM))
    return out.reshape(batch * seq, BRANCH_WIDTH)


def _merge_kernel(hn_ref, y0, y1, y2, y3, bp_ref, wg0, wg1, wg2, wg3, gb_ref, o_ref):
    ys = (y0, y1, y2, y3)
    wgs = (wg0, wg1, wg2, wg3)
    hn = hn_ref[...]
    acc = None
    for i in range(N_BRANCH):
        logit = _dot(hn, wgs[i][...]) + gb_ref[i]
        gate = 0.5 * (jnp.tanh(0.5 * logit) + 1.0)
        term = gate * _dot(ys[i][...], bp_ref[i])
        acc = term if acc is None else acc + term
    o_ref[...] = acc.astype(o_ref.dtype)


def _merge(hn, ys, branch_proj, w_gate, gate_bias, *, tm, tn):
    t, dm = hn.shape
    d = branch_proj.shape[-1]
    nj = d // tn
    y_spec = pl.BlockSpec((tm, BRANCH_WIDTH), lambda j, i: (i, 0))
    wg_specs = [pl.BlockSpec((dm, tn), functools.partial(lambda j, i, b: (0, b * nj + j), b=b))
                for b in range(N_BRANCH)]
    return pl.pallas_call(
        _merge_kernel,
        grid=(nj, t // tm),
        in_specs=[pl.BlockSpec((tm, dm), lambda j, i: (i, 0))]
        + [y_spec] * N_BRANCH
        + [pl.BlockSpec((N_BRANCH, BRANCH_WIDTH, tn), lambda j, i: (0, 0, j))]
        + wg_specs
        + [pl.BlockSpec((N_BRANCH, 1, tn), lambda j, i: (0, 0, j))],
        out_specs=pl.BlockSpec((tm, tn), lambda j, i: (i, j)),
        out_shape=jax.ShapeDtypeStruct((t, d), BF16),
        compiler_params=_cparams(("parallel", "parallel")),
        name="merge",
    )(hn, *ys, _bf(branch_proj), w_gate, w_gate, w_gate, w_gate, gate_bias.reshape(N_BRANCH, 1, d))


def _rms_rows(x, g):
    return (x * lax.rsqrt(jnp.mean(x * x, axis=-1, keepdims=True) + NORM_EPS)) * g


def _ffn_kernel(h_ref, g_ref, gn_ref, w1_ref, w3_ref, w2_ref, o_ref, on_ref, xn_ref, acc_ref, *, nf):
    f = pl.program_id(1)

    @pl.when(f == 0)
    def _():
        acc_ref[...] = jnp.zeros_like(acc_ref)
        xn_ref[...] = _bf(_rms_rows(h_ref[...], g_ref[...]))

    x = xn_ref[...]
    mid = _bf(_silu(_dot(x, w1_ref[...])) * _dot(x, w3_ref[...]))
    acc_ref[...] += _dot(mid, w2_ref[...])

    @pl.when(f == nf - 1)
    def _():
        out = h_ref[...] + acc_ref[...]
        o_ref[...] = out
        on_ref[...] = _rms_rows(out, gn_ref[...]).astype(on_ref.dtype)


def _ffn(h, g, next_g, next_dtype, w1, w3, w2, *, tm, tf):
    t, d = h.shape
    ff = w1.shape[1]
    nf = ff // tf
    row = pl.BlockSpec((tm, d), lambda i, f: (i, 0))
    vec = pl.BlockSpec((1, d), lambda i, f: (0, 0))
    return pl.pallas_call(
        functools.partial(_ffn_kernel, nf=nf),
        grid=(t // tm, nf),
        in_specs=[row, vec, vec,
                  pl.BlockSpec((d, tf), lambda i, f: (0, f)),
                  pl.BlockSpec((d, tf), lambda i, f: (0, f)),
                  pl.BlockSpec((tf, d), lambda i, f: (f, 0))],
        out_specs=[row, row],
        out_shape=[jax.ShapeDtypeStruct((t, d), F32), jax.ShapeDtypeStruct((t, d), next_dtype)],
        scratch_shapes=[pltpu.VMEM((tm, d), BF16), pltpu.VMEM((tm, d), F32)],
        compiler_params=_cparams(("parallel", "arbitrary")),
        name="ffn",
    )(h, g.reshape(1, d), next_g.reshape(1, d), w1, w3, w2)


ROUTE_E0, ROUTE_E1, ROUTE_RANK0, ROUTE_RANK1 = 0, 1, 2, 3
ROUTE_W0, ROUTE_W1 = 0, 1


def _router_kernel(h_ref, g_ref, rw_ref, ri_ref, rf_ref, cnt_ref, run):
    i = pl.program_id(0)

    @pl.when(i == 0)
    def _():
        run[...] = jnp.zeros_like(run)

    tm = h_ref.shape[0]
    hn = _bf(_rms_rows(h_ref[...], g_ref[...]))
    logits = _dot(hn, rw_ref[...])
    lane = _iota2(tm, LANES, 1)
    neg = jnp.float32(-jnp.inf)
    lg = jnp.where(lane < N_EXPERTS, logits, neg)
    m1 = jnp.max(lg, axis=-1, keepdims=True)
    e0 = jnp.min(jnp.where(lg == m1, lane, LANES), axis=-1, keepdims=True)
    lg2 = jnp.where(lane == e0, neg, lg)
    m2 = jnp.max(lg2, axis=-1, keepdims=True)
    e1 = jnp.min(jnp.where(lg2 == m2, lane, LANES), axis=-1, keepdims=True)
    ex = jnp.exp(m2 - m1)
    den = 1.0 + ex
    w0 = 1.0 / den
    w1 = ex / den
    hit0 = lane == e0
    hit1 = lane == e1
    onehot = (hit0 | hit1).astype(F32)
    rr = _iota2(tm, tm, 0)
    cc = _iota2(tm, tm, 1)
    before = _dot((cc < rr).astype(BF16), _bf(onehot)) + run[...]
    rank0 = jnp.sum(jnp.where(hit0, before, 0.0), axis=-1, keepdims=True).astype(I32)
    rank1 = jnp.sum(jnp.where(hit1, before, 0.0), axis=-1, keepdims=True).astype(I32)
    run[...] += jnp.sum(onehot, axis=0, keepdims=True)
    cnt_ref[...] = run[...]
    ri_ref[...] = jnp.where(lane == ROUTE_E0, e0,
                            jnp.where(lane == ROUTE_E1, e1,
                                      jnp.where(lane == ROUTE_RANK0, rank0,
                                                jnp.where(lane == ROUTE_RANK1, rank1, 0))))
    rf_ref[...] = jnp.where(lane == ROUTE_W0, w0, jnp.where(lane == ROUTE_W1, w1, 0.0))


def _router(h, g, router_w, *, tm=TOKEN_TILE):
    t, d = h.shape
    rw = _bf(jnp.zeros((d, LANES), F32).at[:, :N_EXPERTS].set(router_w))
    return pl.pallas_call(
        _router_kernel,
        grid=(t // tm,),
        in_specs=[pl.BlockSpec((tm, d), lambda i: (i, 0)),
                  pl.BlockSpec((1, d), lambda i: (0, 0)),
                  pl.BlockSpec((d, LANES), lambda i: (0, 0))],
        out_specs=[pl.BlockSpec((tm, LANES), lambda i: (i, 0)),
                   pl.BlockSpec((tm, LANES), lambda i: (i, 0)),
                   pl.BlockSpec((1, LANES), lambda i: (0, 0))],
        out_shape=[jax.ShapeDtypeStruct((t, LANES), I32),
                   jax.ShapeDtypeStruct((t, LANES), F32),
                   jax.ShapeDtypeStruct((1, LANES), F32)],
        scratch_shapes=[pltpu.VMEM((1, LANES), F32)],
        compiler_params=_cparams(("arbitrary",)),
        name="moe_router",
    )(h, g.reshape(1, d), rw)


def _row_copy(src_ref, src_row, dst_ref, dst_row, sem):
    return pltpu.make_async_copy(src_ref.at[pl.ds(src_row, 1)], dst_ref.at[pl.ds(dst_row, 1)], sem)


DMA_ISSUE_UNROLL = 8


def _dispatch_kernel(d0_ref, d1_ref, zs_ref, zf_ref, x_ref, xb_ref, zbuf, sem, zsem):
    tm = x_ref.shape[0]
    step = pl.program_id(0)
    base = step * tm

    @pl.when(step == 0)
    def _():
        zbuf[...] = jnp.zeros_like(zbuf)

        def zero_copy(j):
            first = pl.multiple_of(zs_ref[j], MOE_BLOCK)
            return pltpu.make_async_copy(zbuf, xb_ref.at[pl.ds(first, MOE_BLOCK)], zsem)

        for j in range(2 * N_EXPERTS):
            @pl.when(zf_ref[j] > 0)
            def _():
                zero_copy(j).start()
        for j in range(2 * N_EXPERTS):
            @pl.when(zf_ref[j] > 0)
            def _():
                zero_copy(j).wait()

    def copies(rw):
        return (_row_copy(x_ref, rw, xb_ref, d0_ref[base + rw], sem),
                _row_copy(x_ref, rw, xb_ref, d1_ref[base + rw], sem))

    def start(rw, _):
        for prio, cp in enumerate(copies(rw)):
            cp.start(priority=prio)
        return 0

    def wait(rw, _):
        for cp in copies(rw):
            cp.wait()
        return 0

    lax.fori_loop(0, tm, start, 0, unroll=DMA_ISSUE_UNROLL)
    lax.fori_loop(0, tm, wait, 0, unroll=DMA_ISSUE_UNROLL)


def _dispatch(x, dest0, dest1, zero_start, zero_flag, n_rows, *, tm=TOKEN_TILE):
    t, d = x.shape
    return pl.pallas_call(
        _dispatch_kernel,
        grid_spec=pltpu.PrefetchScalarGridSpec(
            num_scalar_prefetch=4,
            grid=(t // tm,),
            in_specs=[pl.BlockSpec((tm, d), lambda i, *_: (i, 0))],
            out_specs=pl.BlockSpec(memory_space=pl.ANY),
            scratch_shapes=[pltpu.VMEM((MOE_BLOCK, d), F32),
                            pltpu.SemaphoreType.DMA(()), pltpu.SemaphoreType.DMA(())]),
        out_shape=jax.ShapeDtypeStruct((n_rows, d), F32),
        compiler_params=_cparams(("arbitrary",)),
        name="moe_dispatch",
    )(dest0, dest1, zero_start, zero_flag, x)


MOE_UNIT_BLOCKS = 2


def _moe_ffn_kernel(ue_ref, ub_ref, uv_ref, zs_ref, zf_ref, *refs, nf):
    del ue_ref
    nu = MOE_UNIT_BLOCKS
    x_refs = refs[:nu]
    g_ref, w1_ref, w3_ref, w2_ref, yb_ref = refs[nu:nu + 5]
    xns = refs[nu + 5:2 * nu + 5]
    accs = refs[2 * nu + 5:3 * nu + 5]
    wb1, wb3, wb2, sem = refs[3 * nu + 5:3 * nu + 9]
    u = pl.program_id(0)
    f = pl.program_id(1)
    slot = [u * nu + j for j in range(nu)]

    @pl.when((u == 0) & (f == 0))
    def _():
        zsrc = accs[nu - 1]
        zsrc[...] = jnp.zeros_like(zsrc)

        def zero_copy(j):
            first = pl.multiple_of(zs_ref[j], MOE_BLOCK)
            return pltpu.make_async_copy(zsrc, yb_ref.at[pl.ds(first, MOE_BLOCK)], sem)

        for j in range(N_EXPERTS):
            @pl.when(zf_ref[j] > 0)
            def _():
                zero_copy(j).start()
        for j in range(N_EXPERTS):
            @pl.when(zf_ref[j] > 0)
            def _():
                zero_copy(j).wait()

    @pl.when(f == 0)
    def _():
        for j in range(nu):
            @pl.when(uv_ref[slot[j]] > 0)
            def _():
                accs[j][...] = jnp.zeros_like(accs[j])
                xns[j][...] = _bf(_rms_rows(x_refs[j][...], g_ref[...]))

    for j in range(nu):
        @pl.when(uv_ref[slot[j]] > 0)
        def _():
            x = xns[j][...]
            if j == 0:
                w1, w3, w2 = _bf(w1_ref[0]), _bf(w3_ref[0]), _bf(w2_ref[0])
                if nu > 1:
                    wb1[...] = w1
                    wb3[...] = w3
                    wb2[...] = w2
            else:
                w1, w3, w2 = wb1[...], wb3[...], wb2[...]
            mid = _bf(_silu(_dot(x, w1)) * _dot(x, w3))
            accs[j][...] += _dot(mid, w2)

    @pl.when(f == nf - 1)
    def _():
        def out_copy(j):
            first = pl.multiple_of(ub_ref[slot[j]] * MOE_BLOCK, MOE_BLOCK)
            return pltpu.make_async_copy(accs[j], yb_ref.at[pl.ds(first, MOE_BLOCK)], sem)

        for j in range(nu):
            @pl.when(uv_ref[slot[j]] > 0)
            def _():
                out_copy(j).start()
        for j in range(nu):
            @pl.when(uv_ref[slot[j]] > 0)
            def _():
                out_copy(j).wait()


def _moe_ffn(xb, g, w1, w3, w2, unit_e, unit_b, unit_v, spare_start, spare_flag, *, tf):
    n_rows, d = xb.shape
    ff = w1.shape[-1]
    nf = ff // tf
    nu = MOE_UNIT_BLOCKS
    n_units = unit_e.shape[0]
    wcol = lambda u, f, ue, ub, uv, *_: (ue[u], 0, f * uv[u * nu])
    wrow = lambda u, f, ue, ub, uv, *_: (ue[u], f * uv[u * nu], 0)
    x_specs = [pl.BlockSpec((MOE_BLOCK, d), functools.partial(lambda u, f, ue, ub, *_, j: (ub[u * nu + j], 0), j=j),
                            pipeline_mode=pl.Buffered(1)) for j in range(nu)]
    return pl.pallas_call(
        functools.partial(_moe_ffn_kernel, nf=nf),
        grid_spec=pltpu.PrefetchScalarGridSpec(
            num_scalar_prefetch=5,
            grid=(n_units, nf),
            in_specs=x_specs + [pl.BlockSpec((1, d), lambda u, f, *_: (0, 0)),
                                pl.BlockSpec((1, d, tf), wcol),
                                pl.BlockSpec((1, d, tf), wcol),
                                pl.BlockSpec((1, tf, d), wrow)],
            out_specs=pl.BlockSpec(memory_space=pl.ANY),
            scratch_shapes=[pltpu.VMEM((MOE_BLOCK, d), BF16)] * nu + [pltpu.VMEM((MOE_BLOCK, d), F32)] * nu
            + [pltpu.VMEM((d, tf), BF16), pltpu.VMEM((d, tf), BF16), pltpu.VMEM((tf, d), BF16),
               pltpu.SemaphoreType.DMA(())]),
        out_shape=jax.ShapeDtypeStruct((n_rows, d), F32),
        compiler_params=_cparams(("arbitrary", "arbitrary")),
        name="moe_experts",
    )(unit_e, unit_b, unit_v, spare_start, spare_flag, *([xb] * nu), g.reshape(1, d), w1, w3, w2)


COMBINE_SLOTS = 2


def _combine_kernel(d0_ref, d1_ref, yb_ref, h_ref, rf_ref, g_ref, o_ref, buf, sems, *, final_norm):
    tm = h_ref.shape[0]
    i = pl.program_id(0)
    n = pl.num_programs(0)
    slot = lax.rem(i, COMBINE_SLOTS)

    def copies(step, sl, rw):
        base = step * tm
        return (_row_copy(yb_ref, d0_ref[base + rw], buf.at[sl, 0], rw, sems.at[sl]),
                _row_copy(yb_ref, d1_ref[base + rw], buf.at[sl, 1], rw, sems.at[sl]))

    def start_all(step, sl):
        def body(rw, _):
            for prio, cp in enumerate(copies(step, sl, rw)):
                cp.start(priority=prio)
            return 0
        lax.fori_loop(0, tm, body, 0, unroll=DMA_ISSUE_UNROLL)

    def wait_all(step, sl):
        def body(rw, _):
            for cp in copies(step, sl, rw):
                cp.wait()
            return 0
        lax.fori_loop(0, tm, body, 0, unroll=DMA_ISSUE_UNROLL)

    @pl.when(i == 0)
    def _():
        start_all(i, slot)

    @pl.when(i + 1 < n)
    def _():
        start_all(i + 1, 1 - slot)

    wait_all(i, slot)
    w0 = rf_ref[:, ROUTE_W0:ROUTE_W0 + 1]
    w1 = rf_ref[:, ROUTE_W1:ROUTE_W1 + 1]
    out = h_ref[...] + (buf[slot, 0] * w0 + buf[slot, 1] * w1)
    o_ref[...] = _rms_rows(out, g_ref[...]) if final_norm else out


def _combine(yb, h, route_f, dest0, dest1, final_g, *, tm=COMBINE_TILE):
    t, d = h.shape
    g = jnp.ones((1, d), F32) if final_g is None else final_g.reshape(1, d)
    return pl.pallas_call(
        functools.partial(_combine_kernel, final_norm=final_g is not None),
        grid_spec=pltpu.PrefetchScalarGridSpec(
            num_scalar_prefetch=2,
            grid=(t // tm,),
            in_specs=[pl.BlockSpec(memory_space=pl.ANY),
                      pl.BlockSpec((tm, d), lambda i, d0, d1: (i, 0)),
                      pl.BlockSpec((tm, LANES), lambda i, d0, d1: (i, 0)),
                      pl.BlockSpec((1, d), lambda i, d0, d1: (0, 0))],
            out_specs=pl.BlockSpec((tm, d), lambda i, d0, d1: (i, 0)),
            scratch_shapes=[pltpu.VMEM((COMBINE_SLOTS, 2, tm, d), F32),
                            pltpu.SemaphoreType.DMA((COMBINE_SLOTS,))]),
        out_shape=jax.ShapeDtypeStruct((t, d), F32),
        compiler_params=_cparams(("arbitrary",)),
        name="moe_combine",
    )(dest0, dest1, yb, h, route_f, g)


def _moe(h, g, router_w, w1, w3, w2, final_g=None):
    t, d = h.shape
    route_i, route_f, counts_f = _router(h, g, router_w)
    counts = counts_f[0, :N_EXPERTS].astype(I32)
    padded = (counts + MOE_BLOCK - 1) // MOE_BLOCK * MOE_BLOCK
    pad_end = jnp.cumsum(padded)
    pad_start = pad_end - padded
    n_rows = (-(-(t * 2) // MOE_BLOCK) + N_EXPERTS) * MOE_BLOCK
    n_blocks = n_rows // MOE_BLOCK
    nu = MOE_UNIT_BLOCKS
    blocks_e = padded // MOE_BLOCK
    units_e = (blocks_e + nu - 1) // nu
    unit_end = jnp.cumsum(units_e)
    n_units = -(-n_blocks // nu) + N_EXPERTS
    uidx = jnp.arange(n_units, dtype=I32)
    unit_e = jnp.minimum(jnp.sum((uidx[:, None] >= unit_end[None, :]).astype(I32), axis=1), N_EXPERTS - 1)
    within = uidx - (unit_end - units_e)[unit_e]
    left = jnp.where(uidx < unit_end[N_EXPERTS - 1], blocks_e[unit_e] - nu * within, 0)
    first_blk = pad_start[unit_e] // MOE_BLOCK + nu * within
    slot_j = jnp.arange(nu, dtype=I32)[None, :]
    unit_v = (slot_j < left[:, None]).astype(I32)
    unit_b = jnp.where(unit_v > 0, first_blk[:, None] + slot_j, jnp.where(left > 0, first_blk, 0)[:, None])
    unit_v = unit_v.reshape(-1)
    unit_b = unit_b.reshape(-1).astype(I32)
    dest0 = pad_start[route_i[:, ROUTE_E0]] + route_i[:, ROUTE_RANK0]
    dest1 = pad_start[route_i[:, ROUTE_E1]] + route_i[:, ROUTE_RANK1]
    spare = pad_end[N_EXPERTS - 1] // MOE_BLOCK + jnp.arange(N_EXPERTS, dtype=I32)
    zero_start = jnp.concatenate([pad_end - MOE_BLOCK, jnp.minimum(spare, n_blocks - 1) * MOE_BLOCK])
    zero_flag = jnp.concatenate([padded > 0, spare < n_blocks]).astype(I32)
    xb = _dispatch(h, dest0, dest1, zero_start.astype(I32), zero_flag, n_rows)
    yb = _moe_ffn(xb, g, w1, w3, w2, unit_e, unit_b, unit_v,
                  zero_start[N_EXPERTS:].astype(I32), zero_flag[N_EXPERTS:], tf=MOE_HIDDEN_TILE)
    return _combine(yb, h, route_f, dest0, dest1, final_g)


def _pad_cols(w, width):
    return jnp.concatenate([w, jnp.zeros((w.shape[0], width - w.shape[1]), w.dtype)], axis=1)


def _split_w_in(w_in):
    bw = BRANCH_WIDTH
    d = w_in.shape[0]
    widths = (3 * bw, bw, GDN_HEADS, GDN_HEADS, 3 * bw + RWKV_DECAY_LORA + RWKV_ICLR_LORA + RWKV_GATE_LORA,
              bw, GLA_QW, GLA_QW, bw, bw, GLA_GATE_RANK, N_BRANCH * d)
    offs = [0]
    for w in widths:
        offs.append(offs[-1] + w)
    col = lambda i: w_in[:, offs[i]:offs[i + 1]]
    gdn = jnp.concatenate([col(0), col(1), _pad_cols(jnp.concatenate([col(2), col(3)], axis=1), LANES)], axis=1)
    rw = col(4)
    o = 3 * bw
    rwkv = jnp.concatenate([rw[:, :o],
                            _pad_cols(rw[:, o:o + RWKV_DECAY_LORA], LANES),
                            _pad_cols(rw[:, o + RWKV_DECAY_LORA:o + RWKV_DECAY_LORA + RWKV_ICLR_LORA], LANES),
                            rw[:, o + RWKV_DECAY_LORA + RWKV_ICLR_LORA:]], axis=1)
    pool = col(5)
    gla = jnp.concatenate([col(6), col(7), col(8), col(9), _pad_cols(col(10), LANES)], axis=1)
    gate = col(11)
    return tuple(_bf(w) for w in (gdn, rwkv, pool, gla, gate))


def _mixer(hn, h, layer, batch, seq, w_in, gdn_conv_w, gdn_a_log, gdn_dt_bias, gdn_norm_g,
           rwkv_mu, rwkv_w0, rwkv_w_up, rwkv_a0, rwkv_a_up, rwkv_g_up, rwkv_k_k, rwkv_k_a,
           rwkv_r_k, rwkv_ln_g, rwkv_ln_b, pool_w, pool_scale, gla_f_up, gla_f_bias, gla_norm_g,
           gate_bias, branch_proj, w_out):
    l = layer
    w_gdn, w_rwkv, w_pool, w_gla, w_gate = _split_w_in(w_in[l])
    tm = TOKEN_TILE
    p_gdn = _matmul(hn, w_gdn, tm=tm, tn=GDN_WIDTH, name="proj_gdn")
    p_rwkv = _matmul(hn, w_rwkv, tm=tm, tn=RWKV_WIDTH, name="proj_rwkv")
    p_pool = _matmul(hn, w_pool, tm=tm, tn=BRANCH_WIDTH, name="proj_pool")
    p_gla = _matmul(hn, w_gla, tm=tm, tn=GLA_WIDTH, name="proj_gla")
    kw = dict(batch=batch, seq=seq, tile=SEQ_TILE)
    y_gdn = _gdn_branch(p_gdn, gdn_conv_w[l], gdn_a_log[l], gdn_dt_bias[l], gdn_norm_g[l], **kw)
    y_rwkv = _rwkv_branch(p_rwkv, rwkv_mu[l], rwkv_w0[l], rwkv_w_up[l], rwkv_a0[l], rwkv_a_up[l],
                          rwkv_g_up[l], rwkv_k_k[l], rwkv_k_a[l], rwkv_r_k[l].reshape(-1),
                          rwkv_ln_g[l], rwkv_ln_b[l], **kw)
    y_pool = _pool_branch(p_pool, pool_w[l], pool_scale[l], **kw)
    y_gla = _gla_branch(p_gla, gla_f_up[l], gla_f_bias[l], gla_norm_g[l], **kw)
    mixed = _merge(hn, (y_gdn, y_rwkv, y_pool, y_gla), branch_proj[l], w_gate, gate_bias[l],
                   tm=TOKEN_TILE, tn=MERGE_COL_TILE)
    return _matmul(mixed, _bf(w_out[l]), tm=TOKEN_TILE, tn=OUT_PROJ_COL_TILE, residual=h, name="out_proj")


def kernel(x, norm1_g, w_in, gdn_conv_w, gdn_a_log, gdn_dt_bias, gdn_norm_g, rwkv_mu, rwkv_w0, rwkv_w_up, rwkv_a0, rwkv_a_up, rwkv_g_up, rwkv_k_k, rwkv_k_a, rwkv_r_k, rwkv_ln_g, rwkv_ln_b, pool_w, pool_scale, gla_f_up, gla_f_bias, gla_norm_g, gate_bias, branch_proj, w_out, norm2_g, ffn_w1, ffn_w3, ffn_w2, moe_router, moe_w1, moe_w3, moe_w2, final_norm_g):
    batch, seq, d = x.shape
    depth = norm1_g.shape[0]
    h = x.reshape(batch * seq, d)
    hn = None
    for layer in range(depth):
        if hn is None:
            hn = _rmsnorm(h, norm1_g[layer], BF16)
        h = _mixer(hn, h, layer, batch, seq, w_in, gdn_conv_w, gdn_a_log, gdn_dt_bias, gdn_norm_g,
                   rwkv_mu, rwkv_w0, rwkv_w_up, rwkv_a0, rwkv_a_up, rwkv_g_up, rwkv_k_k, rwkv_k_a,
                   rwkv_r_k, rwkv_ln_g, rwkv_ln_b, pool_w, pool_scale, gla_f_up, gla_f_bias,
                   gla_norm_g, gate_bias, branch_proj, w_out)
        hn = None
        i = layer // 2
        last = layer == depth - 1
        if layer % 2 == 0:
            next_g, next_dtype = (final_norm_g, F32) if last else (norm1_g[layer + 1], BF16)
            h, nxt = _ffn(h, norm2_g[layer], next_g, next_dtype, _bf(ffn_w1[i]), _bf(ffn_w3[i]), _bf(ffn_w2[i]),
                          tm=TOKEN_TILE, tf=FFN_HIDDEN_TILE)
            if last:
                h = nxt
            else:
                hn = nxt
        else:
            h = _moe(h, norm2_g[layer], moe_router[i], moe_w1[i], moe_w3[i], moe_w2[i],
                     final_g=final_norm_g if last else None)
    return h.reshape(batch, seq, d)
```
